```python
import jax
import jax.numpy as jnp
from jax import lax
import numpy as np

D_MODEL = 1024
BATCH = 16
SEQ = 2048
DEPTH = 2

GRID_W = 64
CTX_LEN = 256

CONV_DIM = 512
CONV_WIDTH = 31
NA_HEADS = 8
NA_HEAD_DIM = 64
NA_DIM = NA_HEADS * NA_HEAD_DIM
NA_KH_MAX = 8
NA_KW = 16
C_HEADS = 16
C_KV_HEADS = 4
C_GROUP = C_HEADS // C_KV_HEADS
C_HEAD_DIM = 64
C_WINDOW = 128
C_BLOCK = 128
ROPE_BASE = 10000.0
N_GROUPS = 4
EXPERTS_PER_GROUP = 4
N_EXPERTS = N_GROUPS * EXPERTS_PER_GROUP
TOP_K = 2
D_EXPERT = 256

LN_EPS = 1e-5
NEG_INF = -1e30

kernel_name = 'hybrid_conv_natten_swa_hmoe_dit'


def layer_norm(x, g, b):
    xf = x.astype(jnp.float32)
    mu = jnp.mean(xf, axis=-1, keepdims=True)
    var = jnp.mean(jnp.square(xf - mu), axis=-1, keepdims=True)
    return ((xf - mu) * lax.rsqrt(var + LN_EPS)).astype(x.dtype) * g + b


def axial_rope_tables(n_tokens, head_dim):
    t = jnp.arange(n_tokens)
    row = (t // GRID_W).astype(jnp.float32)
    col = (t % GRID_W).astype(jnp.float32)
    axis_dim = head_dim // 2
    inv_freq = ROPE_BASE ** (-jnp.arange(0, axis_dim, 2, dtype=jnp.float32) / axis_dim)
    ang = jnp.concatenate([row[:, None] * inv_freq, col[:, None] * inv_freq], axis=-1)
    return jnp.cos(ang), jnp.sin(ang)


def apply_axial_rope(x, cos, sin):
    xr = x.astype(jnp.float32).reshape(x.shape[:-1] + (2, 2, -1))
    bshape = (1, x.shape[1]) + (1,) * (x.ndim - 3) + (2, cos.shape[-1] // 2)
    cs = cos.reshape(bshape)
    sn = sin.reshape(bshape)
    x1 = xr[..., 0, :]
    x2 = xr[..., 1, :]
    out = jnp.stack([x1 * cs - x2 * sn, x2 * cs + x1 * sn], axis=-2)
    return out.reshape(x.shape).astype(x.dtype)


def depthwise_conv(h, w, b):
    half = w.shape[0] // 2
    out = lax.conv_general_dilated(h, w[:, None, :], window_strides=(1,), padding=[(half, half)],
                                   dimension_numbers=('NWC', 'WIO', 'NWC'),
                                   feature_group_count=h.shape[-1])
    return out + b


def conformer_conv(a, conv_w, conv_b, g, b):
    h = a[..., :CONV_DIM] * jax.nn.sigmoid(a[..., CONV_DIM:])
    h = layer_norm(depthwise_conv(h, conv_w, conv_b), g, b)
    return jax.nn.silu(h)


def context_attention(q, k, v, sink=None):
    s = jnp.einsum('bqhgd,bkhd->bhgqk', q, k).astype(jnp.float32) * (q.shape[-1] ** -0.5)
    if sink is not None:
        sink_col = jnp.broadcast_to(sink.astype(jnp.float32)[None, :, :, None, None], s.shape[:-1] + (1,))
        s = jnp.concatenate([s, sink_col], axis=-1)
    p = jax.nn.softmax(s, axis=-1)[..., :k.shape[1]].astype(v.dtype)
    return jnp.einsum('bhgqk,bkhd->bqhgd', p, v)


def neighbourhood_attention(q, k, v, k_ctx, v_ctx, rpb, rows):
    bsz = q.shape[0]
    kh = min(NA_KH_MAX, rows)
    n_win = kh * GRID_W
    grid = lambda t: t.reshape(bsz, rows, GRID_W, NA_HEADS, NA_HEAD_DIM)
    qg, kg, vg = grid(q), grid(k), grid(v)
    r = np.arange(rows)
    row_start = np.clip(r - kh // 2, 0, rows - kh)
    row_off = row_start[:, None] + np.arange(kh)[None, :] - r[:, None] + NA_KH_MAX - 1
    j = np.arange(GRID_W)
    col_start = np.clip(j - NA_KW // 2, 0, GRID_W - NA_KW)
    col_in = (j[None, :] >= col_start[:, None]) & (j[None, :] < col_start[:, None] + NA_KW)
    col_off = np.clip(j[None, :] - j[:, None], -(NA_KW - 1), NA_KW - 1) + NA_KW - 1
    bias = rpb.astype(jnp.float32)[:, row_off][:, :, :, col_off]
    bias = jnp.where(col_in[None, None, None], bias, NEG_INF)
    bias = bias.transpose(1, 0, 3, 2, 4).reshape(rows, NA_HEADS, GRID_W, n_win)
    scale = NA_HEAD_DIM ** -0.5

    def row_step(args):
        start, q_row, b_row = args
        k_blk = lax.dynamic_slice_in_dim(kg, start, kh, axis=1).reshape(bsz, n_win, NA_HEADS, NA_HEAD_DIM)
        v_blk = lax.dynamic_slice_in_dim(vg, start, kh, axis=1).reshape(bsz, n_win, NA_HEADS, NA_HEAD_DIM)
        s_lat = jnp.einsum('bqhd,bkhd->bhqk', q_row, k_blk).astype(jnp.float32) * scale + b_row
        s_ctx = jnp.einsum('bqhd,bkhd->bhqk', q_row, k_ctx).astype(jnp.float32) * scale
        p = jax.nn.softmax(jnp.concatenate([s_lat, s_ctx], axis=-1), axis=-1).astype(v.dtype)
        return (jnp.einsum('bhqk,bkhd->bqhd', p[..., :n_win], v_blk)
                + jnp.einsum('bhqk,bkhd->bqhd', p[..., n_win:], v_ctx))

    out = lax.map(row_step, (jnp.asarray(row_start, jnp.int32), jnp.moveaxis(qg, 1, 0), bias))
    return jnp.moveaxis(out, 0, 1).reshape(bsz, rows * GRID_W, NA_HEADS, NA_HEAD_DIM)


def conv_na_mixer(u_ctx, u_lat, w_in, b_in, conv_w, conv_b, cln_g, cln_b, rpb, w_out, b_out, rows, need_ctx):
    bsz, n_lat, _ = u_lat.shape
    q0 = 2 * CONV_DIM
    kv0 = q0 + NA_DIM
    heads = lambda t: t.reshape(t.shape[:-1] + (NA_HEADS, NA_HEAD_DIM))
    a_lat, q_lat, k_lat, v_lat = jnp.split(u_lat @ w_in + b_in, [q0, kv0, kv0 + NA_DIM], axis=-1)
    k_ctx, v_ctx = jnp.split(u_ctx @ w_in[:, kv0:] + b_in[kv0:], 2, axis=-1)
    k_ctx, v_ctx = heads(k_ctx), heads(v_ctx)
    conv_lat = conformer_conv(a_lat, conv_w, conv_b, cln_g, cln_b)
    na_lat = neighbourhood_attention(heads(q_lat), heads(k_lat), heads(v_lat), k_ctx, v_ctx, rpb, rows)
    o_lat = jnp.concatenate([conv_lat, na_lat.reshape(bsz, n_lat, NA_DIM)], axis=-1) @ w_out + b_out
    o_ctx = None
    if need_ctx:
        a_ctx, q_ctx = jnp.split(u_ctx @ w_in[:, :kv0] + b_in[:kv0], [q0], axis=-1)
        conv_ctx = conformer_conv(a_ctx, conv_w, conv_b, cln_g, cln_b)
        na_ctx = context_attention(heads(q_ctx)[:, :, :, None, :], k_ctx, v_ctx)
        o_ctx = jnp.concatenate([conv_ctx, na_ctx.reshape(bsz, -1, NA_DIM)], axis=-1) @ w_out + b_out
    return o_ctx, o_lat


def banded_window_attention(q, k, v, k_ctx, v_ctx, sink):
    bsz, n, hkv, g, d = q.shape
    nb = n // C_BLOCK
    span = C_BLOCK + 2 * C_WINDOW
    pad = [(0, 0), (C_WINDOW, C_WINDOW), (0, 0), (0, 0)]
    kp = jnp.pad(k, pad)
    vp = jnp.pad(v, pad)
    q_blocks = jnp.moveaxis(q.reshape(bsz, nb, C_BLOCK, hkv, g, d), 1, 0)
    scale = d ** -0.5
    sink_col = jnp.broadcast_to(sink.astype(jnp.float32)[None, :, :, None, None], (bsz, hkv, g, C_BLOCK, 1))

    def block_step(args):
        b_idx, q_blk = args
        start = b_idx * C_BLOCK
        k_blk = lax.dynamic_slice_in_dim(kp, start, span, axis=1)
        v_blk = lax.dynamic_slice_in_dim(vp, start, span, axis=1)
        k_pos = start - C_WINDOW + jnp.arange(span)
        q_pos = start + jnp.arange(C_BLOCK)
        valid = (k_pos >= 0)[None, :] & (k_pos < n)[None, :] & (jnp.abs(q_pos[:, None] - k_pos[None, :]) <= C_WINDOW)
        s_lat = jnp.einsum('bqhgd,bkhd->bhgqk', q_blk, k_blk).astype(jnp.float32) * scale
        s_lat = jnp.where(valid, s_lat, NEG_INF)
        s_ctx = jnp.einsum('bqhgd,bkhd->bhgqk', q_blk, k_ctx).astype(jnp.float32) * scale
        p = jax.nn.softmax(jnp.concatenate([s_lat, s_ctx, sink_col], axis=-1), axis=-1).astype(v.dtype)
        return (jnp.einsum('bhgqk,bkhd->bqhgd', p[..., :span], v_blk)
                + jnp.einsum('bhgqk,bkhd->bqhgd', p[..., span:span + k_ctx.shape[1]], v_ctx))

    out = lax.map(block_step, (jnp.arange(nb, dtype=jnp.int32), q_blocks))
    return jnp.moveaxis(out, 0, 1).reshape(bsz, n, hkv, g, d)


def window_gqa_mixer(u_ctx, u_lat, w_in, sink, w_out, rope_cos, rope_sin, need_ctx):
    bsz, n_lat, _ = u_lat.shape
    q_cols = C_HEADS * C_HEAD_DIM
    kv_cols = C_KV_HEADS * C_HEAD_DIM
    qh = lambda t: t.reshape(t.shape[:-1] + (C_KV_HEADS, C_GROUP, C_HEAD_DIM))
    kvh = lambda t: t.reshape(t.shape[:-1] + (C_KV_HEADS, C_HEAD_DIM))
    q_lat, k_lat, v_lat = jnp.split(u_lat @ w_in, [q_cols, q_cols + kv_cols], axis=-1)
    k_ctx, v_ctx = jnp.split(u_ctx @ w_in[:, q_cols:], [kv_cols], axis=-1)
    k_ctx, v_ctx = kvh(k_ctx), kvh(v_ctx)
    q_lat = apply_axial_rope(qh(q_lat), rope_cos, rope_sin)
    k_lat = apply_axial_rope(kvh(k_lat), rope_cos, rope_sin)
    sink_hg = sink.reshape(C_KV_HEADS, C_GROUP)
    o_lat = banded_window_attention(q_lat, k_lat, kvh(v_lat), k_ctx, v_ctx, sink_hg)
    o_lat = o_lat.reshape(bsz, n_lat, q_cols) @ w_out
    o_ctx = None
    if need_ctx:
        q_ctx = qh(u_ctx @ w_in[:, :q_cols])
        o_ctx = context_attention(q_ctx, k_ctx, v_ctx, sink_hg).reshape(bsz, -1, q_cols) @ w_out
    return o_ctx, o_lat


def hier_moe(u, router_g, router_e, w_gate, w_up, w_down):
    t = u.shape[0]
    g_prob = jax.nn.softmax((u @ router_g).astype(jnp.float32), axis=-1)
    g_p, g_idx = lax.top_k(g_prob, 1)
    e_logits = (u @ router_e).astype(jnp.float32).reshape(t, N_GROUPS, EXPERTS_PER_GROUP)
    e_logits = jnp.take_along_axis(e_logits, g_idx[:, :, None], axis=1)[:, 0]
    e_top, e_idx = lax.top_k(e_logits, TOP_K)
    e_w = jax.nn.softmax(e_top, axis=-1) * g_p
    within = jnp.einsum('tk,tke->te', e_w, jax.nn.one_hot(e_idx, EXPERTS_PER_GROUP, dtype=jnp.float32))
    gates = (jax.nn.one_hot(g_idx[:, 0], N_GROUPS, dtype=jnp.float32)[:, :, None] * within[:, None, :]).astype(u.dtype)
    wg = w_gate.reshape(N_GROUPS, EXPERTS_PER_GROUP, D_MODEL, D_EXPERT)
    wu = w_up.reshape(N_GROUPS, EXPERTS_PER_GROUP, D_MODEL, D_EXPERT)
    wd = w_down.reshape(N_GROUPS, EXPERTS_PER_GROUP, D_EXPERT, D_MODEL)
    y = jnp.zeros_like(u)
    for g in range(N_GROUPS):
        h = jax.nn.silu(jnp.einsum('td,edf->tef', u, wg[g])) * jnp.einsum('td,edf->tef', u, wu[g])
        y = y + jnp.einsum('tef,efd->td', h * gates[:, g, :, None], wd[g])
    return y


def setup_inputs(seed: int = 0) -> dict:
    key = jax.random.key(seed)
    ks = iter(jax.random.split(key, 32))
    f32 = jnp.float32
    n_even = (DEPTH + 1) // 2
    n_odd = DEPTH // 2
    beta = (8.0 * DEPTH) ** -0.25
    ab_cols = 2 * CONV_DIM + 3 * NA_DIM
    gqa_cols = (C_HEADS + 2 * C_KV_HEADS) * C_HEAD_DIM

    def nrm(shape, scale):
        return jax.random.normal(next(ks), shape, f32) * scale

    return {
        'x': nrm((BATCH, SEQ, D_MODEL), 1.0),
        'c': nrm((BATCH, D_MODEL), 1.0),
        'ctx': nrm((BATCH, CTX_LEN, D_MODEL), 1.0),
        'c_ctx': nrm((D_MODEL,), 1.0),
        'ada_w': nrm((DEPTH, D_MODEL, 6 * D_MODEL), 0.5 * D_MODEL ** -0.5),
        'ada_b': nrm((DEPTH, 6 * D_MODEL), 0.02),
        'ln_g': 1.0 + nrm((DEPTH, 2, D_MODEL), 0.02),
        'ln_b': nrm((DEPTH, 2, D_MODEL), 0.02),
        'ab_w_in': nrm((n_even, D_MODEL, ab_cols), D_MODEL ** -0.5),
        'ab_b_in': nrm((n_even, ab_cols), 0.02),
        'conv_w': nrm((n_even, CONV_WIDTH, CONV_DIM), CONV_WIDTH ** -0.5),
        'conv_b': nrm((n_even, CONV_DIM), 0.02),
        'conv_ln_g': 1.0 + nrm((n_even, CONV_DIM), 0.02),
        'conv_ln_b': nrm((n_even, CONV_DIM), 0.02),
        'na_rpb': nrm((n_even, NA_HEADS, 2 * NA_KH_MAX - 1, 2 * NA_KW - 1), 0.1),
        'ab_w_out': nrm((n_even, CONV_DIM + NA_DIM, D_MODEL), beta * (CONV_DIM + NA_DIM) ** -0.5),
        'ab_b_out': nrm((n_even, D_MODEL), 0.02),
        'gqa_w_in': nrm((n_odd, D_MODEL, gqa_cols), D_MODEL ** -0.5),
        'gqa_sink': nrm((n_odd, C_HEADS), 0.5),
        'gqa_w_out': nrm((n_odd, C_HEADS * C_HEAD_DIM, D_MODEL), beta * (C_HEADS * C_HEAD_DIM) ** -0.5),
        'router_group': nrm((DEPTH, D_MODEL, N_GROUPS), D_MODEL ** -0.5),
        'router_expert': nrm((DEPTH, D_MODEL, N_EXPERTS), D_MODEL ** -0.5),
        'exp_w_gate': nrm((DEPTH, N_EXPERTS, D_MODEL, D_EXPERT), D_MODEL ** -0.5),
        'exp_w_up': nrm((DEPTH, N_EXPERTS, D_MODEL, D_EXPERT), D_MODEL ** -0.5),
        'exp_w_down': nrm((DEPTH, N_EXPERTS, D_EXPERT, D_MODEL), beta * D_EXPERT ** -0.5),
    }


def reference(x, c, ctx, c_ctx, ada_w, ada_b, ln_g, ln_b, ab_w_in, ab_b_in, conv_w, conv_b, conv_ln_g,
              conv_ln_b, na_rpb, ab_w_out, ab_b_out, gqa_w_in, gqa_sink, gqa_w_out, router_group,
              router_expert, exp_w_gate, exp_w_up, exp_w_down):
    alpha = (2.0 * DEPTH) ** 0.25
    bsz, n_lat, _ = x.shape
    n_ctx = ctx.shape[1]
    rows = n_lat // GRID_W
    rope_cos, rope_sin = axial_rope_tables(n_lat, C_HEAD_DIM)
    silu_c = jax.nn.silu(c)
    silu_cc = jax.nn.silu(c_ctx)
    h_lat, h_ctx = x, ctx
    for i in range(DEPTH):
        j = i // 2
        need_ctx = i < DEPTH - 1
        m_lat = jnp.split((silu_c @ ada_w[i] + ada_b[i])[:, None, :], 6, axis=-1)
        m_ctx = jnp.split(silu_cc @ ada_w[i] + ada_b[i], 6, axis=-1)
        u_lat = h_lat * (1.0 + m_lat[1]) + m_lat[0]
        u_ctx = h_ctx * (1.0 + m_ctx[1]) + m_ctx[0]
        if i % 2 == 0:
            o_ctx, o_lat = conv_na_mixer(u_ctx, u_lat, ab_w_in[j], ab_b_in[j], conv_w[j], conv_b[j],
                                         conv_ln_g[j], conv_ln_b[j], na_rpb[j], ab_w_out[j], ab_b_out[j],
                                         rows, need_ctx)
        else:
            o_ctx, o_lat = window_gqa_mixer(u_ctx, u_lat, gqa_w_in[j], gqa_sink[j], gqa_w_out[j],
                                            rope_cos, rope_sin, need_ctx)
        h_lat = layer_norm(alpha * h_lat + m_lat[2] * o_lat, ln_g[i, 0], ln_b[i, 0])
        t_lat = h_lat * (1.0 + m_lat[4]) + m_lat[3]
        if need_ctx:
            h_ctx = layer_norm(alpha * h_ctx + m_ctx[2] * o_ctx, ln_g[i, 0], ln_b[i, 0])
            t_ctx = h_ctx * (1.0 + m_ctx[4]) + m_ctx[3]
            tokens = jnp.concatenate([t_ctx, t_lat], axis=1)
        else:
            tokens = t_lat
        y = hier_moe(tokens.reshape(-1, D_MODEL), router_group[i], router_expert[i], exp_w_gate[i],
                     exp_w_up[i], exp_w_down[i]).reshape(bsz, -1, D_MODEL)
        h_lat = layer_norm(alpha * h_lat + m_lat[5] * y[:, y.shape[1] - n_lat:], ln_g[i, 1], ln_b[i, 1])
        if need_ctx:
            h_ctx = layer_norm(alpha * h_ctx + m_ctx[5] * y[:, :n_ctx], ln_g[i, 1], ln_b[i, 1])
    return h_lat
```

```python
import functools

import numpy as np
import jax
import jax.numpy as jnp
from jax import lax
from jax.experimental import pallas as pl
from jax.experimental.pallas import tpu as pltpu

F32 = jnp.float32
BF16 = jnp.bfloat16

DEPTH = 2
GRID_W = 64
CONV_DIM = 512
CONV_WIDTH = 31
NA_HEADS = 8
NA_HEAD_DIM = 64
NA_DIM = NA_HEADS * NA_HEAD_DIM
NA_KH = 8
NA_KW = 16
C_HEADS = 16
C_KV_HEADS = 4
C_GROUP = C_HEADS // C_KV_HEADS
C_HEAD_DIM = 64
C_WINDOW = 128
C_BLOCK = 128
ROPE_BASE = 10000.0
N_GROUPS = 4
EXPERTS_PER_GROUP = 4
N_EXPERTS = N_GROUPS * EXPERTS_PER_GROUP
D_EXPERT = 256
LN_EPS = 1e-5
NEG_INF = -1e30

LANES = 128
SUBLANES = 8
VMEM_LIMIT_BYTES = 56 * 1024 * 1024

TOKEN_TILE = 512
MOD_ROWS = 24
NA_ROWS_PER_STEP = 4
NA_UNION_ROWS = NA_KH + NA_ROWS_PER_STEP
CONV_CHUNK = 32
CONV_HALO = 16
ROUTER_LANES = LANES
GATE_LANE0 = N_GROUPS


def _cparams(semantics):
    return pltpu.CompilerParams(dimension_semantics=semantics, vmem_limit_bytes=VMEM_LIMIT_BYTES)


def _dot(a, b):
    return jnp.dot(a, b, preferred_element_type=F32)


def _dot_nt(a, b):
    return lax.dot_general(a, b, (((1,), (1,)), ((), ())), preferred_element_type=F32)


def _split_bf16(a):
    hi = a.astype(BF16)
    lo = (a - hi.astype(F32)).astype(BF16)
    return hi, lo


def _dot3(a, b):
    a_hi, a_lo = _split_bf16(a)
    b_hi, b_lo = _split_bf16(b)
    return _dot(a_hi, b_hi) + (_dot(a_lo, b_hi) + _dot(a_hi, b_lo))


def _layer_norm(x, g, b):
    mu = jnp.mean(x, axis=-1, keepdims=True)
    xc = x - mu
    var = jnp.mean(xc * xc, axis=-1, keepdims=True)
    return xc * lax.rsqrt(var + LN_EPS) * g + b


def _silu(x):
    return x * jax.nn.sigmoid(x)


def _mod_kernel(cc_ref, w_ref, b_ref, o_ref):
    o_ref[0] = _dot3(_silu(cc_ref[...]), w_ref[0]) + b_ref[0]


def _modulation(cc, ada_w, ada_b):
    depth, d, n = ada_w.shape
    tn = n // 4
    return pl.pallas_call(
        _mod_kernel,
        grid=(depth, n // tn),
        in_specs=[
            pl.BlockSpec((MOD_ROWS, d), lambda i, j: (0, 0)),
            pl.BlockSpec((1, d, tn), lambda i, j: (i, 0, j)),
            pl.BlockSpec((1, 1, tn), lambda i, j: (i, 0, j)),
        ],
        out_specs=pl.BlockSpec((1, MOD_ROWS, tn), lambda i, j: (i, 0, j)),
        out_shape=jax.ShapeDtypeStruct((depth, MOD_ROWS, n), F32),
        compiler_params=_cparams(("arbitrary", "arbitrary")),
        name="adaln_mod",
    )(cc, ada_w, ada_b.reshape(depth, 1, n))


def _mod_spec(rows_per_group, group0, tm):
    d = None
    return lambda i: (group0 + (i * tm) // rows_per_group, 0, 0)


def _modulate(h, m, shift_row, scale_row):
    return h * (1.0 + m[scale_row:scale_row + 1]) + m[shift_row:shift_row + 1]


def _inproj_ab_kernel(h_ref, mod_ref, w_ref, b_ref, g_ref, q_ref, k_ref, v_ref):
    u = _modulate(h_ref[...], mod_ref[0], 0, 1).astype(BF16)
    c = CONV_DIM
    za = _dot(u, w_ref[:, 0:c]) + b_ref[:, 0:c]
    zb = _dot(u, w_ref[:, c:2 * c]) + b_ref[:, c:2 * c]
    g_ref[...] = za * jax.nn.sigmoid(zb)
    q0 = 2 * c
    zq = _dot(u, w_ref[:, q0:q0 + NA_DIM]) + b_ref[:, q0:q0 + NA_DIM]
    q_ref[...] = (zq * (NA_HEAD_DIM ** -0.5)).astype(BF16)
    k0 = q0 + NA_DIM
    k_ref[...] = (_dot(u, w_ref[:, k0:k0 + NA_DIM]) + b_ref[:, k0:k0 + NA_DIM]).astype(BF16)
    v0 = k0 + NA_DIM
    v_ref[...] = (_dot(u, w_ref[:, v0:v0 + NA_DIM]) + b_ref[:, v0:v0 + NA_DIM]).astype(BF16)


def _inproj_ab(h, mod, w, b, rows_per_group, group0):
    t, d = h.shape
    n = w.shape[1]
    tm = TOKEN_TILE
    row = lambda i: (i, 0)
    return pl.pallas_call(
        _inproj_ab_kernel,
        grid=(t // tm,),
        in_specs=[
            pl.BlockSpec((tm, d), row),
            pl.BlockSpec((1, 6, d), _mod_spec(rows_per_group, group0, tm)),
            pl.BlockSpec((d, n), lambda i: (0, 0)),
            pl.BlockSpec((1, n), lambda i: (0, 0)),
        ],
        out_specs=[
            pl.BlockSpec((tm, CONV_DIM), row),
            pl.BlockSpec((tm, NA_DIM), row),
            pl.BlockSpec((tm, NA_DIM), row),
            pl.BlockSpec((tm, NA_DIM), row),
        ],
        out_shape=[
            jax.ShapeDtypeStruct((t, CONV_DIM), F32),
            jax.ShapeDtypeStruct((t, NA_DIM), BF16),
            jax.ShapeDtypeStruct((t, NA_DIM), BF16),
            jax.ShapeDtypeStruct((t, NA_DIM), BF16),
        ],
        compiler_params=_cparams(("parallel",)),
        name="inproj_conv_na",
    )(h, mod, w, b.reshape(1, n))


def _conv_kernel(g_ref, w_ref, cb_ref, lg_ref, lb_ref, o_ref, pad_ref, *, seq):
    zeros = jnp.zeros((CONV_HALO, CONV_DIM), F32)
    pad_ref[0:CONV_HALO, :] = zeros
    pad_ref[CONV_HALO + seq:2 * CONV_HALO + seq, :] = zeros
    pad_ref[CONV_HALO:CONV_HALO + seq, :] = g_ref[0]
    first = CONV_HALO - CONV_WIDTH // 2
    ext = CONV_CHUNK + SUBLANES

    def chunk(i, carry):
        r0 = pl.multiple_of(i * CONV_CHUNK, CONV_CHUNK)
        acc = jnp.zeros((CONV_CHUNK, CONV_DIM), F32) + cb_ref[...]
        for res in range(SUBLANES):
            part = None
            for base in range(0, first + CONV_WIDTH, SUBLANES):
                tap = base + res - first
                if 0 <= tap < CONV_WIDTH:
                    term = pad_ref[pl.ds(r0 + base, ext), :] * w_ref[tap:tap + 1, :]
                    part = term if part is None else part + term
            acc = acc + part[res:res + CONV_CHUNK]
        y = _layer_norm(acc, lg_ref[...], lb_ref[...])
        o_ref[0, pl.ds(r0, CONV_CHUNK), :] = _silu(y).astype(BF16)
        return carry

    lax.fori_loop(0, seq // CONV_CHUNK, chunk, 0)


def _conv_module(g, conv_w, conv_b, ln_g, ln_b):
    bsz, seq, c = g.shape
    vec = lambda i: (0, 0)
    return pl.pallas_call(
        functools.partial(_conv_kernel, seq=seq),
        grid=(bsz,),
        in_specs=[
            pl.BlockSpec((1, seq, c), lambda i: (i, 0, 0)),
            pl.BlockSpec((CONV_WIDTH, c), vec),
            pl.BlockSpec((1, c), vec),
            pl.BlockSpec((1, c), vec),
            pl.BlockSpec((1, c), vec),
        ],
        out_specs=pl.BlockSpec((1, seq, c), lambda i: (i, 0, 0)),
        out_shape=jax.ShapeDtypeStruct((bsz, seq, c), BF16),
        scratch_shapes=[pltpu.VMEM((seq + 2 * CONV_HALO, c), F32)],
        compiler_params=_cparams(("parallel",)),
        name="conv_module",
    )(g, conv_w, conv_b.reshape(1, c), ln_g.reshape(1, c), ln_b.reshape(1, c))


def _na_bias_tables(rpb, rows):
    rq, ru = NA_ROWS_PER_STEP, NA_UNION_ROWS
    n_steps = rows // rq
    qr = np.repeat(np.arange(rq), GRID_W)[:, None]
    qc = np.tile(np.arange(GRID_W), rq)[:, None]
    kr = np.repeat(np.arange(ru), GRID_W)[None, :]
    kc = np.tile(np.arange(GRID_W), ru)[None, :]
    col_start = np.clip(qc - NA_KW // 2, 0, GRID_W - NA_KW)
    col_in = (kc >= col_start) & (kc < col_start + NA_KW)
    col_off = np.clip(kc - qc, -(NA_KW - 1), NA_KW - 1) + NA_KW - 1
    geoms, types = [], []
    for step in range(n_steps):
        r = step * rq + qr
        key_row = _na_union_start(step, rows) + kr
        row_start = np.clip(r - NA_KH // 2, 0, rows - NA_KH)
        row_in = (key_row >= row_start) & (key_row < row_start + NA_KH)
        row_off = np.clip(key_row - r + NA_KH - 1, 0, 2 * NA_KH - 2)
        geom = (np.broadcast_to(row_off, (rq * GRID_W, ru * GRID_W)), row_in & col_in)
        for t, (ro, ok) in enumerate(geoms):
            if np.array_equal(ro[ok], geom[0][geom[1]]) and np.array_equal(ok, geom[1]):
                types.append(t)
                break
        else:
            types.append(len(geoms))
            geoms.append(geom)
    tables = []
    for ro, ok in geoms:
        bias = rpb.astype(F32)[:, ro, np.broadcast_to(col_off, ro.shape)]
        tables.append(jnp.where(ok[None], bias, NEG_INF))
    return jnp.stack(tables), np.asarray(types, np.int32)


def _na_union_start(step, rows):
    return np.clip(step * NA_ROWS_PER_STEP - NA_KH // 2, 0, rows - NA_UNION_ROWS)


def _softmax_pv(scores, values):
    m = functools.reduce(jnp.maximum, [jnp.max(s, axis=1, keepdims=True) for s in scores])
    ps = [jnp.exp(s - m) for s in scores]
    denom = functools.reduce(jnp.add, [jnp.sum(p, axis=1, keepdims=True) for p in ps])
    out = functools.reduce(jnp.add, [_dot(p.astype(BF16), v) for p, v in zip(ps, values)])
    return out / denom


def _na_kernel(type_ref, q_ref, k_ref, v_ref, kc_ref, vc_ref, bias_ref, o_ref, *, rows):
    del type_ref
    step = pl.program_id(1)
    start_row = jnp.clip(step * NA_ROWS_PER_STEP - NA_KH // 2, 0, rows - NA_UNION_ROWS)
    start = pl.multiple_of(start_row * GRID_W, GRID_W)
    n_win = NA_UNION_ROWS * GRID_W
    for h in range(NA_HEADS):
        sl = slice(h * NA_HEAD_DIM, (h + 1) * NA_HEAD_DIM)
        qh = q_ref[0, :, sl]
        s_lat = _dot_nt(qh, k_ref[0, pl.ds(start, n_win), sl]) + bias_ref[0, h]
        s_ctx = _dot_nt(qh, kc_ref[0, :, sl])
        out = _softmax_pv([s_lat, s_ctx], [v_ref[0, pl.ds(start, n_win), sl], vc_ref[0, :, sl]])
        o_ref[0, :, sl] = out.astype(BF16)


def _neighbourhood_attention(q, k, v, k_ctx, v_ctx, bias, types):
    bsz, seq, dim = q.shape
    n_ctx = k_ctx.shape[1]
    rows = seq // GRID_W
    tq = NA_ROWS_PER_STEP * GRID_W
    n_win = NA_UNION_ROWS * GRID_W
    full = lambda b, s, t: (b, 0, 0)
    grid_spec = pltpu.PrefetchScalarGridSpec(
        num_scalar_prefetch=1,
        grid=(bsz, seq // tq),
        in_specs=[
            pl.BlockSpec((1, tq, dim), lambda b, s, t: (b, s, 0)),
            pl.BlockSpec((1, seq, dim), full),
            pl.BlockSpec((1, seq, dim), full),
            pl.BlockSpec((1, n_ctx, dim), full),
            pl.BlockSpec((1, n_ctx, dim), full),
            pl.BlockSpec((1, NA_HEADS, tq, n_win), lambda b, s, t: (t[s], 0, 0, 0)),
        ],
        out_specs=pl.BlockSpec((1, tq, dim), lambda b, s, t: (b, s, 0)),
    )
    return pl.pallas_call(
        functools.partial(_na_kernel, rows=rows),
        grid_spec=grid_spec,
        out_shape=jax.ShapeDtypeStruct((bsz, seq, dim), BF16),
        compiler_params=_cparams(("parallel", "arbitrary")),
        name="neighbourhood_attn",
    )(jnp.asarray(types), q, k, v, k_ctx, v_ctx, bias)


def _ctx_attn_kernel(q_ref, k_ref, v_ref, o_ref):
    for h in range(NA_HEADS):
        sl = slice(h * NA_HEAD_DIM, (h + 1) * NA_HEAD_DIM)
        s = _dot_nt(q_ref[0, :, sl], k_ref[0, :, sl])
        o_ref[0, :, sl] = _softmax_pv([s], [v_ref[0, :, sl]]).astype(BF16)


def _context_attention(q, k, v):
    bsz, n, dim = q.shape
    spec = pl.BlockSpec((1, n, dim), lambda b: (b, 0, 0))
    return pl.pallas_call(
        _ctx_attn_kernel,
        grid=(bsz,),
        in_specs=[spec, spec, spec],
        out_specs=spec,
        out_shape=jax.ShapeDtypeStruct((bsz, n, dim), BF16),
        compiler_params=_cparams(("parallel",)),
        name="context_attn",
    )(q, k, v)


def _rope(x, cos, sin_signed):
    n = x.shape[1]
    half = C_HEAD_DIM // 4
    lane = lax.broadcasted_iota(jnp.int32, x.shape, 1)
    partner = jnp.where(lane % (2 * half) < half, pltpu.roll(x, n - half, 1), pltpu.roll(x, half, 1))
    reps = n // LANES
    cos_full = jnp.concatenate([cos] * reps, axis=1)
    sin_full = jnp.concatenate([sin_signed] * reps, axis=1)
    return x * cos_full + partner * sin_full


def _inproj_gqa_kernel(h_ref, mod_ref, w_ref, cos_ref, sin_ref, q_ref, k_ref, v_ref):
    u = _modulate(h_ref[...], mod_ref[0], 0, 1).astype(BF16)
    qn = C_HEADS * C_HEAD_DIM
    kn = C_KV_HEADS * C_HEAD_DIM
    cos, sin = cos_ref[...], sin_ref[...]
    q = _rope(_dot(u, w_ref[:, 0:qn]), cos, sin)
    q_ref[...] = (q * (C_HEAD_DIM ** -0.5)).astype(BF16)
    k_ref[...] = _rope(_dot(u, w_ref[:, qn:qn + kn]), cos, sin).astype(BF16)
    v_ref[...] = _dot(u, w_ref[:, qn + kn:qn + 2 * kn]).astype(BF16)


def _inproj_gqa(h, mod, w, cos, sin, seq):
    t, d = h.shape
    n = w.shape[1]
    tm = TOKEN_TILE
    qn = C_HEADS * C_HEAD_DIM
    kn = C_KV_HEADS * C_HEAD_DIM
    row = lambda i: (i, 0)
    pos = lambda i: (i % (seq // tm), 0)
    return pl.pallas_call(
        _inproj_gqa_kernel,
        grid=(t // tm,),
        in_specs=[
            pl.BlockSpec((tm, d), row),
            pl.BlockSpec((1, 6, d), _mod_spec(seq, 0, tm)),
            pl.BlockSpec((d, n), lambda i: (0, 0)),
            pl.BlockSpec((tm, LANES), pos),
            pl.BlockSpec((tm, LANES), pos),
        ],
        out_specs=[pl.BlockSpec((tm, qn), row), pl.BlockSpec((tm, kn), row), pl.BlockSpec((tm, kn), row)],
        out_shape=[
            jax.ShapeDtypeStruct((t, qn), BF16),
            jax.ShapeDtypeStruct((t, kn), BF16),
            jax.ShapeDtypeStruct((t, kn), BF16),
        ],
        compiler_params=_cparams(("parallel",)),
        name="inproj_gqa",
    )(h, mod, w, cos, sin)


def _inproj_kv_kernel(h_ref, mod_ref, w_ref, k_ref, v_ref):
    u = _modulate(h_ref[...], mod_ref[0], 0, 1).astype(BF16)
    kn = C_KV_HEADS * C_HEAD_DIM
    k_ref[...] = _dot(u, w_ref[:, 0:kn]).astype(BF16)
    v_ref[...] = _dot(u, w_ref[:, kn:2 * kn]).astype(BF16)


def _inproj_kv(h, mod, w, rows_per_group, group0):
    t, d = h.shape
    n = w.shape[1]
    tm = TOKEN_TILE
    kn = C_KV_HEADS * C_HEAD_DIM
    row = lambda i: (i, 0)
    return pl.pallas_call(
        _inproj_kv_kernel,
        grid=(t // tm,),
        in_specs=[
            pl.BlockSpec((tm, d), row),
            pl.BlockSpec((1, 6, d), _mod_spec(rows_per_group, group0, tm)),
            pl.BlockSpec((d, n), lambda i: (0, 0)),
        ],
        out_specs=[pl.BlockSpec((tm, kn), row), pl.BlockSpec((tm, kn), row)],
        out_shape=[jax.ShapeDtypeStruct((t, kn), BF16), jax.ShapeDtypeStruct((t, kn), BF16)],
        compiler_params=_cparams(("parallel",)),
        name="inproj_ctx_kv",
    )(h, mod, w)


def _rope_tables(seq):
    t = jnp.arange(seq)
    row = (t // GRID_W).astype(F32)
    col = (t % GRID_W).astype(F32)
    axis_dim = C_HEAD_DIM // 2
    inv_freq = ROPE_BASE ** (-jnp.arange(0, axis_dim, 2, dtype=F32) / axis_dim)
    ang_r = row[:, None] * inv_freq
    ang_c = col[:, None] * inv_freq
    cos = jnp.concatenate([jnp.cos(ang_r)] * 2 + [jnp.cos(ang_c)] * 2, axis=-1)
    sin = jnp.concatenate([-jnp.sin(ang_r), jnp.sin(ang_r), -jnp.sin(ang_c), jnp.sin(ang_c)], axis=-1)
    reps = LANES // C_HEAD_DIM
    return jnp.tile(cos, (1, reps)), jnp.tile(sin, (1, reps))


def _gqa_kernel(sink_ref, q_ref, k_ref, v_ref, kc_ref, vc_ref, o_ref, *, seq):
    blk = pl.program_id(1)
    span = C_BLOCK + 2 * C_WINDOW
    start = pl.multiple_of(jnp.clip(blk * C_BLOCK - C_WINDOW, 0, seq - span), C_BLOCK)
    q_pos = blk * C_BLOCK + lax.broadcasted_iota(jnp.int32, (C_BLOCK, span), 0)
    k_pos = start + lax.broadcasted_iota(jnp.int32, (C_BLOCK, span), 1)
    valid = jnp.abs(q_pos - k_pos) <= C_WINDOW
    valid = jnp.concatenate([valid] * C_GROUP, axis=0)
    d = C_HEAD_DIM
    for hk in range(C_KV_HEADS):
        ksl = slice(hk * d, (hk + 1) * d)
        q0 = hk * C_GROUP * d
        qs = jnp.concatenate([q_ref[0, :, q0 + g * d:q0 + (g + 1) * d] for g in range(C_GROUP)], axis=0)
        s_lat = jnp.where(valid, _dot_nt(qs, k_ref[0, pl.ds(start, span), ksl]), NEG_INF)
        s_ctx = _dot_nt(qs, kc_ref[0, :, ksl])
        sink = jnp.concatenate(
            [jnp.full((C_BLOCK, 1), sink_ref[hk * C_GROUP + g], F32) for g in range(C_GROUP)], axis=0)
        m = jnp.maximum(jnp.maximum(jnp.max(s_lat, axis=1, keepdims=True), jnp.max(s_ctx, axis=1, keepdims=True)),
                        sink)
        p_lat = jnp.exp(s_lat - m)
        p_ctx = jnp.exp(s_ctx - m)
        denom = jnp.sum(p_lat, axis=1, keepdims=True) + jnp.sum(p_ctx, axis=1, keepdims=True) + jnp.exp(sink - m)
        out = (_dot(p_lat.astype(BF16), v_ref[0, pl.ds(start, span), ksl])
               + _dot(p_ctx.astype(BF16), vc_ref[0, :, ksl])) / denom
        for g in range(C_GROUP):
            o_ref[0, :, q0 + g * d:q0 + (g + 1) * d] = out[g * C_BLOCK:(g + 1) * C_BLOCK].astype(BF16)


def _window_attention(q, k, v, k_ctx, v_ctx, sink):
    bsz, seq, qn = q.shape
    kn = k.shape[2]
    n_ctx = k_ctx.shape[1]
    full = lambda b, s: (b, 0, 0)
    return pl.pallas_call(
        functools.partial(_gqa_kernel, seq=seq),
        grid=(bsz, seq // C_BLOCK),
        in_specs=[
            pl.BlockSpec(memory_space=pltpu.SMEM),
            pl.BlockSpec((1, C_BLOCK, qn), lambda b, s: (b, s, 0)),
            pl.BlockSpec((1, seq, kn), full),
            pl.BlockSpec((1, seq, kn), full),
            pl.BlockSpec((1, n_ctx, kn), full),
            pl.BlockSpec((1, n_ctx, kn), full),
        ],
        out_specs=pl.BlockSpec((1, C_BLOCK, qn), lambda b, s: (b, s, 0)),
        out_shape=jax.ShapeDtypeStruct((bsz, seq, qn), BF16),
        compiler_params=_cparams(("parallel", "arbitrary")),
        name="window_gqa_attn",
    )(sink, q, k, v, k_ctx, v_ctx)


def _route(logits):
    lane = lax.broadcasted_iota(jnp.int32, logits.shape, 1)
    big = jnp.int32(ROUTER_LANES)
    is_group = lane < N_GROUPS
    lg = jnp.where(is_group, logits, NEG_INF)
    g_max = jnp.max(lg, axis=1, keepdims=True)
    g_prob = 1.0 / jnp.sum(jnp.exp(lg - g_max), axis=1, keepdims=True)
    g_idx = jnp.min(jnp.where(lg == g_max, lane, big), axis=1, keepdims=True)
    first = GATE_LANE0 + g_idx * EXPERTS_PER_GROUP
    in_group = (lane >= first) & (lane < first + EXPERTS_PER_GROUP)
    le = jnp.where(in_group, logits, NEG_INF)
    e1 = jnp.max(le, axis=1, keepdims=True)
    i1 = jnp.min(jnp.where(le == e1, lane, big), axis=1, keepdims=True)
    le2 = jnp.where(lane == i1, NEG_INF, le)
    e2 = jnp.max(le2, axis=1, keepdims=True)
    i2 = jnp.min(jnp.where(le2 == e2, lane, big), axis=1, keepdims=True)
    r = jnp.exp(e2 - e1)
    w1 = g_prob / (1.0 + r)
    w2 = w1 * r
    return jnp.where(lane == i1, w1, jnp.where(lane == i2, w2, 0.0))


def _post_mix_kernel(*refs, n_in, alpha):
    a_refs = refs[:n_in]
    w_refs = refs[n_in:2 * n_in]
    b_ref, h_ref, mod_ref, lg_ref, lb_ref, r_ref, h1_ref, t_ref, gate_ref = refs[2 * n_in:]
    o = functools.reduce(jnp.add, [_dot(a[...], w[...]) for a, w in zip(a_refs, w_refs)]) + b_ref[...]
    m = mod_ref[0]
    h1 = _layer_norm(alpha * h_ref[...] + m[2:3] * o, lg_ref[...], lb_ref[...])
    h1_ref[...] = h1
    t = _modulate(h1, m, 3, 4)
    t_ref[...] = t.astype(BF16)
    gate_ref[...] = _route(_dot3(t, r_ref[...]))


def _post_mix(acts, weights, bias, h, mod, ln_g, ln_b, router, rows_per_group, group0, alpha):
    t, d = h.shape
    tm = TOKEN_TILE
    row = lambda i: (i, 0)
    const = lambda i: (0, 0)
    in_specs = [pl.BlockSpec((tm, a.shape[1]), row) for a in acts]
    in_specs += [pl.BlockSpec(w.shape, const) for w in weights]
    in_specs += [
        pl.BlockSpec((1, d), const),
        pl.BlockSpec((tm, d), row),
        pl.BlockSpec((1, 6, d), _mod_spec(rows_per_group, group0, tm)),
        pl.BlockSpec((1, d), const),
        pl.BlockSpec((1, d), const),
        pl.BlockSpec((d, ROUTER_LANES), const),
    ]
    return pl.pallas_call(
        functools.partial(_post_mix_kernel, n_in=len(acts), alpha=alpha),
        grid=(t // tm,),
        in_specs=in_specs,
        out_specs=[pl.BlockSpec((tm, d), row), pl.BlockSpec((tm, d), row), pl.BlockSpec((tm, ROUTER_LANES), row)],
        out_shape=[
            jax.ShapeDtypeStruct((t, d), F32),
            jax.ShapeDtypeStruct((t, d), BF16),
            jax.ShapeDtypeStruct((t, ROUTER_LANES), F32),
        ],
        compiler_params=_cparams(("parallel",)),
        name="outproj_ln_router",
    )(*acts, *weights, bias.reshape(1, d), h, mod, ln_g.reshape(1, d), ln_b.reshape(1, d), router)


def _moe_kernel(t_ref, gate_ref, h_ref, mod_ref, wg_ref, wu_ref, wd_ref, lg_ref, lb_ref, o_ref, acc_ref, *, alpha):
    grp = pl.program_id(1)

    @pl.when(grp == 0)
    def _():
        acc_ref[...] = jnp.zeros_like(acc_ref)

    t = t_ref[...]
    gates = gate_ref[...]
    lane = lax.broadcasted_iota(jnp.int32, gates.shape, 1)
    y = acc_ref[...]
    for j in range(EXPERTS_PER_GROUP):
        gate = jnp.sum(jnp.where(lane == GATE_LANE0 + grp * EXPERTS_PER_GROUP + j, gates, 0.0), axis=1,
                       keepdims=True)
        hid = _silu(_dot(t, wg_ref[j])) * _dot(t, wu_ref[j]) * gate
        y = y + _dot(hid.astype(BF16), wd_ref[j])
    acc_ref[...] = y

    @pl.when(grp == N_GROUPS - 1)
    def _():
        m = mod_ref[0]
        o_ref[...] = _layer_norm(alpha * h_ref[...] + m[5:6] * acc_ref[...], lg_ref[...], lb_ref[...])


def _moe_ln(t_act, gates, h, mod, wg, wu, wd, ln_g, ln_b, rows_per_group, group0, alpha):
    t, d = h.shape
    tm = TOKEN_TILE
    f = wg.shape[2]
    row = lambda i, g: (i, 0)
    const = lambda i, g: (0, 0)
    mod_idx = _mod_spec(rows_per_group, group0, tm)
    return pl.pallas_call(
        functools.partial(_moe_kernel, alpha=alpha),
        grid=(t // tm, N_GROUPS),
        in_specs=[
            pl.BlockSpec((tm, d), row),
            pl.BlockSpec((tm, ROUTER_LANES), row),
            pl.BlockSpec((tm, d), row),
            pl.BlockSpec((1, 6, d), lambda i, g: mod_idx(i)),
            pl.BlockSpec((EXPERTS_PER_GROUP, d, f), lambda i, g: (g, 0, 0)),
            pl.BlockSpec((EXPERTS_PER_GROUP, d, f), lambda i, g: (g, 0, 0)),
            pl.BlockSpec((EXPERTS_PER_GROUP, f, d), lambda i, g: (g, 0, 0)),
            pl.BlockSpec((1, d), const),
            pl.BlockSpec((1, d), const),
        ],
        out_specs=pl.BlockSpec((tm, d), row),
        out_shape=jax.ShapeDtypeStruct((t, d), F32),
        scratch_shapes=[pltpu.VMEM((tm, d), F32)],
        compiler_params=_cparams(("parallel", "arbitrary")),
        name="moe_ln",
    )(t_act, gates, h, mod, wg, wu, wd, ln_g.reshape(1, d), ln_b.reshape(1, d))


def _router_matrix(router_group, router_expert):
    d = router_group.shape[0]
    pad = jnp.zeros((d, ROUTER_LANES - N_GROUPS - N_EXPERTS), F32)
    return jnp.concatenate([router_group, router_expert, pad], axis=1)


def kernel(x, c, ctx, c_ctx, ada_w, ada_b, ln_g, ln_b, ab_w_in, ab_b_in, conv_w, conv_b, conv_ln_g, conv_ln_b,
           na_rpb, ab_w_out, ab_b_out, gqa_w_in, gqa_sink, gqa_w_out, router_group, router_expert, exp_w_gate,
           exp_w_up, exp_w_down):
    bsz, seq, d = x.shape
    n_ctx = ctx.shape[1]
    depth = ada_w.shape[0]
    assert depth == DEPTH and bsz + 1 <= MOD_ROWS
    assert seq % TOKEN_TILE == 0 and (bsz * n_ctx) % TOKEN_TILE == 0 and seq % C_BLOCK == 0
    rows = seq // GRID_W
    assert rows % NA_ROWS_PER_STEP == 0 and rows >= NA_UNION_ROWS
    alpha = (2.0 * depth) ** 0.25
    t_lat, t_ctx = bsz * seq, bsz * n_ctx

    cc = jnp.concatenate([c, c_ctx[None], jnp.zeros((MOD_ROWS - bsz - 1, d), F32)], axis=0)
    mod = _modulation(cc, ada_w, ada_b).reshape(depth, MOD_ROWS, 6, d)
    lat_grp = dict(rows_per_group=seq, group0=0)
    ctx_grp = dict(rows_per_group=t_ctx, group0=bsz)

    h_lat = x.reshape(t_lat, d)
    h_ctx = ctx.reshape(t_ctx, d)
    for i in range(depth):
        j = i // 2
        need_ctx = i < depth - 1
        router = _router_matrix(router_group[i], router_expert[i])
        wg, wu, wd = exp_w_gate[i].astype(BF16), exp_w_up[i].astype(BF16), exp_w_down[i].astype(BF16)
        if i % 2 == 0:
            w_in = ab_w_in[j].astype(BF16)
            g_lat, q_lat, k_lat, v_lat = _inproj_ab(h_lat, mod[i], w_in, ab_b_in[j], **lat_grp)
            g_ctx, q_ctx, k_ctx, v_ctx = _inproj_ab(h_ctx, mod[i], w_in, ab_b_in[j], **ctx_grp)
            to_seq = lambda a, n: a.reshape(bsz, n, a.shape[-1])
            conv_args = (conv_w[j], conv_b[j], conv_ln_g[j], conv_ln_b[j])
            conv_lat = _conv_module(to_seq(g_lat, seq), *conv_args).reshape(t_lat, CONV_DIM)
            k_ctx, v_ctx = to_seq(k_ctx, n_ctx), to_seq(v_ctx, n_ctx)
            bias, types = _na_bias_tables(na_rpb[j], rows)
            na_lat = _neighbourhood_attention(to_seq(q_lat, seq), to_seq(k_lat, seq), to_seq(v_lat, seq), k_ctx,
                                              v_ctx, bias, types).reshape(t_lat, NA_DIM)
            w_out = ab_w_out[j].astype(BF16)
            w_outs = [w_out[:CONV_DIM], w_out[CONV_DIM:]]
            b_out = ab_b_out[j]
            acts_lat = [conv_lat, na_lat]
            if need_ctx:
                conv_ctx = _conv_module(to_seq(g_ctx, n_ctx), *conv_args).reshape(t_ctx, CONV_DIM)
                na_ctx = _context_attention(to_seq(q_ctx, n_ctx), k_ctx, v_ctx).reshape(t_ctx, NA_DIM)
                acts_ctx = [conv_ctx, na_ctx]
        else:
            assert not need_ctx
            qn = C_HEADS * C_HEAD_DIM
            w_in = gqa_w_in[j].astype(BF16)
            cos, sin = _rope_tables(seq)
            q_lat, k_lat, v_lat = _inproj_gqa(h_lat, mod[i], w_in, cos, sin, seq)
            k_ctx, v_ctx = _inproj_kv(h_ctx, mod[i], w_in[:, qn:], **ctx_grp)
            to_seq = lambda a, n: a.reshape(bsz, n, a.shape[-1])
            att = _window_attention(to_seq(q_lat, seq), to_seq(k_lat, seq), to_seq(v_lat, seq),
                                    to_seq(k_ctx, n_ctx), to_seq(v_ctx, n_ctx), gqa_sink[j])
            acts_lat = [att.reshape(t_lat, qn)]
            w_outs = [gqa_w_out[j].astype(BF16)]
            b_out = jnp.zeros((d,), F32)
        ln1 = (ln_g[i, 0], ln_b[i, 0])
        ln2 = (ln_g[i, 1], ln_b[i, 1])
        h1, t_act, gates = _post_mix(acts_lat, w_outs, b_out, h_lat, mod[i], *ln1, router, alpha=alpha, **lat_grp)
        h_lat = _moe_ln(t_act, gates, h1, mod[i], wg, wu, wd, *ln2, alpha=alpha, **lat_grp)
        if need_ctx:
            h1, t_act, gates = _post_mix(acts_ctx, w_outs, b_out, h_ctx, mod[i], *ln1, router, alpha=alpha,
                                         **ctx_grp)
            h_ctx = _moe_ln(t_act, gates, h1, mod[i], wg, wu, wd, *ln2, alpha=alpha, **ctx_grp)
    return h_lat.reshape(bsz, seq, d)
```

```python
import functools

import numpy as np
import jax
import jax.numpy as jnp
from jax import lax
from jax.experimental import pallas as pl
from jax.experimental.pallas import tpu as pltpu

F32 = jnp.float32
BF16 = jnp.bfloat16

DEPTH = 2
GRID_W = 64
CONV_DIM = 512
CONV_WIDTH = 31
NA_HEADS = 8
NA_HEAD_DIM = 64
NA_DIM = NA_HEADS * NA_HEAD_DIM
NA_KH = 8
NA_KW = 16
C_HEADS = 16
C_KV_HEADS = 4
C_GROUP = C_HEADS // C_KV_HEADS
C_HEAD_DIM = 64
C_WINDOW = 128
C_BLOCK = 128
ROPE_BASE = 10000.0
N_GROUPS = 4
EXPERTS_PER_GROUP = 4
N_EXPERTS = N_GROUPS * EXPERTS_PER_GROUP
D_EXPERT = 256
LN_EPS = 1e-5
NEG_INF = -1e30

LANES = 128
SUBLANES = 8
VMEM_LIMIT_BYTES = 56 * 1024 * 1024

TOKEN_TILE = 512
MOD_ROWS = 24
NA_ROWS_PER_STEP = 4
NA_UNION_ROWS = NA_KH + NA_ROWS_PER_STEP
CONV_CHUNK = 32
CONV_HALO = 16
ROUTER_LANES = LANES
GATE_LANE0 = N_GROUPS


def _cparams(semantics):
    return pltpu.CompilerParams(dimension_semantics=semantics, vmem_limit_bytes=VMEM_LIMIT_BYTES)


def _dot(a, b):
    return jnp.dot(a, b, preferred_element_type=F32)


def _dot_nt(a, b):
    return lax.dot_general(a, b, (((1,), (1,)), ((), ())), preferred_element_type=F32)


def _dot_tn(a, b):
    return lax.dot_general(a, b, (((0,), (0,)), ((), ())), preferred_element_type=F32)


def _split_bf16(a):
    hi = a.astype(BF16)
    lo = (a - hi.astype(F32)).astype(BF16)
    return hi, lo


def _dot3(a, b):
    a_hi, a_lo = _split_bf16(a)
    b_hi, b_lo = _split_bf16(b)
    return _dot(a_hi, b_hi) + (_dot(a_lo, b_hi) + _dot(a_hi, b_lo))


def _layer_norm(x, g, b):
    mu = jnp.mean(x, axis=-1, keepdims=True)
    xc = x - mu
    var = jnp.mean(xc * xc, axis=-1, keepdims=True)
    return xc * lax.rsqrt(var + LN_EPS) * g + b


def _silu(x):
    return x * jax.nn.sigmoid(x)


def _mod_kernel(cc_ref, w_ref, b_ref, o_ref):
    o_ref[0] = _dot3(_silu(cc_ref[...]), w_ref[0]) + b_ref[0]


def _modulation(cc, ada_w, ada_b):
    depth, d, n = ada_w.shape
    tn = n // 4
    return pl.pallas_call(
        _mod_kernel,
        grid=(depth, n // tn),
        in_specs=[
            pl.BlockSpec((MOD_ROWS, d), lambda i, j: (0, 0)),
            pl.BlockSpec((1, d, tn), lambda i, j: (i, 0, j)),
            pl.BlockSpec((1, 1, tn), lambda i, j: (i, 0, j)),
        ],
        out_specs=pl.BlockSpec((1, MOD_ROWS, tn), lambda i, j: (i, 0, j)),
        out_shape=jax.ShapeDtypeStruct((depth, MOD_ROWS, n), F32),
        compiler_params=_cparams(("arbitrary", "arbitrary")),
        name="adaln_mod",
    )(cc, ada_w, ada_b.reshape(depth, 1, n))


def _mod_spec(rows_per_group, group0, tm):
    return lambda i: (group0 + (i * tm) // rows_per_group, 0, 0)


def _modulate(h, m, shift_row, scale_row):
    return h * (1.0 + m[scale_row:scale_row + 1]) + m[shift_row:shift_row + 1]


def _inproj_ab_kernel(h_ref, mod_ref, w_ref, b_ref, g_ref, q_ref, k_ref, v_ref):
    u = _modulate(h_ref[...], mod_ref[0], 0, 1).astype(BF16)
    c = CONV_DIM
    za = _dot(u, w_ref[:, 0:c]) + b_ref[:, 0:c]
    zb = _dot(u, w_ref[:, c:2 * c]) + b_ref[:, c:2 * c]
    g_ref[...] = za * jax.nn.sigmoid(zb)
    q0 = 2 * c
    zq = _dot(u, w_ref[:, q0:q0 + NA_DIM]) + b_ref[:, q0:q0 + NA_DIM]
    q_ref[...] = (zq * (NA_HEAD_DIM ** -0.5)).astype(BF16)
    k0 = q0 + NA_DIM
    k_ref[...] = (_dot(u, w_ref[:, k0:k0 + NA_DIM]) + b_ref[:, k0:k0 + NA_DIM]).astype(BF16)
    v0 = k0 + NA_DIM
    v_ref[...] = (_dot(u, w_ref[:, v0:v0 + NA_DIM]) + b_ref[:, v0:v0 + NA_DIM]).astype(BF16)


def _inproj_ab(h, mod, w, b, rows_per_group, group0):
    t, d = h.shape
    n = w.shape[1]
    tm = TOKEN_TILE
    row = lambda i: (i, 0)
    return pl.pallas_call(
        _inproj_ab_kernel,
        grid=(t // tm,),
        in_specs=[
            pl.BlockSpec((tm, d), row),
            pl.BlockSpec((1, 6, d), _mod_spec(rows_per_group, group0, tm)),
            pl.BlockSpec((d, n), lambda i: (0, 0)),
            pl.BlockSpec((1, n), lambda i: (0, 0)),
        ],
        out_specs=[
            pl.BlockSpec((tm, CONV_DIM), row),
            pl.BlockSpec((tm, NA_DIM), row),
            pl.BlockSpec((tm, NA_DIM), row),
            pl.BlockSpec((tm, NA_DIM), row),
        ],
        out_shape=[
            jax.ShapeDtypeStruct((t, CONV_DIM), F32),
            jax.ShapeDtypeStruct((t, NA_DIM), BF16),
            jax.ShapeDtypeStruct((t, NA_DIM), BF16),
            jax.ShapeDtypeStruct((t, NA_DIM), BF16),
        ],
        compiler_params=_cparams(("parallel",)),
        name="inproj_conv_na",
    )(h, mod, w, b.reshape(1, n))


def _conv_kernel(g_ref, w_ref, cb_ref, lg_ref, lb_ref, o_ref, pad_ref, *, seq):
    zeros = jnp.zeros((CONV_HALO, CONV_DIM), F32)
    pad_ref[0:CONV_HALO, :] = zeros
    pad_ref[CONV_HALO + seq:2 * CONV_HALO + seq, :] = zeros
    pad_ref[CONV_HALO:CONV_HALO + seq, :] = g_ref[0]
    first = CONV_HALO - CONV_WIDTH // 2
    ext = CONV_CHUNK + SUBLANES

    def chunk(i, carry):
        r0 = pl.multiple_of(i * CONV_CHUNK, CONV_CHUNK)
        acc = jnp.zeros((CONV_CHUNK, CONV_DIM), F32) + cb_ref[...]
        for res in range(SUBLANES):
            part = None
            for base in range(0, first + CONV_WIDTH, SUBLANES):
                tap = base + res - first
                if 0 <= tap < CONV_WIDTH:
                    term = pad_ref[pl.ds(r0 + base, ext), :] * w_ref[tap:tap + 1, :]
                    part = term if part is None else part + term
            acc = acc + part[res:res + CONV_CHUNK]
        y = _layer_norm(acc, lg_ref[...], lb_ref[...])
        o_ref[0, pl.ds(r0, CONV_CHUNK), :] = _silu(y).astype(BF16)
        return carry

    lax.fori_loop(0, seq // CONV_CHUNK, chunk, 0)


def _conv_module(g, conv_w, conv_b, ln_g, ln_b):
    bsz, seq, c = g.shape
    vec = lambda i: (0, 0)
    return pl.pallas_call(
        functools.partial(_conv_kernel, seq=seq),
        grid=(bsz,),
        in_specs=[
            pl.BlockSpec((1, seq, c), lambda i: (i, 0, 0)),
            pl.BlockSpec((CONV_WIDTH, c), vec),
            pl.BlockSpec((1, c), vec),
            pl.BlockSpec((1, c), vec),
            pl.BlockSpec((1, c), vec),
        ],
        out_specs=pl.BlockSpec((1, seq, c), lambda i: (i, 0, 0)),
        out_shape=jax.ShapeDtypeStruct((bsz, seq, c), BF16),
        scratch_shapes=[pltpu.VMEM((seq + 2 * CONV_HALO, c), F32)],
        compiler_params=_cparams(("parallel",)),
        name="conv_module",
    )(g, conv_w, conv_b.reshape(1, c), ln_g.reshape(1, c), ln_b.reshape(1, c))


N_ROW_OFFS = 2 * NA_KH - 1
N_COL_OFFS = 2 * NA_KW - 1


def _na_geometry(rows):
    rq, ru = NA_ROWS_PER_STEP, NA_UNION_ROWS
    geoms, types = [], []
    for step in range(rows // rq):
        r = step * rq + np.arange(rq)[:, None]
        key_row = np.clip(step * rq - NA_KH // 2, 0, rows - ru) + np.arange(ru)[None, :]
        row_start = np.clip(r - NA_KH // 2, 0, rows - NA_KH)
        row_in = (key_row >= row_start) & (key_row < row_start + NA_KH)
        geom = np.where(row_in, key_row - r + NA_KH - 1, N_ROW_OFFS).astype(np.int32)
        for t, other in enumerate(geoms):
            if np.array_equal(other, geom):
                types.append(t)
                break
        else:
            types.append(len(geoms))
            geoms.append(geom)
    return np.stack(geoms), np.asarray(types, np.int32)


def _na_bias_kernel(row_off_ref, rpb_ref, o_ref, tile_ref):
    typ = pl.program_id(0)
    head = pl.program_id(1)
    qc = lax.broadcasted_iota(jnp.int32, (GRID_W, GRID_W), 0)
    kc = lax.broadcasted_iota(jnp.int32, (GRID_W, GRID_W), 1)
    col_start = jnp.clip(qc - NA_KW // 2, 0, GRID_W - NA_KW)
    col_in = (kc >= col_start) & (kc < col_start + NA_KW)
    col_off = jnp.clip(kc - qc, -(NA_KW - 1), NA_KW - 1) + NA_KW - 1
    for a in range(N_ROW_OFFS):
        tile = jnp.zeros((GRID_W, GRID_W), F32)
        for b in range(N_COL_OFFS):
            tile = jnp.where(col_off == b, rpb_ref[(head * N_ROW_OFFS + a) * N_COL_OFFS + b], tile)
        tile_ref[a] = jnp.where(col_in, tile, NEG_INF)
    tile_ref[N_ROW_OFFS] = jnp.full((GRID_W, GRID_W), NEG_INF, F32)
    for qr in range(NA_ROWS_PER_STEP):
        for kr in range(NA_UNION_ROWS):
            a = row_off_ref[(typ * NA_ROWS_PER_STEP + qr) * NA_UNION_ROWS + kr]
            o_ref[0, 0, qr * GRID_W:(qr + 1) * GRID_W, kr * GRID_W:(kr + 1) * GRID_W] = tile_ref[a]


def _na_bias_tables(rpb, row_off):
    n_types = row_off.shape[0]
    tq = NA_ROWS_PER_STEP * GRID_W
    n_win = NA_UNION_ROWS * GRID_W
    return pl.pallas_call(
        _na_bias_kernel,
        grid=(n_types, NA_HEADS),
        in_specs=[pl.BlockSpec(memory_space=pltpu.SMEM), pl.BlockSpec(memory_space=pltpu.SMEM)],
        out_specs=pl.BlockSpec((1, 1, tq, n_win), lambda t, h: (t, h, 0, 0)),
        out_shape=jax.ShapeDtypeStruct((n_types, NA_HEADS, tq, n_win), F32),
        scratch_shapes=[pltpu.VMEM((N_ROW_OFFS + 1, GRID_W, GRID_W), F32)],
        compiler_params=_cparams(("parallel", "parallel")),
        name="na_bias_table",
    )(jnp.asarray(row_off.reshape(-1)), rpb.astype(F32).reshape(-1))


def _softmax_pv(scores, values):
    m = functools.reduce(jnp.maximum, [jnp.max(s, axis=1, keepdims=True) for s in scores])
    ps = [jnp.exp(s - m) for s in scores]
    denom = functools.reduce(jnp.add, [jnp.sum(p, axis=1, keepdims=True) for p in ps])
    out = functools.reduce(jnp.add, [_dot(p.astype(BF16), v) for p, v in zip(ps, values)])
    return out / denom


def _na_kernel(type_ref, q_ref, k_ref, v_ref, kc_ref, vc_ref, bias_ref, o_ref, *, rows):
    del type_ref
    step = pl.program_id(1)
    start_row = jnp.clip(step * NA_ROWS_PER_STEP - NA_KH // 2, 0, rows - NA_UNION_ROWS)
    start = pl.multiple_of(start_row * GRID_W, GRID_W)
    n_win = NA_UNION_ROWS * GRID_W
    for h in range(NA_HEADS):
        sl = slice(h * NA_HEAD_DIM, (h + 1) * NA_HEAD_DIM)
        qh = q_ref[0, :, sl]
        s_lat = _dot_nt(qh, k_ref[0, pl.ds(start, n_win), sl]) + bias_ref[0, h]
        s_ctx = _dot_nt(qh, kc_ref[0, :, sl])
        out = _softmax_pv([s_lat, s_ctx], [v_ref[0, pl.ds(start, n_win), sl], vc_ref[0, :, sl]])
        o_ref[0, :, sl] = out.astype(BF16)


def _neighbourhood_attention(q, k, v, k_ctx, v_ctx, bias, types):
    bsz, seq, dim = q.shape
    n_ctx = k_ctx.shape[1]
    rows = seq // GRID_W
    tq = NA_ROWS_PER_STEP * GRID_W
    n_win = NA_UNION_ROWS * GRID_W
    full = lambda b, s, t: (b, 0, 0)
    grid_spec = pltpu.PrefetchScalarGridSpec(
        num_scalar_prefetch=1,
        grid=(bsz, seq // tq),
        in_specs=[
            pl.BlockSpec((1, tq, dim), lambda b, s, t: (b, s, 0)),
            pl.BlockSpec((1, seq, dim), full),
            pl.BlockSpec((1, seq, dim), full),
            pl.BlockSpec((1, n_ctx, dim), full),
            pl.BlockSpec((1, n_ctx, dim), full),
            pl.BlockSpec((1, NA_HEADS, tq, n_win), lambda b, s, t: (t[s], 0, 0, 0)),
        ],
        out_specs=pl.BlockSpec((1, tq, dim), lambda b, s, t: (b, s, 0)),
    )
    return pl.pallas_call(
        functools.partial(_na_kernel, rows=rows),
        grid_spec=grid_spec,
        out_shape=jax.ShapeDtypeStruct((bsz, seq, dim), BF16),
        compiler_params=_cparams(("parallel", "arbitrary")),
        name="neighbourhood_attn",
    )(jnp.asarray(types), q, k, v, k_ctx, v_ctx, bias)


def _ctx_attn_kernel(q_ref, k_ref, v_ref, o_ref):
    for h in range(NA_HEADS):
        sl = slice(h * NA_HEAD_DIM, (h + 1) * NA_HEAD_DIM)
        s = _dot_nt(q_ref[0, :, sl], k_ref[0, :, sl])
        o_ref[0, :, sl] = _softmax_pv([s], [v_ref[0, :, sl]]).astype(BF16)


def _context_attention(q, k, v):
    bsz, n, dim = q.shape
    spec = pl.BlockSpec((1, n, dim), lambda b: (b, 0, 0))
    return pl.pallas_call(
        _ctx_attn_kernel,
        grid=(bsz,),
        in_specs=[spec, spec, spec],
        out_specs=spec,
        out_shape=jax.ShapeDtypeStruct((bsz, n, dim), BF16),
        compiler_params=_cparams(("parallel",)),
        name="context_attn",
    )(q, k, v)


def _rope(x, cos, sin_signed):
    n = x.shape[1]
    half = C_HEAD_DIM // 4
    lane = lax.broadcasted_iota(jnp.int32, x.shape, 1)
    partner = jnp.where(lane % (2 * half) < half, pltpu.roll(x, n - half, 1), pltpu.roll(x, half, 1))
    reps = n // LANES
    cos_full = jnp.concatenate([cos] * reps, axis=1)
    sin_full = jnp.concatenate([sin_signed] * reps, axis=1)
    return x * cos_full + partner * sin_full


def _inproj_gqa_kernel(h_ref, mod_ref, w_ref, cos_ref, sin_ref, q_ref, k_ref, v_ref):
    u = _modulate(h_ref[...], mod_ref[0], 0, 1).astype(BF16)
    qn = C_HEADS * C_HEAD_DIM
    kn = C_KV_HEADS * C_HEAD_DIM
    cos, sin = cos_ref[...], sin_ref[...]
    q = _rope(_dot(u, w_ref[:, 0:qn]), cos, sin)
    q_ref[...] = (q * (C_HEAD_DIM ** -0.5)).astype(BF16)
    k_ref[...] = _rope(_dot(u, w_ref[:, qn:qn + kn]), cos, sin).astype(BF16)
    v_ref[...] = _dot(u, w_ref[:, qn + kn:qn + 2 * kn]).astype(BF16)


def _inproj_gqa(h, mod, w, cos, sin, seq):
    t, d = h.shape
    n = w.shape[1]
    tm = TOKEN_TILE
    qn = C_HEADS * C_HEAD_DIM
    kn = C_KV_HEADS * C_HEAD_DIM
    row = lambda i: (i, 0)
    pos = lambda i: (i % (seq // tm), 0)
    return pl.pallas_call(
        _inproj_gqa_kernel,
        grid=(t // tm,),
        in_specs=[
            pl.BlockSpec((tm, d), row),
            pl.BlockSpec((1, 6, d), _mod_spec(seq, 0, tm)),
            pl.BlockSpec((d, n), lambda i: (0, 0)),
            pl.BlockSpec((tm, LANES), pos),
            pl.BlockSpec((tm, LANES), pos),
        ],
        out_specs=[pl.BlockSpec((tm, qn), row), pl.BlockSpec((tm, kn), row), pl.BlockSpec((tm, kn), row)],
        out_shape=[
            jax.ShapeDtypeStruct((t, qn), BF16),
            jax.ShapeDtypeStruct((t, kn), BF16),
            jax.ShapeDtypeStruct((t, kn), BF16),
        ],
        compiler_params=_cparams(("parallel",)),
        name="inproj_gqa",
    )(h, mod, w, cos, sin)


def _inproj_kv_kernel(h_ref, mod_ref, w_ref, k_ref, v_ref):
    u = _modulate(h_ref[...], mod_ref[0], 0, 1).astype(BF16)
    kn = C_KV_HEADS * C_HEAD_DIM
    k_ref[...] = _dot(u, w_ref[:, 0:kn]).astype(BF16)
    v_ref[...] = _dot(u, w_ref[:, kn:2 * kn]).astype(BF16)


def _inproj_kv(h, mod, w, rows_per_group, group0):
    t, d = h.shape
    n = w.shape[1]
    tm = TOKEN_TILE
    kn = C_KV_HEADS * C_HEAD_DIM
    row = lambda i: (i, 0)
    return pl.pallas_call(
        _inproj_kv_kernel,
        grid=(t // tm,),
        in_specs=[
            pl.BlockSpec((tm, d), row),
            pl.BlockSpec((1, 6, d), _mod_spec(rows_per_group, group0, tm)),
            pl.BlockSpec((d, n), lambda i: (0, 0)),
        ],
        out_specs=[pl.BlockSpec((tm, kn), row), pl.BlockSpec((tm, kn), row)],
        out_shape=[jax.ShapeDtypeStruct((t, kn), BF16), jax.ShapeDtypeStruct((t, kn), BF16)],
        compiler_params=_cparams(("parallel",)),
        name="inproj_ctx_kv",
    )(h, mod, w)


def _rope_tables(seq):
    t = jnp.arange(seq)
    row = (t // GRID_W).astype(F32)
    col = (t % GRID_W).astype(F32)
    axis_dim = C_HEAD_DIM // 2
    inv_freq = ROPE_BASE ** (-jnp.arange(0, axis_dim, 2, dtype=F32) / axis_dim)
    ang_r = row[:, None] * inv_freq
    ang_c = col[:, None] * inv_freq
    cos = jnp.concatenate([jnp.cos(ang_r)] * 2 + [jnp.cos(ang_c)] * 2, axis=-1)
    sin = jnp.concatenate([-jnp.sin(ang_r), jnp.sin(ang_r), -jnp.sin(ang_c), jnp.sin(ang_c)], axis=-1)
    reps = LANES // C_HEAD_DIM
    return jnp.tile(cos, (1, reps)), jnp.tile(sin, (1, reps))


def _gqa_kernel(sink_ref, q_ref, k_ref, v_ref, kc_ref, vc_ref, o_ref, *, seq):
    blk = pl.program_id(1)
    span = C_BLOCK + 2 * C_WINDOW
    start = pl.multiple_of(jnp.clip(blk * C_BLOCK - C_WINDOW, 0, seq - span), C_BLOCK)
    k_pos = start + lax.broadcasted_iota(jnp.int32, (span, C_BLOCK), 0)
    q_pos = blk * C_BLOCK + lax.broadcasted_iota(jnp.int32, (span, C_BLOCK), 1)
    valid = jnp.abs(q_pos - k_pos) <= C_WINDOW
    valid = jnp.concatenate([valid] * C_GROUP, axis=1)
    d = C_HEAD_DIM
    for hk in range(C_KV_HEADS):
        ksl = slice(hk * d, (hk + 1) * d)
        q0 = hk * C_GROUP * d
        qs = jnp.concatenate([q_ref[0, :, q0 + g * d:q0 + (g + 1) * d] for g in range(C_GROUP)], axis=0)
        s_lat = jnp.where(valid, _dot_nt(k_ref[0, pl.ds(start, span), ksl], qs), NEG_INF)
        s_ctx = _dot_nt(kc_ref[0, :, ksl], qs)
        sink = jnp.concatenate(
            [jnp.full((1, C_BLOCK), sink_ref[hk * C_GROUP + g], F32) for g in range(C_GROUP)], axis=1)
        m = jnp.maximum(jnp.maximum(jnp.max(s_lat, axis=0, keepdims=True), jnp.max(s_ctx, axis=0, keepdims=True)),
                        sink)
        p_lat = jnp.exp(s_lat - m)
        p_ctx = jnp.exp(s_ctx - m)
        denom = jnp.sum(p_lat, axis=0, keepdims=True) + jnp.sum(p_ctx, axis=0, keepdims=True) + jnp.exp(sink - m)
        inv = 1.0 / denom
        out = (_dot_tn((p_lat * inv).astype(BF16), v_ref[0, pl.ds(start, span), ksl])
               + _dot_tn((p_ctx * inv).astype(BF16), vc_ref[0, :, ksl]))
        for g in range(C_GROUP):
            o_ref[0, :, q0 + g * d:q0 + (g + 1) * d] = out[g * C_BLOCK:(g + 1) * C_BLOCK].astype(BF16)


def _window_attention(q, k, v, k_ctx, v_ctx, sink):
    bsz, seq, qn = q.shape
    kn = k.shape[2]
    n_ctx = k_ctx.shape[1]
    full = lambda b, s: (b, 0, 0)
    return pl.pallas_call(
        functools.partial(_gqa_kernel, seq=seq),
        grid=(bsz, seq // C_BLOCK),
        in_specs=[
            pl.BlockSpec(memory_space=pltpu.SMEM),
            pl.BlockSpec((1, C_BLOCK, qn), lambda b, s: (b, s, 0)),
            pl.BlockSpec((1, seq, kn), full),
            pl.BlockSpec((1, seq, kn), full),
            pl.BlockSpec((1, n_ctx, kn), full),
            pl.BlockSpec((1, n_ctx, kn), full),
        ],
        out_specs=pl.BlockSpec((1, C_BLOCK, qn), lambda b, s: (b, s, 0)),
        out_shape=jax.ShapeDtypeStruct((bsz, seq, qn), BF16),
        compiler_params=_cparams(("parallel", "arbitrary")),
        name="window_gqa_attn",
    )(sink, q, k, v, k_ctx, v_ctx)


def _route(logits):
    lane = lax.broadcasted_iota(jnp.int32, logits.shape, 1)
    big = jnp.int32(ROUTER_LANES)
    is_group = lane < N_GROUPS
    lg = jnp.where(is_group, logits, NEG_INF)
    g_max = jnp.max(lg, axis=1, keepdims=True)
    g_prob = 1.0 / jnp.sum(jnp.exp(lg - g_max), axis=1, keepdims=True)
    g_idx = jnp.min(jnp.where(lg == g_max, lane, big), axis=1, keepdims=True)
    first = GATE_LANE0 + g_idx * EXPERTS_PER_GROUP
    in_group = (lane >= first) & (lane < first + EXPERTS_PER_GROUP)
    le = jnp.where(in_group, logits, NEG_INF)
    e1 = jnp.max(le, axis=1, keepdims=True)
    i1 = jnp.min(jnp.where(le == e1, lane, big), axis=1, keepdims=True)
    le2 = jnp.where(lane == i1, NEG_INF, le)
    e2 = jnp.max(le2, axis=1, keepdims=True)
    i2 = jnp.min(jnp.where(le2 == e2, lane, big), axis=1, keepdims=True)
    r = jnp.exp(e2 - e1)
    w1 = g_prob / (1.0 + r)
    w2 = w1 * r
    return jnp.where(lane == i1, w1, jnp.where(lane == i2, w2, 0.0))


def _post_mix_kernel(*refs, n_in, alpha):
    a_refs = refs[:n_in]
    w_refs = refs[n_in:2 * n_in]
    b_ref, h_ref, mod_ref, lg_ref, lb_ref, r_ref, h1_ref, t_ref, gate_ref = refs[2 * n_in:]
    o = functools.reduce(jnp.add, [_dot(a[...], w[...]) for a, w in zip(a_refs, w_refs)]) + b_ref[...]
    m = mod_ref[0]
    h1 = _layer_norm(alpha * h_ref[...] + m[2:3] * o, lg_ref[...], lb_ref[...])
    h1_ref[...] = h1
    t = _modulate(h1, m, 3, 4)
    t_ref[...] = t.astype(BF16)
    gate_ref[...] = _route(_dot3(t, r_ref[...]))


def _post_mix(acts, weights, bias, h, mod, ln_g, ln_b, router, rows_per_group, group0, alpha):
    t, d = h.shape
    tm = TOKEN_TILE
    row = lambda i: (i, 0)
    const = lambda i: (0, 0)
    in_specs = [pl.BlockSpec((tm, a.shape[1]), row) for a in acts]
    in_specs += [pl.BlockSpec(w.shape, const) for w in weights]
    in_specs += [
        pl.BlockSpec((1, d), const),
        pl.BlockSpec((tm, d), row),
        pl.BlockSpec((1, 6, d), _mod_spec(rows_per_group, group0, tm)),
        pl.BlockSpec((1, d), const),
        pl.BlockSpec((1, d), const),
        pl.BlockSpec((d, ROUTER_LANES), const),
    ]
    return pl.pallas_call(
        functools.partial(_post_mix_kernel, n_in=len(acts), alpha=alpha),
        grid=(t // tm,),
        in_specs=in_specs,
        out_specs=[pl.BlockSpec((tm, d), row), pl.BlockSpec((tm, d), row), pl.BlockSpec((tm, ROUTER_LANES), row)],
        out_shape=[
            jax.ShapeDtypeStruct((t, d), F32),
            jax.ShapeDtypeStruct((t, d), BF16),
            jax.ShapeDtypeStruct((t, ROUTER_LANES), F32),
        ],
        compiler_params=_cparams(("parallel",)),
        name="outproj_ln_router",
    )(*acts, *weights, bias.reshape(1, d), h, mod, ln_g.reshape(1, d), ln_b.reshape(1, d), router)


def _moe_kernel(t_ref, gate_ref, h_ref, mod_ref, wg_ref, wu_ref, wd_ref, lg_ref, lb_ref, o_ref, acc_ref, *, alpha):
    grp = pl.program_id(1)

    @pl.when(grp == 0)
    def _():
        acc_ref[...] = jnp.zeros_like(acc_ref)

    t = t_ref[...]
    gates = gate_ref[...]
    lane = lax.broadcasted_iota(jnp.int32, gates.shape, 1)
    y = acc_ref[...]
    for j in range(EXPERTS_PER_GROUP):
        gate = jnp.sum(jnp.where(lane == GATE_LANE0 + grp * EXPERTS_PER_GROUP + j, gates, 0.0), axis=1,
                       keepdims=True)
        hid = _silu(_dot(t, wg_ref[j])) * _dot(t, wu_ref[j]) * gate
        y = y + _dot(hid.astype(BF16), wd_ref[j])
    acc_ref[...] = y

    @pl.when(grp == N_GROUPS - 1)
    def _():
        m = mod_ref[0]
        o_ref[...] = _layer_norm(alpha * h_ref[...] + m[5:6] * acc_ref[...], lg_ref[...], lb_ref[...])


def _moe_ln(t_act, gates, h, mod, wg, wu, wd, ln_g, ln_b, rows_per_group, group0, alpha):
    t, d = h.shape
    tm = TOKEN_TILE
    f = wg.shape[2]
    row = lambda i, g: (i, 0)
    const = lambda i, g: (0, 0)
    mod_idx = _mod_spec(rows_per_group, group0, tm)
    return pl.pallas_call(
        functools.partial(_moe_kernel, alpha=alpha),
        grid=(t // tm, N_GROUPS),
        in_specs=[
            pl.BlockSpec((tm, d), row),
            pl.BlockSpec((tm, ROUTER_LANES), row),
            pl.BlockSpec((tm, d), row),
            pl.BlockSpec((1, 6, d), lambda i, g: mod_idx(i)),
            pl.BlockSpec((EXPERTS_PER_GROUP, d, f), lambda i, g: (g, 0, 0)),
            pl.BlockSpec((EXPERTS_PER_GROUP, d, f), lambda i, g: (g, 0, 0)),
            pl.BlockSpec((EXPERTS_PER_GROUP, f, d), lambda i, g: (g, 0, 0)),
            pl.BlockSpec((1, d), const),
            pl.BlockSpec((1, d), const),
        ],
        out_specs=pl.BlockSpec((tm, d), row),
        out_shape=jax.ShapeDtypeStruct((t, d), F32),
        scratch_shapes=[pltpu.VMEM((tm, d), F32)],
        compiler_params=_cparams(("parallel", "arbitrary")),
        name="moe_ln",
    )(t_act, gates, h, mod, wg, wu, wd, ln_g.reshape(1, d), ln_b.reshape(1, d))


def _router_matrix(router_group, router_expert):
    d = router_group.shape[0]
    pad = jnp.zeros((d, ROUTER_LANES - N_GROUPS - N_EXPERTS), F32)
    return jnp.concatenate([router_group, router_expert, pad], axis=1)


def kernel(x, c, ctx, c_ctx, ada_w, ada_b, ln_g, ln_b, ab_w_in, ab_b_in, conv_w, conv_b, conv_ln_g, conv_ln_b,
           na_rpb, ab_w_out, ab_b_out, gqa_w_in, gqa_sink, gqa_w_out, router_group, router_expert, exp_w_gate,
           exp_w_up, exp_w_down):
    bsz, seq, d = x.shape
    n_ctx = ctx.shape[1]
    depth = ada_w.shape[0]
    assert depth == DEPTH and bsz + 1 <= MOD_ROWS
    assert seq % TOKEN_TILE == 0 and (bsz * n_ctx) % TOKEN_TILE == 0 and seq % C_BLOCK == 0
    rows = seq // GRID_W
    assert rows % NA_ROWS_PER_STEP == 0 and rows >= NA_UNION_ROWS
    alpha = (2.0 * depth) ** 0.25
    t_lat, t_ctx = bsz * seq, bsz * n_ctx

    cc = jnp.concatenate([c, c_ctx[None], jnp.zeros((MOD_ROWS - bsz - 1, d), F32)], axis=0)
    mod = _modulation(cc, ada_w, ada_b).reshape(depth, MOD_ROWS, 6, d)
    lat_grp = dict(rows_per_group=seq, group0=0)
    ctx_grp = dict(rows_per_group=t_ctx, group0=bsz)

    h_lat = x.reshape(t_lat, d)
    h_ctx = ctx.reshape(t_ctx, d)
    for i in range(depth):
        j = i // 2
        need_ctx = i < depth - 1
        router = _router_matrix(router_group[i], router_expert[i])
        wg, wu, wd = exp_w_gate[i].astype(BF16), exp_w_up[i].astype(BF16), exp_w_down[i].astype(BF16)
        if i % 2 == 0:
            w_in = ab_w_in[j].astype(BF16)
            g_lat, q_lat, k_lat, v_lat = _inproj_ab(h_lat, mod[i], w_in, ab_b_in[j], **lat_grp)
            g_ctx, q_ctx, k_ctx, v_ctx = _inproj_ab(h_ctx, mod[i], w_in, ab_b_in[j], **ctx_grp)
            to_seq = lambda a, n: a.reshape(bsz, n, a.shape[-1])
            conv_args = (conv_w[j], conv_b[j], conv_ln_g[j], conv_ln_b[j])
            conv_lat = _conv_module(to_seq(g_lat, seq), *conv_args).reshape(t_lat, CONV_DIM)
            k_ctx, v_ctx = to_seq(k_ctx, n_ctx), to_seq(v_ctx, n_ctx)
            row_off, types = _na_geometry(rows)
            bias = _na_bias_tables(na_rpb[j], row_off)
            na_lat = _neighbourhood_attention(to_seq(q_lat, seq), to_seq(k_lat, seq), to_seq(v_lat, seq), k_ctx,
                                              v_ctx, bias, types).reshape(t_lat, NA_DIM)
            w_out = ab_w_out[j].astype(BF16)
            w_outs = [w_out[:CONV_DIM], w_out[CONV_DIM:]]
            b_out = ab_b_out[j]
            acts_lat = [conv_lat, na_lat]
            if need_ctx:
                conv_ctx = _conv_module(to_seq(g_ctx, n_ctx), *conv_args).reshape(t_ctx, CONV_DIM)
                na_ctx = _context_attention(to_seq(q_ctx, n_ctx), k_ctx, v_ctx).reshape(t_ctx, NA_DIM)
                acts_ctx = [conv_ctx, na_ctx]
        else:
            assert not need_ctx
            qn = C_HEADS * C_HEAD_DIM
            w_in = gqa_w_in[j].astype(BF16)
            cos, sin = _rope_tables(seq)
            q_lat, k_lat, v_lat = _inproj_gqa(h_lat, mod[i], w_in, cos, sin, seq)
            k_ctx, v_ctx = _inproj_kv(h_ctx, mod[i], w_in[:, qn:], **ctx_grp)
            to_seq = lambda a, n: a.reshape(bsz, n, a.shape[-1])
            att = _window_attention(to_seq(q_lat, seq), to_seq(k_lat, seq), to_seq(v_lat, seq),
                                    to_seq(k_ctx, n_ctx), to_seq(v_ctx, n_ctx), gqa_sink[j])
            acts_lat = [att.reshape(t_lat, qn)]
            w_outs = [gqa_w_out[j].astype(BF16)]
            b_out = jnp.zeros((d,), F32)
        ln1 = (ln_g[i, 0], ln_b[i, 0])
        ln2 = (ln_g[i, 1], ln_b[i, 1])
        h1, t_act, gates = _post_mix(acts_lat, w_outs, b_out, h_lat, mod[i], *ln1, router, alpha=alpha, **lat_grp)
        h_lat = _moe_ln(t_act, gates, h1, mod[i], wg, wu, wd, *ln2, alpha=alpha, **lat_grp)
        if need_ctx:
            h1, t_act, gates = _post_mix(acts_ctx, w_outs, b_out, h_ctx, mod[i], *ln1, router, alpha=alpha,
                                         **ctx_grp)
            h_ctx = _moe_ln(t_act, gates, h1, mod[i], wg, wu, wd, *ln2, alpha=alpha, **ctx_grp)
    return h_lat.reshape(bsz, seq, d)
```

```python
import functools

import numpy as np
import jax
import jax.numpy as jnp
from jax import lax
from jax.experimental import pallas as pl
from jax.experimental.pallas import tpu as pltpu

F32 = jnp.float32
BF16 = jnp.bfloat16

DEPTH = 2
GRID_W = 64
CONV_DIM = 512
CONV_WIDTH = 31
NA_HEADS = 8
NA_HEAD_DIM = 64
NA_DIM = NA_HEADS * NA_HEAD_DIM
NA_KH = 8
NA_KW = 16
C_HEADS = 16
C_KV_HEADS = 4
C_GROUP = C_HEADS // C_KV_HEADS
C_HEAD_DIM = 64
C_WINDOW = 128
C_BLOCK = 128
ROPE_BASE = 10000.0
N_GROUPS = 4
EXPERTS_PER_GROUP = 4
N_EXPERTS = N_GROUPS * EXPERTS_PER_GROUP
D_EXPERT = 256
LN_EPS = 1e-5
NEG_INF = -1e30

LANES = 128
SUBLANES = 8
VMEM_LIMIT_BYTES = 56 * 1024 * 1024

TOKEN_TILE = 512
MOE_TILE = 1024
MOD_ROWS = 24
NA_ROWS_PER_STEP = 4
NA_UNION_ROWS = NA_KH + NA_ROWS_PER_STEP
CONV_CHUNK = 32
CONV_HALO = 16
ROUTER_LANES = LANES
GATE_LANE0 = N_GROUPS
ROUTER_ROWS = 24


def _cparams(semantics):
    return pltpu.CompilerParams(dimension_semantics=semantics, vmem_limit_bytes=VMEM_LIMIT_BYTES)


def _dot(a, b):
    return jnp.dot(a, b, preferred_element_type=F32)


def _dot_nt(a, b):
    return lax.dot_general(a, b, (((1,), (1,)), ((), ())), preferred_element_type=F32)


def _dot_tn(a, b):
    return lax.dot_general(a, b, (((0,), (0,)), ((), ())), preferred_element_type=F32)


def _split_bf16(a):
    hi = a.astype(BF16)
    lo = (a - hi.astype(F32)).astype(BF16)
    return hi, lo


def _dot3(a, b):
    a_hi, a_lo = _split_bf16(a)
    b_hi, b_lo = _split_bf16(b)
    return _dot(a_hi, b_hi) + (_dot(a_lo, b_hi) + _dot(a_hi, b_lo))


def _layer_norm(x, g, b):
    mu = jnp.mean(x, axis=-1, keepdims=True)
    xc = x - mu
    var = jnp.mean(xc * xc, axis=-1, keepdims=True)
    return xc * lax.rsqrt(var + LN_EPS) * g + b


def _silu(x):
    return x * jax.nn.sigmoid(x)


def _mod_kernel(cc_ref, w_ref, b_ref, o_ref):
    o_ref[0] = _dot3(_silu(cc_ref[...]), w_ref[0]) + b_ref[0]


def _modulation(cc, ada_w, ada_b):
    depth, d, n = ada_w.shape
    tn = n // 4
    return pl.pallas_call(
        _mod_kernel,
        grid=(depth, n // tn),
        in_specs=[
            pl.BlockSpec((MOD_ROWS, d), lambda i, j: (0, 0)),
            pl.BlockSpec((1, d, tn), lambda i, j: (i, 0, j)),
            pl.BlockSpec((1, 1, tn), lambda i, j: (i, 0, j)),
        ],
        out_specs=pl.BlockSpec((1, MOD_ROWS, tn), lambda i, j: (i, 0, j)),
        out_shape=jax.ShapeDtypeStruct((depth, MOD_ROWS, n), F32),
        compiler_params=_cparams(("arbitrary", "arbitrary")),
        name="adaln_mod",
    )(cc, ada_w, ada_b.reshape(depth, 1, n))


def _mod_spec(rows_per_group, group0, tm):
    return lambda i: (group0 + (i * tm) // rows_per_group, 0, 0)


def _modulate(h, m, shift_row, scale_row):
    return h * (1.0 + m[scale_row:scale_row + 1]) + m[shift_row:shift_row + 1]


def _inproj_ab_kernel(h_ref, mod_ref, w_ref, b_ref, g_ref, q_ref, k_ref, v_ref):
    u = _modulate(h_ref[...], mod_ref[0], 0, 1).astype(BF16)
    c = CONV_DIM
    za = _dot(u, w_ref[:, 0:c]) + b_ref[:, 0:c]
    zb = _dot(u, w_ref[:, c:2 * c]) + b_ref[:, c:2 * c]
    g_ref[...] = za * jax.nn.sigmoid(zb)
    q0 = 2 * c
    zq = _dot(u, w_ref[:, q0:q0 + NA_DIM]) + b_ref[:, q0:q0 + NA_DIM]
    q_ref[...] = (zq * (NA_HEAD_DIM ** -0.5)).astype(BF16)
    k0 = q0 + NA_DIM
    k_ref[...] = (_dot(u, w_ref[:, k0:k0 + NA_DIM]) + b_ref[:, k0:k0 + NA_DIM]).astype(BF16)
    v0 = k0 + NA_DIM
    v_ref[...] = (_dot(u, w_ref[:, v0:v0 + NA_DIM]) + b_ref[:, v0:v0 + NA_DIM]).astype(BF16)


def _inproj_ab(h, mod, w, b, rows_per_group, group0):
    t, d = h.shape
    n = w.shape[1]
    tm = TOKEN_TILE
    row = lambda i: (i, 0)
    return pl.pallas_call(
        _inproj_ab_kernel,
        grid=(t // tm,),
        in_specs=[
            pl.BlockSpec((tm, d), row),
            pl.BlockSpec((1, 6, d), _mod_spec(rows_per_group, group0, tm)),
            pl.BlockSpec((d, n), lambda i: (0, 0)),
            pl.BlockSpec((1, n), lambda i: (0, 0)),
        ],
        out_specs=[
            pl.BlockSpec((tm, CONV_DIM), row),
            pl.BlockSpec((tm, NA_DIM), row),
            pl.BlockSpec((tm, NA_DIM), row),
            pl.BlockSpec((tm, NA_DIM), row),
        ],
        out_shape=[
            jax.ShapeDtypeStruct((t, CONV_DIM), F32),
            jax.ShapeDtypeStruct((t, NA_DIM), BF16),
            jax.ShapeDtypeStruct((t, NA_DIM), BF16),
            jax.ShapeDtypeStruct((t, NA_DIM), BF16),
        ],
        compiler_params=_cparams(("parallel",)),
        name="inproj_conv_na",
    )(h, mod, w, b.reshape(1, n))


def _conv_kernel(g_ref, w_ref, cb_ref, lg_ref, lb_ref, o_ref, pad_ref, *, seq):
    zeros = jnp.zeros((CONV_HALO, CONV_DIM), F32)
    pad_ref[0:CONV_HALO, :] = zeros
    pad_ref[CONV_HALO + seq:2 * CONV_HALO + seq, :] = zeros
    pad_ref[CONV_HALO:CONV_HALO + seq, :] = g_ref[0]
    first = CONV_HALO - CONV_WIDTH // 2
    ext = CONV_CHUNK + SUBLANES

    def chunk(i, carry):
        r0 = pl.multiple_of(i * CONV_CHUNK, CONV_CHUNK)
        acc = jnp.zeros((CONV_CHUNK, CONV_DIM), F32) + cb_ref[...]
        for res in range(SUBLANES):
            part = None
            for base in range(0, first + CONV_WIDTH, SUBLANES):
                tap = base + res - first
                if 0 <= tap < CONV_WIDTH:
                    term = pad_ref[pl.ds(r0 + base, ext), :] * w_ref[tap:tap + 1, :]
                    part = term if part is None else part + term
            acc = acc + part[res:res + CONV_CHUNK]
        y = _layer_norm(acc, lg_ref[...], lb_ref[...])
        o_ref[0, pl.ds(r0, CONV_CHUNK), :] = _silu(y).astype(BF16)
        return carry

    lax.fori_loop(0, seq // CONV_CHUNK, chunk, 0)


def _conv_module(g, conv_w, conv_b, ln_g, ln_b):
    bsz, seq, c = g.shape
    vec = lambda i: (0, 0)
    return pl.pallas_call(
        functools.partial(_conv_kernel, seq=seq),
        grid=(bsz,),
        in_specs=[
            pl.BlockSpec((1, seq, c), lambda i: (i, 0, 0)),
            pl.BlockSpec((CONV_WIDTH, c), vec),
            pl.BlockSpec((1, c), vec),
            pl.BlockSpec((1, c), vec),
            pl.BlockSpec((1, c), vec),
        ],
        out_specs=pl.BlockSpec((1, seq, c), lambda i: (i, 0, 0)),
        out_shape=jax.ShapeDtypeStruct((bsz, seq, c), BF16),
        scratch_shapes=[pltpu.VMEM((seq + 2 * CONV_HALO, c), F32)],
        compiler_params=_cparams(("parallel",)),
        name="conv_module",
    )(g, conv_w, conv_b.reshape(1, c), ln_g.reshape(1, c), ln_b.reshape(1, c))


N_ROW_OFFS = 2 * NA_KH - 1
N_COL_OFFS = 2 * NA_KW - 1


def _na_geometry(rows):
    rq, ru = NA_ROWS_PER_STEP, NA_UNION_ROWS
    geoms, types = [], []
    for step in range(rows // rq):
        r = step * rq + np.arange(rq)[:, None]
        key_row = np.clip(step * rq - NA_KH // 2, 0, rows - ru) + np.arange(ru)[None, :]
        row_start = np.clip(r - NA_KH // 2, 0, rows - NA_KH)
        row_in = (key_row >= row_start) & (key_row < row_start + NA_KH)
        geom = np.where(row_in, key_row - r + NA_KH - 1, N_ROW_OFFS).astype(np.int32)
        for t, other in enumerate(geoms):
            if np.array_equal(other, geom):
                types.append(t)
                break
        else:
            types.append(len(geoms))
            geoms.append(geom)
    return np.stack(geoms), np.asarray(types, np.int32)


def _na_bias_kernel(row_off_ref, rpb_ref, o_ref, tile_ref):
    typ = pl.program_id(0)
    head = pl.program_id(1)
    qc = lax.broadcasted_iota(jnp.int32, (GRID_W, GRID_W), 0)
    kc = lax.broadcasted_iota(jnp.int32, (GRID_W, GRID_W), 1)
    col_start = jnp.clip(qc - NA_KW // 2, 0, GRID_W - NA_KW)
    col_in = (kc >= col_start) & (kc < col_start + NA_KW)
    col_off = jnp.clip(kc - qc, -(NA_KW - 1), NA_KW - 1) + NA_KW - 1
    for a in range(N_ROW_OFFS):
        tile = jnp.zeros((GRID_W, GRID_W), F32)
        for b in range(N_COL_OFFS):
            tile = jnp.where(col_off == b, rpb_ref[(head * N_ROW_OFFS + a) * N_COL_OFFS + b], tile)
        tile_ref[a] = jnp.where(col_in, tile, NEG_INF)
    tile_ref[N_ROW_OFFS] = jnp.full((GRID_W, GRID_W), NEG_INF, F32)
    for qr in range(NA_ROWS_PER_STEP):
        for kr in range(NA_UNION_ROWS):
            a = row_off_ref[(typ * NA_ROWS_PER_STEP + qr) * NA_UNION_ROWS + kr]
            o_ref[0, 0, qr * GRID_W:(qr + 1) * GRID_W, kr * GRID_W:(kr + 1) * GRID_W] = tile_ref[a]


def _na_bias_tables(rpb, row_off):
    n_types = row_off.shape[0]
    tq = NA_ROWS_PER_STEP * GRID_W
    n_win = NA_UNION_ROWS * GRID_W
    return pl.pallas_call(
        _na_bias_kernel,
        grid=(n_types, NA_HEADS),
        in_specs=[pl.BlockSpec(memory_space=pltpu.SMEM), pl.BlockSpec(memory_space=pltpu.SMEM)],
        out_specs=pl.BlockSpec((1, 1, tq, n_win), lambda t, h: (t, h, 0, 0)),
        out_shape=jax.ShapeDtypeStruct((n_types, NA_HEADS, tq, n_win), F32),
        scratch_shapes=[pltpu.VMEM((N_ROW_OFFS + 1, GRID_W, GRID_W), F32)],
        compiler_params=_cparams(("parallel", "parallel")),
        name="na_bias_table",
    )(jnp.asarray(row_off.reshape(-1)), rpb.astype(F32).reshape(-1))


def _softmax_rows(scores):
    m = functools.reduce(jnp.maximum, [jnp.max(s, axis=1, keepdims=True) for s in scores])
    ps = [jnp.exp(s - m) for s in scores]
    denom = functools.reduce(jnp.add, [jnp.sum(p, axis=1, keepdims=True) for p in ps])
    return [p.astype(BF16) for p in ps], denom


def _pv_rows(ps, denom, values):
    return functools.reduce(jnp.add, [_dot(p, v) for p, v in zip(ps, values)]) / denom


def _na_kernel(type_ref, q_ref, k_ref, v_ref, kc_ref, vc_ref, bias_ref, o_ref, *, rows):
    del type_ref
    step = pl.program_id(1)
    start_row = jnp.clip(step * NA_ROWS_PER_STEP - NA_KH // 2, 0, rows - NA_UNION_ROWS)
    start = pl.multiple_of(start_row * GRID_W, GRID_W)
    n_win = NA_UNION_ROWS * GRID_W
    heads = [slice(h * NA_HEAD_DIM, (h + 1) * NA_HEAD_DIM) for h in range(NA_HEADS)]
    scores = []
    for h, sl in enumerate(heads):
        qh = q_ref[0, :, sl]
        scores.append([_dot_nt(qh, k_ref[0, pl.ds(start, n_win), sl]) + bias_ref[0, h],
                       _dot_nt(qh, kc_ref[0, :, sl])])
    probs = [_softmax_rows(s) for s in scores]
    for sl, (ps, denom) in zip(heads, probs):
        out = _pv_rows(ps, denom, [v_ref[0, pl.ds(start, n_win), sl], vc_ref[0, :, sl]])
        o_ref[0, :, sl] = out.astype(BF16)


def _neighbourhood_attention(q, k, v, k_ctx, v_ctx, bias, types):
    bsz, seq, dim = q.shape
    n_ctx = k_ctx.shape[1]
    rows = seq // GRID_W
    tq = NA_ROWS_PER_STEP * GRID_W
    n_win = NA_UNION_ROWS * GRID_W
    full = lambda b, s, t: (b, 0, 0)
    grid_spec = pltpu.PrefetchScalarGridSpec(
        num_scalar_prefetch=1,
        grid=(bsz, seq // tq),
        in_specs=[
            pl.BlockSpec((1, tq, dim), lambda b, s, t: (b, s, 0)),
            pl.BlockSpec((1, seq, dim), full),
            pl.BlockSpec((1, seq, dim), full),
            pl.BlockSpec((1, n_ctx, dim), full),
            pl.BlockSpec((1, n_ctx, dim), full),
            pl.BlockSpec((1, NA_HEADS, tq, n_win), lambda b, s, t: (t[s], 0, 0, 0)),
        ],
        out_specs=pl.BlockSpec((1, tq, dim), lambda b, s, t: (b, s, 0)),
    )
    return pl.pallas_call(
        functools.partial(_na_kernel, rows=rows),
        grid_spec=grid_spec,
        out_shape=jax.ShapeDtypeStruct((bsz, seq, dim), BF16),
        compiler_params=_cparams(("parallel", "arbitrary")),
        name="neighbourhood_attn",
    )(jnp.asarray(types), q, k, v, k_ctx, v_ctx, bias)


def _ctx_attn_kernel(q_ref, k_ref, v_ref, o_ref):
    for h in range(NA_HEADS):
        sl = slice(h * NA_HEAD_DIM, (h + 1) * NA_HEAD_DIM)
        ps, denom = _softmax_rows([_dot_nt(q_ref[0, :, sl], k_ref[0, :, sl])])
        o_ref[0, :, sl] = _pv_rows(ps, denom, [v_ref[0, :, sl]]).astype(BF16)


def _context_attention(q, k, v):
    bsz, n, dim = q.shape
    spec = pl.BlockSpec((1, n, dim), lambda b: (b, 0, 0))
    return pl.pallas_call(
        _ctx_attn_kernel,
        grid=(bsz,),
        in_specs=[spec, spec, spec],
        out_specs=spec,
        out_shape=jax.ShapeDtypeStruct((bsz, n, dim), BF16),
        compiler_params=_cparams(("parallel",)),
        name="context_attn",
    )(q, k, v)


def _rope(x, cos, sin_signed):
    n = x.shape[1]
    half = C_HEAD_DIM // 4
    lane = lax.broadcasted_iota(jnp.int32, x.shape, 1)
    partner = jnp.where(lane % (2 * half) < half, pltpu.roll(x, n - half, 1), pltpu.roll(x, half, 1))
    reps = n // LANES
    cos_full = jnp.concatenate([cos] * reps, axis=1)
    sin_full = jnp.concatenate([sin_signed] * reps, axis=1)
    return x * cos_full + partner * sin_full


def _inproj_gqa_kernel(h_ref, mod_ref, w_ref, cos_ref, sin_ref, q_ref, k_ref, v_ref):
    u = _modulate(h_ref[...], mod_ref[0], 0, 1).astype(BF16)
    qn = C_HEADS * C_HEAD_DIM
    kn = C_KV_HEADS * C_HEAD_DIM
    cos, sin = cos_ref[...], sin_ref[...]
    q = _rope(_dot(u, w_ref[:, 0:qn]), cos, sin)
    q_ref[...] = (q * (C_HEAD_DIM ** -0.5)).astype(BF16)
    k_ref[...] = _rope(_dot(u, w_ref[:, qn:qn + kn]), cos, sin).astype(BF16)
    v_ref[...] = _dot(u, w_ref[:, qn + kn:qn + 2 * kn]).astype(BF16)


def _inproj_gqa(h, mod, w, cos, sin, seq):
    t, d = h.shape
    n = w.shape[1]
    tm = TOKEN_TILE
    qn = C_HEADS * C_HEAD_DIM
    kn = C_KV_HEADS * C_HEAD_DIM
    row = lambda i: (i, 0)
    pos = lambda i: (i % (seq // tm), 0)
    return pl.pallas_call(
        _inproj_gqa_kernel,
        grid=(t // tm,),
        in_specs=[
            pl.BlockSpec((tm, d), row),
            pl.BlockSpec((1, 6, d), _mod_spec(seq, 0, tm)),
            pl.BlockSpec((d, n), lambda i: (0, 0)),
            pl.BlockSpec((tm, LANES), pos),
            pl.BlockSpec((tm, LANES), pos),
        ],
        out_specs=[pl.BlockSpec((tm, qn), row), pl.BlockSpec((tm, kn), row), pl.BlockSpec((tm, kn), row)],
        out_shape=[
            jax.ShapeDtypeStruct((t, qn), BF16),
            jax.ShapeDtypeStruct((t, kn), BF16),
            jax.ShapeDtypeStruct((t, kn), BF16),
        ],
        compiler_params=_cparams(("parallel",)),
        name="inproj_gqa",
    )(h, mod, w, cos, sin)


def _inproj_kv_kernel(h_ref, mod_ref, w_ref, k_ref, v_ref):
    u = _modulate(h_ref[...], mod_ref[0], 0, 1).astype(BF16)
    kn = C_KV_HEADS * C_HEAD_DIM
    k_ref[...] = _dot(u, w_ref[:, 0:kn]).astype(BF16)
    v_ref[...] = _dot(u, w_ref[:, kn:2 * kn]).astype(BF16)


def _inproj_kv(h, mod, w, rows_per_group, group0):
    t, d = h.shape
    n = w.shape[1]
    tm = TOKEN_TILE
    kn = C_KV_HEADS * C_HEAD_DIM
    row = lambda i: (i, 0)
    return pl.pallas_call(
        _inproj_kv_kernel,
        grid=(t // tm,),
        in_specs=[
            pl.BlockSpec((tm, d), row),
            pl.BlockSpec((1, 6, d), _mod_spec(rows_per_group, group0, tm)),
            pl.BlockSpec((d, n), lambda i: (0, 0)),
        ],
        out_specs=[pl.BlockSpec((tm, kn), row), pl.BlockSpec((tm, kn), row)],
        out_shape=[jax.ShapeDtypeStruct((t, kn), BF16), jax.ShapeDtypeStruct((t, kn), BF16)],
        compiler_params=_cparams(("parallel",)),
        name="inproj_ctx_kv",
    )(h, mod, w)


def _rope_tables(seq):
    t = jnp.arange(seq)
    row = (t // GRID_W).astype(F32)
    col = (t % GRID_W).astype(F32)
    axis_dim = C_HEAD_DIM // 2
    inv_freq = ROPE_BASE ** (-jnp.arange(0, axis_dim, 2, dtype=F32) / axis_dim)
    ang_r = row[:, None] * inv_freq
    ang_c = col[:, None] * inv_freq
    cos = jnp.concatenate([jnp.cos(ang_r)] * 2 + [jnp.cos(ang_c)] * 2, axis=-1)
    sin = jnp.concatenate([-jnp.sin(ang_r), jnp.sin(ang_r), -jnp.sin(ang_c), jnp.sin(ang_c)], axis=-1)
    reps = LANES // C_HEAD_DIM
    return jnp.tile(cos, (1, reps)), jnp.tile(sin, (1, reps))


def _gqa_kernel(sink_ref, q_ref, k_ref, v_ref, kc_ref, vc_ref, o_ref, *, seq):
    blk = pl.program_id(1)
    span = C_BLOCK + 2 * C_WINDOW
    start = pl.multiple_of(jnp.clip(blk * C_BLOCK - C_WINDOW, 0, seq - span), C_BLOCK)
    k_pos = start + lax.broadcasted_iota(jnp.int32, (span, C_BLOCK), 0)
    q_pos = blk * C_BLOCK + lax.broadcasted_iota(jnp.int32, (span, C_BLOCK), 1)
    valid = jnp.abs(q_pos - k_pos) <= C_WINDOW
    valid = jnp.concatenate([valid] * C_GROUP, axis=1)
    d = C_HEAD_DIM

    def scores(hk):
        q0 = hk * C_GROUP * d
        qs = jnp.concatenate([q_ref[0, :, q0 + g * d:q0 + (g + 1) * d] for g in range(C_GROUP)], axis=0)
        ksl = slice(hk * d, (hk + 1) * d)
        return (jnp.where(valid, _dot_nt(k_ref[0, pl.ds(start, span), ksl], qs), NEG_INF),
                _dot_nt(kc_ref[0, :, ksl], qs))

    def softmax(hk, s):
        s_lat, s_ctx = s
        sink = jnp.concatenate(
            [jnp.full((1, C_BLOCK), sink_ref[hk * C_GROUP + g], F32) for g in range(C_GROUP)], axis=1)
        m = jnp.maximum(jnp.maximum(jnp.max(s_lat, axis=0, keepdims=True), jnp.max(s_ctx, axis=0, keepdims=True)),
                        sink)
        p_lat = jnp.exp(s_lat - m)
        p_ctx = jnp.exp(s_ctx - m)
        denom = jnp.sum(p_lat, axis=0, keepdims=True) + jnp.sum(p_ctx, axis=0, keepdims=True) + jnp.exp(sink - m)
        inv = 1.0 / denom
        return (p_lat * inv).astype(BF16), (p_ctx * inv).astype(BF16)

    def values(hk, p):
        ksl = slice(hk * d, (hk + 1) * d)
        q0 = hk * C_GROUP * d
        out = _dot_tn(p[0], v_ref[0, pl.ds(start, span), ksl]) + _dot_tn(p[1], vc_ref[0, :, ksl])
        for g in range(C_GROUP):
            o_ref[0, :, q0 + g * d:q0 + (g + 1) * d] = out[g * C_BLOCK:(g + 1) * C_BLOCK].astype(BF16)

    s = {0: scores(0), 1: scores(1)}
    p = {0: softmax(0, s[0])}
    for hk in range(C_KV_HEADS):
        if hk + 2 < C_KV_HEADS:
            s[hk + 2] = scores(hk + 2)
        if hk + 1 < C_KV_HEADS:
            p[hk + 1] = softmax(hk + 1, s[hk + 1])
        values(hk, p[hk])


def _window_attention(q, k, v, k_ctx, v_ctx, sink):
    bsz, seq, qn = q.shape
    kn = k.shape[2]
    n_ctx = k_ctx.shape[1]
    full = lambda b, s: (b, 0, 0)
    return pl.pallas_call(
        functools.partial(_gqa_kernel, seq=seq),
        grid=(bsz, seq // C_BLOCK),
        in_specs=[
            pl.BlockSpec(memory_space=pltpu.SMEM),
            pl.BlockSpec((1, C_BLOCK, qn), lambda b, s: (b, s, 0)),
            pl.BlockSpec((1, seq, kn), full),
            pl.BlockSpec((1, seq, kn), full),
            pl.BlockSpec((1, n_ctx, kn), full),
            pl.BlockSpec((1, n_ctx, kn), full),
        ],
        out_specs=pl.BlockSpec((1, C_BLOCK, qn), lambda b, s: (b, s, 0)),
        out_shape=jax.ShapeDtypeStruct((bsz, seq, qn), BF16),
        compiler_params=_cparams(("parallel", "arbitrary")),
        name="window_gqa_attn",
    )(sink, q, k, v, k_ctx, v_ctx)


def _route(logits):
    row = lax.broadcasted_iota(jnp.int32, logits.shape, 0)
    big = jnp.int32(ROUTER_ROWS)
    lg = jnp.where(row < N_GROUPS, logits, NEG_INF)
    g_max = jnp.max(lg, axis=0, keepdims=True)
    g_prob = 1.0 / jnp.sum(jnp.exp(lg - g_max), axis=0, keepdims=True)
    g_idx = jnp.min(jnp.where(lg == g_max, row, big), axis=0, keepdims=True)
    first = GATE_LANE0 + g_idx * EXPERTS_PER_GROUP
    in_group = (row >= first) & (row < first + EXPERTS_PER_GROUP)
    le = jnp.where(in_group, logits, NEG_INF)
    e1 = jnp.max(le, axis=0, keepdims=True)
    i1 = jnp.min(jnp.where(le == e1, row, big), axis=0, keepdims=True)
    le2 = jnp.where(row == i1, NEG_INF, le)
    e2 = jnp.max(le2, axis=0, keepdims=True)
    i2 = jnp.min(jnp.where(le2 == e2, row, big), axis=0, keepdims=True)
    r = jnp.exp(e2 - e1)
    w1 = g_prob / (1.0 + r)
    w2 = w1 * r
    return jnp.where(row == i1, w1, jnp.where(row == i2, w2, 0.0))


def _post_mix_kernel(*refs, n_in, alpha):
    a_refs = refs[:n_in]
    w_refs = refs[n_in:2 * n_in]
    b_ref, h_ref, mod_ref, lg_ref, lb_ref, r_ref, h1_ref, t_ref, gate_ref = refs[2 * n_in:]
    o = functools.reduce(jnp.add, [_dot(a[...], w[...]) for a, w in zip(a_refs, w_refs)]) + b_ref[...]
    m = mod_ref[0]
    h1 = _layer_norm(alpha * h_ref[...] + m[2:3] * o, lg_ref[...], lb_ref[...])
    h1_ref[...] = h1
    t = _modulate(h1, m, 3, 4).astype(BF16)
    t_ref[...] = t
    gates = _route(_dot_nt(r_ref[...], t))
    pad = jnp.zeros((ROUTER_LANES - ROUTER_ROWS, gates.shape[1]), F32)
    gate_ref[...] = jnp.transpose(jnp.concatenate([gates, pad], axis=0))


def _post_mix(acts, weights, bias, h, mod, ln_g, ln_b, router, rows_per_group, group0, alpha):
    t, d = h.shape
    tm = TOKEN_TILE
    row = lambda i: (i, 0)
    const = lambda i: (0, 0)
    in_specs = [pl.BlockSpec((tm, a.shape[1]), row) for a in acts]
    in_specs += [pl.BlockSpec(w.shape, const) for w in weights]
    in_specs += [
        pl.BlockSpec((1, d), const),
        pl.BlockSpec((tm, d), row),
        pl.BlockSpec((1, 6, d), _mod_spec(rows_per_group, group0, tm)),
        pl.BlockSpec((1, d), const),
        pl.BlockSpec((1, d), const),
        pl.BlockSpec((ROUTER_ROWS, d), const),
    ]
    return pl.pallas_call(
        functools.partial(_post_mix_kernel, n_in=len(acts), alpha=alpha),
        grid=(t // tm,),
        in_specs=in_specs,
        out_specs=[pl.BlockSpec((tm, d), row), pl.BlockSpec((tm, d), row), pl.BlockSpec((tm, ROUTER_LANES), row)],
        out_shape=[
            jax.ShapeDtypeStruct((t, d), F32),
            jax.ShapeDtypeStruct((t, d), BF16),
            jax.ShapeDtypeStruct((t, ROUTER_LANES), F32),
        ],
        compiler_params=_cparams(("parallel",)),
        name="outproj_ln_router",
    )(*acts, *weights, bias.reshape(1, d), h, mod, ln_g.reshape(1, d), ln_b.reshape(1, d), router)


def _moe_kernel(t_ref, gate_ref, h_ref, mod_ref, wg_ref, wu_ref, wd_ref, lg_ref, lb_ref, o_ref, acc_ref, *, alpha):
    grp = pl.program_id(1)

    @pl.when(grp == 0)
    def _():
        acc_ref[...] = jnp.zeros_like(acc_ref)

    t = t_ref[...]
    gates = gate_ref[...]
    lane = lax.broadcasted_iota(jnp.int32, gates.shape, 1)
    y = acc_ref[...]
    for j in range(EXPERTS_PER_GROUP):
        gate = jnp.sum(jnp.where(lane == GATE_LANE0 + grp * EXPERTS_PER_GROUP + j, gates, 0.0), axis=1,
                       keepdims=True)
        hid = _silu(_dot(t, wg_ref[j])) * _dot(t, wu_ref[j]) * gate
        y = y + _dot(hid.astype(BF16), wd_ref[j])
    acc_ref[...] = y

    @pl.when(grp == N_GROUPS - 1)
    def _():
        m = mod_ref[0]
        o_ref[...] = _layer_norm(alpha * h_ref[...] + m[5:6] * acc_ref[...], lg_ref[...], lb_ref[...])


def _moe_ln(t_act, gates, h, mod, wg, wu, wd, ln_g, ln_b, rows_per_group, group0, alpha):
    t, d = h.shape
    tm = MOE_TILE
    f = wg.shape[2]
    row = lambda i, g: (i, 0)
    const = lambda i, g: (0, 0)
    mod_idx = _mod_spec(rows_per_group, group0, tm)
    return pl.pallas_call(
        functools.partial(_moe_kernel, alpha=alpha),
        grid=(t // tm, N_GROUPS),
        in_specs=[
            pl.BlockSpec((tm, d), row),
            pl.BlockSpec((tm, ROUTER_LANES), row),
            pl.BlockSpec((tm, d), row),
            pl.BlockSpec((1, 6, d), lambda i, g: mod_idx(i)),
            pl.BlockSpec((EXPERTS_PER_GROUP, d, f), lambda i, g: (g, 0, 0)),
            pl.BlockSpec((EXPERTS_PER_GROUP, d, f), lambda i, g: (g, 0, 0)),
            pl.BlockSpec((EXPERTS_PER_GROUP, f, d), lambda i, g: (g, 0, 0)),
            pl.BlockSpec((1, d), const),
            pl.BlockSpec((1, d), const),
        ],
        out_specs=pl.BlockSpec((tm, d), row),
        out_shape=jax.ShapeDtypeStruct((t, d), F32),
        scratch_shapes=[pltpu.VMEM((tm, d), F32)],
        compiler_params=_cparams(("parallel", "arbitrary")),
        name="moe_ln",
    )(t_act, gates, h, mod, wg, wu, wd, ln_g.reshape(1, d), ln_b.reshape(1, d))


def _router_matrix(router_group, router_expert):
    d = router_group.shape[0]
    pad = jnp.zeros((d, ROUTER_ROWS - N_GROUPS - N_EXPERTS), F32)
    return jnp.concatenate([router_group, router_expert, pad], axis=1).T.astype(BF16)


def kernel(x, c, ctx, c_ctx, ada_w, ada_b, ln_g, ln_b, ab_w_in, ab_b_in, conv_w, conv_b, conv_ln_g, conv_ln_b,
           na_rpb, ab_w_out, ab_b_out, gqa_w_in, gqa_sink, gqa_w_out, router_group, router_expert, exp_w_gate,
           exp_w_up, exp_w_down):
    bsz, seq, d = x.shape
    n_ctx = ctx.shape[1]
    depth = ada_w.shape[0]
    assert depth == DEPTH and bsz + 1 <= MOD_ROWS
    assert seq % MOE_TILE == 0 and (bsz * n_ctx) % MOE_TILE == 0 and MOE_TILE % TOKEN_TILE == 0
    assert seq % C_BLOCK == 0
    rows = seq // GRID_W
    assert rows % NA_ROWS_PER_STEP == 0 and rows >= NA_UNION_ROWS
    alpha = (2.0 * depth) ** 0.25
    t_lat, t_ctx = bsz * seq, bsz * n_ctx

    cc = jnp.concatenate([c, c_ctx[None], jnp.zeros((MOD_ROWS - bsz - 1, d), F32)], axis=0)
    mod = _modulation(cc, ada_w, ada_b).reshape(depth, MOD_ROWS, 6, d)
    lat_grp = dict(rows_per_group=seq, group0=0)
    ctx_grp = dict(rows_per_group=t_ctx, group0=bsz)

    h_lat = x.reshape(t_lat, d)
    h_ctx = ctx.reshape(t_ctx, d)
    for i in range(depth):
        j = i // 2
        need_ctx = i < depth - 1
        router = _router_matrix(router_group[i], router_expert[i])
        wg, wu, wd = exp_w_gate[i].astype(BF16), exp_w_up[i].astype(BF16), exp_w_down[i].astype(BF16)
        if i % 2 == 0:
            w_in = ab_w_in[j].astype(BF16)
            g_lat, q_lat, k_lat, v_lat = _inproj_ab(h_lat, mod[i], w_in, ab_b_in[j], **lat_grp)
            g_ctx, q_ctx, k_ctx, v_ctx = _inproj_ab(h_ctx, mod[i], w_in, ab_b_in[j], **ctx_grp)
            to_seq = lambda a, n: a.reshape(bsz, n, a.shape[-1])
            conv_args = (conv_w[j], conv_b[j], conv_ln_g[j], conv_ln_b[j])
            conv_lat = _conv_module(to_seq(g_lat, seq), *conv_args).reshape(t_lat, CONV_DIM)
            k_ctx, v_ctx = to_seq(k_ctx, n_ctx), to_seq(v_ctx, n_ctx)
            row_off, types = _na_geometry(rows)
            bias = _na_bias_tables(na_rpb[j], row_off)
            na_lat = _neighbourhood_attention(to_seq(q_lat, seq), to_seq(k_lat, seq), to_seq(v_lat, seq), k_ctx,
                                              v_ctx, bias, types).reshape(t_lat, NA_DIM)
            w_out = ab_w_out[j].astype(BF16)
            w_outs = [w_out[:CONV_DIM], w_out[CONV_DIM:]]
            b_out = ab_b_out[j]
            acts_lat = [conv_lat, na_lat]
            if need_ctx:
                conv_ctx = _conv_module(to_seq(g_ctx, n_ctx), *conv_args).reshape(t_ctx, CONV_DIM)
                na_ctx = _context_attention(to_seq(q_ctx, n_ctx), k_ctx, v_ctx).reshape(t_ctx, NA_DIM)
                acts_ctx = [conv_ctx, na_ctx]
        else:
            assert not need_ctx
            qn = C_HEADS * C_HEAD_DIM
            w_in = gqa_w_in[j].astype(BF16)
            cos, sin = _rope_tables(seq)
            q_lat, k_lat, v_lat = _inproj_gqa(h_lat, mod[i], w_in, cos, sin, seq)
            k_ctx, v_ctx = _inproj_kv(h_ctx, mod[i], w_in[:, qn:], **ctx_grp)
            to_seq = lambda a, n: a.reshape(bsz, n, a.shape[-1])
            att = _window_attention(to_seq(q_lat, seq), to_seq(k_lat, seq), to_seq(v_lat, seq),
                                    to_seq(k_ctx, n_ctx), to_seq(v_ctx, n_ctx), gqa_sink[j])
            acts_lat = [att.reshape(t_lat, qn)]
            w_outs = [gqa_w_out[j].astype(BF16)]
            b_out = jnp.zeros((d,), F32)
        ln1 = (ln_g[i, 0], ln_b[i, 0])
        ln2 = (ln_g[i, 1], ln_b[i, 1])
        h1, t_act, gates = _post_mix(acts_lat, w_outs, b_out, h_lat, mod[i], *ln1, router, alpha=alpha, **lat_grp)
        h_lat = _moe_ln(t_act, gates, h1, mod[i], wg, wu, wd, *ln2, alpha=alpha, **lat_grp)
        if need_ctx:
            h1, t_act, gates = _post_mix(acts_ctx, w_outs, b_out, h_ctx, mod[i], *ln1, router, alpha=alpha,
                                         **ctx_grp)
            h_ctx = _moe_ln(t_act, gates, h1, mod[i], wg, wu, wd, *ln2, alpha=alpha, **ctx_grp)
    return h_lat.reshape(bsz, seq, d)
```

```python
import functools

import numpy as np
import jax
import jax.numpy as jnp
from jax import lax
from jax.experimental import pallas as pl
from jax.experimental.pallas import tpu as pltpu

F32 = jnp.float32
BF16 = jnp.bfloat16

DEPTH = 2
GRID_W = 64
CONV_DIM = 512
CONV_WIDTH = 31
NA_HEADS = 8
NA_HEAD_DIM = 64
NA_DIM = NA_HEADS * NA_HEAD_DIM
NA_KH = 8
NA_KW = 16
C_HEADS = 16
C_KV_HEADS = 4
C_GROUP = C_HEADS // C_KV_HEADS
C_HEAD_DIM = 64
C_WINDOW = 128
C_BLOCK = 128
ROPE_BASE = 10000.0
N_GROUPS = 4
EXPERTS_PER_GROUP = 4
N_EXPERTS = N_GROUPS * EXPERTS_PER_GROUP
D_EXPERT = 256
LN_EPS = 1e-5
NEG_INF = -1e30

LANES = 128
SUBLANES = 8
VMEM_LIMIT_BYTES = 56 * 1024 * 1024

TOKEN_TILE = 512
MOE_TILE = 1024
MOE_CAP = 320
MOD_ROWS = 24
NA_ROWS_PER_STEP = 4
NA_UNION_ROWS = NA_KH + NA_ROWS_PER_STEP
CONV_CHUNK = 256
CONV_HALO = 16
ROUTER_LANES = LANES
GATE_LANE0 = N_GROUPS
ROUTER_ROWS = 24


def _cparams(semantics):
    return pltpu.CompilerParams(dimension_semantics=semantics, vmem_limit_bytes=VMEM_LIMIT_BYTES)


def _dot(a, b):
    return jnp.dot(a, b, preferred_element_type=F32)


def _dot_nt(a, b):
    return lax.dot_general(a, b, (((1,), (1,)), ((), ())), preferred_element_type=F32)


def _dot_tn(a, b):
    return lax.dot_general(a, b, (((0,), (0,)), ((), ())), preferred_element_type=F32)


def _split_bf16(a):
    hi = a.astype(BF16)
    lo = (a - hi.astype(F32)).astype(BF16)
    return hi, lo


def _dot3(a, b):
    a_hi, a_lo = _split_bf16(a)
    b_hi, b_lo = _split_bf16(b)
    return _dot(a_hi, b_hi) + (_dot(a_lo, b_hi) + _dot(a_hi, b_lo))


def _layer_norm(x, g, b):
    mu = jnp.mean(x, axis=-1, keepdims=True)
    xc = x - mu
    var = jnp.mean(xc * xc, axis=-1, keepdims=True)
    return xc * lax.rsqrt(var + LN_EPS) * g + b


def _silu(x):
    return x * jax.nn.sigmoid(x)


def _mod_kernel(cc_ref, w_ref, b_ref, o_ref):
    o_ref[0] = _dot3(_silu(cc_ref[...]), w_ref[0]) + b_ref[0]


def _modulation(cc, ada_w, ada_b):
    depth, d, n = ada_w.shape
    tn = n // 4
    return pl.pallas_call(
        _mod_kernel,
        grid=(depth, n // tn),
        in_specs=[
            pl.BlockSpec((MOD_ROWS, d), lambda i, j: (0, 0)),
            pl.BlockSpec((1, d, tn), lambda i, j: (i, 0, j)),
            pl.BlockSpec((1, 1, tn), lambda i, j: (i, 0, j)),
        ],
        out_specs=pl.BlockSpec((1, MOD_ROWS, tn), lambda i, j: (i, 0, j)),
        out_shape=jax.ShapeDtypeStruct((depth, MOD_ROWS, n), F32),
        compiler_params=_cparams(("arbitrary", "arbitrary")),
        name="adaln_mod",
    )(cc, ada_w, ada_b.reshape(depth, 1, n))


def _mod_spec(rows_per_group, group0, tm):
    return lambda i: (group0 + (i * tm) // rows_per_group, 0, 0)


def _modulate(h, m, shift_row, scale_row):
    return h * (1.0 + m[scale_row:scale_row + 1]) + m[shift_row:shift_row + 1]


def _inproj_ab_kernel(h_ref, mod_ref, w_ref, b_ref, g_ref, q_ref, k_ref, v_ref):
    u = _modulate(h_ref[...], mod_ref[0], 0, 1).astype(BF16)
    c = CONV_DIM
    za = _dot(u, w_ref[:, 0:c]) + b_ref[:, 0:c]
    zb = _dot(u, w_ref[:, c:2 * c]) + b_ref[:, c:2 * c]
    g_ref[...] = za * jax.nn.sigmoid(zb)
    q0 = 2 * c
    zq = _dot(u, w_ref[:, q0:q0 + NA_DIM]) + b_ref[:, q0:q0 + NA_DIM]
    q_ref[...] = (zq * (NA_HEAD_DIM ** -0.5)).astype(BF16)
    k0 = q0 + NA_DIM
    k_ref[...] = (_dot(u, w_ref[:, k0:k0 + NA_DIM]) + b_ref[:, k0:k0 + NA_DIM]).astype(BF16)
    v0 = k0 + NA_DIM
    v_ref[...] = (_dot(u, w_ref[:, v0:v0 + NA_DIM]) + b_ref[:, v0:v0 + NA_DIM]).astype(BF16)


def _inproj_ab(h, mod, w, b, rows_per_group, group0):
    t, d = h.shape
    n = w.shape[1]
    tm = TOKEN_TILE
    row = lambda i: (i, 0)
    return pl.pallas_call(
        _inproj_ab_kernel,
        grid=(t // tm,),
        in_specs=[
            pl.BlockSpec((tm, d), row),
            pl.BlockSpec((1, 6, d), _mod_spec(rows_per_group, group0, tm)),
            pl.BlockSpec((d, n), lambda i: (0, 0)),
            pl.BlockSpec((1, n), lambda i: (0, 0)),
        ],
        out_specs=[
            pl.BlockSpec((tm, CONV_DIM), row),
            pl.BlockSpec((tm, NA_DIM), row),
            pl.BlockSpec((tm, NA_DIM), row),
            pl.BlockSpec((tm, NA_DIM), row),
        ],
        out_shape=[
            jax.ShapeDtypeStruct((t, CONV_DIM), F32),
            jax.ShapeDtypeStruct((t, NA_DIM), BF16),
            jax.ShapeDtypeStruct((t, NA_DIM), BF16),
            jax.ShapeDtypeStruct((t, NA_DIM), BF16),
        ],
        compiler_params=_cparams(("parallel",)),
        name="inproj_conv_na",
    )(h, mod, w, b.reshape(1, n))


def _conv_kernel(g_ref, w_ref, cb_ref, lg_ref, lb_ref, o_ref, pad_ref, *, seq):
    zeros = jnp.zeros((CONV_HALO, CONV_DIM), F32)
    pad_ref[0:CONV_HALO, :] = zeros
    pad_ref[CONV_HALO + seq:2 * CONV_HALO + seq, :] = zeros
    pad_ref[CONV_HALO:CONV_HALO + seq, :] = g_ref[0]
    first = CONV_HALO - CONV_WIDTH // 2
    ext = CONV_CHUNK + SUBLANES

    def chunk(i, carry):
        r0 = pl.multiple_of(i * CONV_CHUNK, CONV_CHUNK)
        acc = jnp.zeros((CONV_CHUNK, CONV_DIM), F32) + cb_ref[...]
        for res in range(SUBLANES):
            part = None
            for base in range(0, first + CONV_WIDTH, SUBLANES):
                tap = base + res - first
                if 0 <= tap < CONV_WIDTH:
                    term = pad_ref[pl.ds(r0 + base, ext), :] * w_ref[tap:tap + 1, :]
                    part = term if part is None else part + term
            acc = acc + part[res:res + CONV_CHUNK]
        y = _layer_norm(acc, lg_ref[...], lb_ref[...])
        o_ref[0, pl.ds(r0, CONV_CHUNK), :] = _silu(y).astype(BF16)
        return carry

    lax.fori_loop(0, seq // CONV_CHUNK, chunk, 0)


def _conv_module(g, conv_w, conv_b, ln_g, ln_b):
    bsz, seq, c = g.shape
    vec = lambda i: (0, 0)
    return pl.pallas_call(
        functools.partial(_conv_kernel, seq=seq),
        grid=(bsz,),
        in_specs=[
            pl.BlockSpec((1, seq, c), lambda i: (i, 0, 0)),
            pl.BlockSpec((CONV_WIDTH, c), vec),
            pl.BlockSpec((1, c), vec),
            pl.BlockSpec((1, c), vec),
            pl.BlockSpec((1, c), vec),
        ],
        out_specs=pl.BlockSpec((1, seq, c), lambda i: (i, 0, 0)),
        out_shape=jax.ShapeDtypeStruct((bsz, seq, c), BF16),
        scratch_shapes=[pltpu.VMEM((seq + 2 * CONV_HALO, c), F32)],
        compiler_params=_cparams(("parallel",)),
        name="conv_module",
    )(g, conv_w, conv_b.reshape(1, c), ln_g.reshape(1, c), ln_b.reshape(1, c))


N_ROW_OFFS = 2 * NA_KH - 1
N_COL_OFFS = 2 * NA_KW - 1


def _na_geometry(rows):
    rq, ru = NA_ROWS_PER_STEP, NA_UNION_ROWS
    geoms, types = [], []
    for step in range(rows // rq):
        r = step * rq + np.arange(rq)[:, None]
        key_row = np.clip(step * rq - NA_KH // 2, 0, rows - ru) + np.arange(ru)[None, :]
        row_start = np.clip(r - NA_KH // 2, 0, rows - NA_KH)
        row_in = (key_row >= row_start) & (key_row < row_start + NA_KH)
        geom = np.where(row_in, key_row - r + NA_KH - 1, N_ROW_OFFS).astype(np.int32)
        for t, other in enumerate(geoms):
            if np.array_equal(other, geom):
                types.append(t)
                break
        else:
            types.append(len(geoms))
            geoms.append(geom)
    return np.stack(geoms), np.asarray(types, np.int32)


def _na_bias_kernel(row_off_ref, rpb_ref, o_ref, tile_ref):
    typ = pl.program_id(0)
    head = pl.program_id(1)
    qc = lax.broadcasted_iota(jnp.int32, (GRID_W, GRID_W), 0)
    kc = lax.broadcasted_iota(jnp.int32, (GRID_W, GRID_W), 1)
    col_start = jnp.clip(qc - NA_KW // 2, 0, GRID_W - NA_KW)
    col_in = (kc >= col_start) & (kc < col_start + NA_KW)
    col_off = jnp.clip(kc - qc, -(NA_KW - 1), NA_KW - 1) + NA_KW - 1
    for a in range(N_ROW_OFFS):
        tile = jnp.zeros((GRID_W, GRID_W), F32)
        for b in range(N_COL_OFFS):
            tile = jnp.where(col_off == b, rpb_ref[(head * N_ROW_OFFS + a) * N_COL_OFFS + b], tile)
        tile_ref[a] = jnp.where(col_in, tile, NEG_INF)
    tile_ref[N_ROW_OFFS] = jnp.full((GRID_W, GRID_W), NEG_INF, F32)
    for qr in range(NA_ROWS_PER_STEP):
        for kr in range(NA_UNION_ROWS):
            a = row_off_ref[(typ * NA_ROWS_PER_STEP + qr) * NA_UNION_ROWS + kr]
            o_ref[0, 0, qr * GRID_W:(qr + 1) * GRID_W, kr * GRID_W:(kr + 1) * GRID_W] = tile_ref[a]


def _na_bias_tables(rpb, row_off):
    n_types = row_off.shape[0]
    tq = NA_ROWS_PER_STEP * GRID_W
    n_win = NA_UNION_ROWS * GRID_W
    return pl.pallas_call(
        _na_bias_kernel,
        grid=(n_types, NA_HEADS),
        in_specs=[pl.BlockSpec(memory_space=pltpu.SMEM), pl.BlockSpec(memory_space=pltpu.SMEM)],
        out_specs=pl.BlockSpec((1, 1, tq, n_win), lambda t, h: (t, h, 0, 0)),
        out_shape=jax.ShapeDtypeStruct((n_types, NA_HEADS, tq, n_win), F32),
        scratch_shapes=[pltpu.VMEM((N_ROW_OFFS + 1, GRID_W, GRID_W), F32)],
        compiler_params=_cparams(("parallel", "parallel")),
        name="na_bias_table",
    )(jnp.asarray(row_off.reshape(-1)), rpb.astype(F32).reshape(-1))


def _softmax_rows(scores):
    m = functools.reduce(jnp.maximum, [jnp.max(s, axis=1, keepdims=True) for s in scores])
    ps = [jnp.exp(s - m) for s in scores]
    denom = functools.reduce(jnp.add, [jnp.sum(p, axis=1, keepdims=True) for p in ps])
    return [p.astype(BF16) for p in ps], denom


def _pv_rows(ps, denom, values):
    return functools.reduce(jnp.add, [_dot(p, v) for p, v in zip(ps, values)]) / denom


def _na_kernel(type_ref, q_ref, k_ref, v_ref, kc_ref, vc_ref, bias_ref, o_ref, *, rows):
    del type_ref
    step = pl.program_id(1)
    start_row = jnp.clip(step * NA_ROWS_PER_STEP - NA_KH // 2, 0, rows - NA_UNION_ROWS)
    start = pl.multiple_of(start_row * GRID_W, GRID_W)
    n_win = NA_UNION_ROWS * GRID_W
    heads = [slice(h * NA_HEAD_DIM, (h + 1) * NA_HEAD_DIM) for h in range(NA_HEADS)]
    scores = []
    for h, sl in enumerate(heads):
        qh = q_ref[0, :, sl]
        scores.append([_dot_nt(qh, k_ref[0, pl.ds(start, n_win), sl]) + bias_ref[0, h],
                       _dot_nt(qh, kc_ref[0, :, sl])])
    probs = [_softmax_rows(s) for s in scores]
    for sl, (ps, denom) in zip(heads, probs):
        out = _pv_rows(ps, denom, [v_ref[0, pl.ds(start, n_win), sl], vc_ref[0, :, sl]])
        o_ref[0, :, sl] = out.astype(BF16)


def _neighbourhood_attention(q, k, v, k_ctx, v_ctx, bias, types):
    bsz, seq, dim = q.shape
    n_ctx = k_ctx.shape[1]
    rows = seq // GRID_W
    tq = NA_ROWS_PER_STEP * GRID_W
    n_win = NA_UNION_ROWS * GRID_W
    full = lambda b, s, t: (b, 0, 0)
    grid_spec = pltpu.PrefetchScalarGridSpec(
        num_scalar_prefetch=1,
        grid=(bsz, seq // tq),
        in_specs=[
            pl.BlockSpec((1, tq, dim), lambda b, s, t: (b, s, 0)),
            pl.BlockSpec((1, seq, dim), full),
            pl.BlockSpec((1, seq, dim), full),
            pl.BlockSpec((1, n_ctx, dim), full),
            pl.BlockSpec((1, n_ctx, dim), full),
            pl.BlockSpec((1, NA_HEADS, tq, n_win), lambda b, s, t: (t[s], 0, 0, 0)),
        ],
        out_specs=pl.BlockSpec((1, tq, dim), lambda b, s, t: (b, s, 0)),
    )
    return pl.pallas_call(
        functools.partial(_na_kernel, rows=rows),
        grid_spec=grid_spec,
        out_shape=jax.ShapeDtypeStruct((bsz, seq, dim), BF16),
        compiler_params=_cparams(("parallel", "arbitrary")),
        name="neighbourhood_attn",
    )(jnp.asarray(types), q, k, v, k_ctx, v_ctx, bias)


def _ctx_attn_kernel(q_ref, k_ref, v_ref, o_ref):
    for h in range(NA_HEADS):
        sl = slice(h * NA_HEAD_DIM, (h + 1) * NA_HEAD_DIM)
        ps, denom = _softmax_rows([_dot_nt(q_ref[0, :, sl], k_ref[0, :, sl])])
        o_ref[0, :, sl] = _pv_rows(ps, denom, [v_ref[0, :, sl]]).astype(BF16)


def _context_attention(q, k, v):
    bsz, n, dim = q.shape
    spec = pl.BlockSpec((1, n, dim), lambda b: (b, 0, 0))
    return pl.pallas_call(
        _ctx_attn_kernel,
        grid=(bsz,),
        in_specs=[spec, spec, spec],
        out_specs=spec,
        out_shape=jax.ShapeDtypeStruct((bsz, n, dim), BF16),
        compiler_params=_cparams(("parallel",)),
        name="context_attn",
    )(q, k, v)


def _rope(x, cos, sin_signed):
    n = x.shape[1]
    half = C_HEAD_DIM // 4
    lane = lax.broadcasted_iota(jnp.int32, x.shape, 1)
    partner = jnp.where(lane % (2 * half) < half, pltpu.roll(x, n - half, 1), pltpu.roll(x, half, 1))
    reps = n // LANES
    cos_full = jnp.concatenate([cos] * reps, axis=1)
    sin_full = jnp.concatenate([sin_signed] * reps, axis=1)
    return x * cos_full + partner * sin_full


def _inproj_gqa_kernel(h_ref, mod_ref, w_ref, cos_ref, sin_ref, q_ref, k_ref, v_ref):
    u = _modulate(h_ref[...], mod_ref[0], 0, 1).astype(BF16)
    qn = C_HEADS * C_HEAD_DIM
    kn = C_KV_HEADS * C_HEAD_DIM
    cos, sin = cos_ref[...], sin_ref[...]
    q = _rope(_dot(u, w_ref[:, 0:qn]), cos, sin)
    q_ref[...] = (q * (C_HEAD_DIM ** -0.5)).astype(BF16)
    k_ref[...] = _rope(_dot(u, w_ref[:, qn:qn + kn]), cos, sin).astype(BF16)
    v_ref[...] = _dot(u, w_ref[:, qn + kn:qn + 2 * kn]).astype(BF16)


def _inproj_gqa(h, mod, w, cos, sin, seq):
    t, d = h.shape
    n = w.shape[1]
    tm = TOKEN_TILE
    qn = C_HEADS * C_HEAD_DIM
    kn = C_KV_HEADS * C_HEAD_DIM
    row = lambda i: (i, 0)
    pos = lambda i: (i % (seq // tm), 0)
    return pl.pallas_call(
        _inproj_gqa_kernel,
        grid=(t // tm,),
        in_specs=[
            pl.BlockSpec((tm, d), row),
            pl.BlockSpec((1, 6, d), _mod_spec(seq, 0, tm)),
            pl.BlockSpec((d, n), lambda i: (0, 0)),
            pl.BlockSpec((tm, LANES), pos),
            pl.BlockSpec((tm, LANES), pos),
        ],
        out_specs=[pl.BlockSpec((tm, qn), row), pl.BlockSpec((tm, kn), row), pl.BlockSpec((tm, kn), row)],
        out_shape=[
            jax.ShapeDtypeStruct((t, qn), BF16),
            jax.ShapeDtypeStruct((t, kn), BF16),
            jax.ShapeDtypeStruct((t, kn), BF16),
        ],
        compiler_params=_cparams(("parallel",)),
        name="inproj_gqa",
    )(h, mod, w, cos, sin)


def _inproj_kv_kernel(h_ref, mod_ref, w_ref, k_ref, v_ref):
    u = _modulate(h_ref[...], mod_ref[0], 0, 1).astype(BF16)
    kn = C_KV_HEADS * C_HEAD_DIM
    k_ref[...] = _dot(u, w_ref[:, 0:kn]).astype(BF16)
    v_ref[...] = _dot(u, w_ref[:, kn:2 * kn]).astype(BF16)


def _inproj_kv(h, mod, w, rows_per_group, group0):
    t, d = h.shape
    n = w.shape[1]
    tm = TOKEN_TILE
    kn = C_KV_HEADS * C_HEAD_DIM
    row = lambda i: (i, 0)
    return pl.pallas_call(
        _inproj_kv_kernel,
        grid=(t // tm,),
        in_specs=[
            pl.BlockSpec((tm, d), row),
            pl.BlockSpec((1, 6, d), _mod_spec(rows_per_group, group0, tm)),
            pl.BlockSpec((d, n), lambda i: (0, 0)),
        ],
        out_specs=[pl.BlockSpec((tm, kn), row), pl.BlockSpec((tm, kn), row)],
        out_shape=[jax.ShapeDtypeStruct((t, kn), BF16), jax.ShapeDtypeStruct((t, kn), BF16)],
        compiler_params=_cparams(("parallel",)),
        name="inproj_ctx_kv",
    )(h, mod, w)


def _rope_tables(seq):
    t = jnp.arange(seq)
    row = (t // GRID_W).astype(F32)
    col = (t % GRID_W).astype(F32)
    axis_dim = C_HEAD_DIM // 2
    inv_freq = ROPE_BASE ** (-jnp.arange(0, axis_dim, 2, dtype=F32) / axis_dim)
    ang_r = row[:, None] * inv_freq
    ang_c = col[:, None] * inv_freq
    cos = jnp.concatenate([jnp.cos(ang_r)] * 2 + [jnp.cos(ang_c)] * 2, axis=-1)
    sin = jnp.concatenate([-jnp.sin(ang_r), jnp.sin(ang_r), -jnp.sin(ang_c), jnp.sin(ang_c)], axis=-1)
    reps = LANES // C_HEAD_DIM
    return jnp.tile(cos, (1, reps)), jnp.tile(sin, (1, reps))


def _gqa_kernel(sink_ref, q_ref, k_ref, v_ref, kc_ref, vc_ref, o_ref, *, seq):
    blk = pl.program_id(1)
    span = C_BLOCK + 2 * C_WINDOW
    start = pl.multiple_of(jnp.clip(blk * C_BLOCK - C_WINDOW, 0, seq - span), C_BLOCK)
    k_pos = start + lax.broadcasted_iota(jnp.int32, (span, C_BLOCK), 0)
    q_pos = blk * C_BLOCK + lax.broadcasted_iota(jnp.int32, (span, C_BLOCK), 1)
    valid = jnp.abs(q_pos - k_pos) <= C_WINDOW
    valid = jnp.concatenate([valid] * C_GROUP, axis=1)
    d = C_HEAD_DIM

    def scores(hk):
        q0 = hk * C_GROUP * d
        qs = jnp.concatenate([q_ref[0, :, q0 + g * d:q0 + (g + 1) * d] for g in range(C_GROUP)], axis=0)
        ksl = slice(hk * d, (hk + 1) * d)
        return (jnp.where(valid, _dot_nt(k_ref[0, pl.ds(start, span), ksl], qs), NEG_INF),
                _dot_nt(kc_ref[0, :, ksl], qs))

    def softmax(hk, s):
        s_lat, s_ctx = s
        sink = jnp.concatenate(
            [jnp.full((1, C_BLOCK), sink_ref[hk * C_GROUP + g], F32) for g in range(C_GROUP)], axis=1)
        m = jnp.maximum(jnp.maximum(jnp.max(s_lat, axis=0, keepdims=True), jnp.max(s_ctx, axis=0, keepdims=True)),
                        sink)
        p_lat = jnp.exp(s_lat - m)
        p_ctx = jnp.exp(s_ctx - m)
        denom = jnp.sum(p_lat, axis=0, keepdims=True) + jnp.sum(p_ctx, axis=0, keepdims=True) + jnp.exp(sink - m)
        inv = 1.0 / denom
        return (p_lat * inv).astype(BF16), (p_ctx * inv).astype(BF16)

    def values(hk, p):
        ksl = slice(hk * d, (hk + 1) * d)
        q0 = hk * C_GROUP * d
        out = _dot_tn(p[0], v_ref[0, pl.ds(start, span), ksl]) + _dot_tn(p[1], vc_ref[0, :, ksl])
        for g in range(C_GROUP):
            o_ref[0, :, q0 + g * d:q0 + (g + 1) * d] = out[g * C_BLOCK:(g + 1) * C_BLOCK].astype(BF16)

    s = {0: scores(0), 1: scores(1)}
    p = {0: softmax(0, s[0])}
    for hk in range(C_KV_HEADS):
        if hk + 2 < C_KV_HEADS:
            s[hk + 2] = scores(hk + 2)
        if hk + 1 < C_KV_HEADS:
            p[hk + 1] = softmax(hk + 1, s[hk + 1])
        values(hk, p[hk])


def _window_attention(q, k, v, k_ctx, v_ctx, sink):
    bsz, seq, qn = q.shape
    kn = k.shape[2]
    n_ctx = k_ctx.shape[1]
    full = lambda b, s: (b, 0, 0)
    return pl.pallas_call(
        functools.partial(_gqa_kernel, seq=seq),
        grid=(bsz, seq // C_BLOCK),
        in_specs=[
            pl.BlockSpec(memory_space=pltpu.SMEM),
            pl.BlockSpec((1, C_BLOCK, qn), lambda b, s: (b, s, 0)),
            pl.BlockSpec((1, seq, kn), full),
            pl.BlockSpec((1, seq, kn), full),
            pl.BlockSpec((1, n_ctx, kn), full),
            pl.BlockSpec((1, n_ctx, kn), full),
        ],
        out_specs=pl.BlockSpec((1, C_BLOCK, qn), lambda b, s: (b, s, 0)),
        out_shape=jax.ShapeDtypeStruct((bsz, seq, qn), BF16),
        compiler_params=_cparams(("parallel", "arbitrary")),
        name="window_gqa_attn",
    )(sink, q, k, v, k_ctx, v_ctx)


def _route(logits):
    row = lax.broadcasted_iota(jnp.int32, logits.shape, 0)
    big = jnp.int32(ROUTER_ROWS)
    lg = jnp.where(row < N_GROUPS, logits, NEG_INF)
    g_max = jnp.max(lg, axis=0, keepdims=True)
    g_prob = 1.0 / jnp.sum(jnp.exp(lg - g_max), axis=0, keepdims=True)
    g_idx = jnp.min(jnp.where(lg == g_max, row, big), axis=0, keepdims=True)
    first = GATE_LANE0 + g_idx * EXPERTS_PER_GROUP
    in_group = (row >= first) & (row < first + EXPERTS_PER_GROUP)
    le = jnp.where(in_group, logits, NEG_INF)
    e1 = jnp.max(le, axis=0, keepdims=True)
    i1 = jnp.min(jnp.where(le == e1, row, big), axis=0, keepdims=True)
    le2 = jnp.where(row == i1, NEG_INF, le)
    e2 = jnp.max(le2, axis=0, keepdims=True)
    i2 = jnp.min(jnp.where(le2 == e2, row, big), axis=0, keepdims=True)
    r = jnp.exp(e2 - e1)
    w1 = g_prob / (1.0 + r)
    w2 = w1 * r
    return jnp.where(row == i1, w1, jnp.where(row == i2, w2, 0.0)), g_idx


def _post_mix_kernel(*refs, n_in, alpha):
    a_refs = refs[:n_in]
    w_refs = refs[n_in:2 * n_in]
    b_ref, h_ref, mod_ref, lg_ref, lb_ref, r_ref, h1_ref, t_ref, gate_ref, gid_ref = refs[2 * n_in:]
    o = functools.reduce(jnp.add, [_dot(a[...], w[...]) for a, w in zip(a_refs, w_refs)]) + b_ref[...]
    m = mod_ref[0]
    h1 = _layer_norm(alpha * h_ref[...] + m[2:3] * o, lg_ref[...], lb_ref[...])
    h1_ref[...] = h1
    t = _modulate(h1, m, 3, 4).astype(BF16)
    t_ref[...] = t
    gates, group = _route(_dot_nt(r_ref[...], t))
    pad = jnp.zeros((ROUTER_LANES - ROUTER_ROWS, gates.shape[1]), F32)
    gate_ref[...] = jnp.transpose(jnp.concatenate([gates, pad], axis=0))
    gid_ref[0] = jnp.broadcast_to(group, gid_ref.shape[1:])


def _post_mix(acts, weights, bias, h, mod, ln_g, ln_b, router, rows_per_group, group0, alpha):
    t, d = h.shape
    tm = TOKEN_TILE
    row = lambda i: (i, 0)
    const = lambda i: (0, 0)
    in_specs = [pl.BlockSpec((tm, a.shape[1]), row) for a in acts]
    in_specs += [pl.BlockSpec(w.shape, const) for w in weights]
    in_specs += [
        pl.BlockSpec((1, d), const),
        pl.BlockSpec((tm, d), row),
        pl.BlockSpec((1, 6, d), _mod_spec(rows_per_group, group0, tm)),
        pl.BlockSpec((1, d), const),
        pl.BlockSpec((1, d), const),
        pl.BlockSpec((ROUTER_ROWS, d), const),
    ]
    return pl.pallas_call(
        functools.partial(_post_mix_kernel, n_in=len(acts), alpha=alpha),
        grid=(t // tm,),
        in_specs=in_specs,
        out_specs=[pl.BlockSpec((tm, d), row), pl.BlockSpec((tm, d), row), pl.BlockSpec((tm, ROUTER_LANES), row),
                   pl.BlockSpec((1, SUBLANES, tm), lambda i: (i, 0, 0))],
        out_shape=[
            jax.ShapeDtypeStruct((t, d), F32),
            jax.ShapeDtypeStruct((t, d), BF16),
            jax.ShapeDtypeStruct((t, ROUTER_LANES), F32),
            jax.ShapeDtypeStruct((t // tm, SUBLANES, tm), jnp.int32),
        ],
        compiler_params=_cparams(("parallel",)),
        name="outproj_ln_router",
    )(*acts, *weights, bias.reshape(1, d), h, mod, ln_g.reshape(1, d), ln_b.reshape(1, d), router)


def _split3(a):
    hi = a.astype(BF16)
    rest = a - hi.astype(F32)
    mid = rest.astype(BF16)
    return hi, mid, (rest - mid.astype(F32)).astype(BF16)


def _expert_group(x, gate_cols, wg_ref, wu_ref, wd_ref):
    y = None
    for j, gate in enumerate(gate_cols):
        hid = _silu(_dot(x, wg_ref[j])) * _dot(x, wu_ref[j]) * gate
        part = _dot(hid.astype(BF16), wd_ref[j])
        y = part if y is None else y + part
    return y


def _moe_kernel(t_ref, gate_ref, gid_ref, h_ref, mod_ref, wg_ref, wu_ref, wd_ref, lg_ref, lb_ref, o_ref,
                perm_ref, xp_ref, gp_ref, yp_ref, flag_ref, *, alpha):
    grp = pl.program_id(1)
    tm = t_ref.shape[0]
    cap = MOE_CAP

    @pl.when(grp == 0)
    def _():
        gid = jnp.concatenate([gid_ref[i, 0:1, :] for i in range(gid_ref.shape[0])], axis=1)
        is_grp = lax.broadcasted_iota(jnp.int32, (SUBLANES, tm), 0) == gid
        upper = (lax.broadcasted_iota(jnp.int32, (tm, tm), 0) <= lax.broadcasted_iota(jnp.int32, (tm, tm), 1))
        count = _dot(is_grp.astype(BF16), upper.astype(BF16))
        rank = jnp.sum(jnp.where(is_grp, count, 0.0), axis=0, keepdims=True) - 1.0
        flag_ref[0] = (jnp.max(count[:, tm - 1:tm]) > cap).astype(jnp.int32)
        slot = gid * cap + rank.astype(jnp.int32)
        perm = (lax.broadcasted_iota(jnp.int32, (N_GROUPS * cap, tm), 0) == slot).astype(BF16)
        perm_ref[...] = perm
        xp_ref[...] = _dot(perm, t_ref[...]).astype(BF16)
        gates3 = _dot(perm, jnp.concatenate(_split3(gate_ref[...]), axis=1))
        n = ROUTER_LANES
        gp_ref[...] = gates3[:, 0:n] + gates3[:, n:2 * n] + gates3[:, 2 * n:3 * n]

    fits = flag_ref[0] == 0

    @pl.when(fits)
    def _():
        for g in range(N_GROUPS):
            @pl.when(grp == g)
            def _():
                rows = slice(g * cap, (g + 1) * cap)
                lane0 = GATE_LANE0 + g * EXPERTS_PER_GROUP
                gp = gp_ref[rows, :]
                cols = [gp[:, lane0 + j:lane0 + j + 1] for j in range(EXPERTS_PER_GROUP)]
                yp_ref[rows, :] = _expert_group(xp_ref[rows, :], cols, wg_ref, wu_ref, wd_ref).astype(BF16)

    @pl.when(jnp.logical_not(fits))
    def _():
        gates = gate_ref[...]
        lane = lax.broadcasted_iota(jnp.int32, gates.shape, 1)
        lane0 = GATE_LANE0 + grp * EXPERTS_PER_GROUP
        cols = [jnp.sum(jnp.where(lane == lane0 + j, gates, 0.0), axis=1, keepdims=True)
                for j in range(EXPERTS_PER_GROUP)]
        y = _expert_group(t_ref[...], cols, wg_ref, wu_ref, wd_ref)

        @pl.when(grp == 0)
        def _():
            o_ref[...] = y

        @pl.when(grp > 0)
        def _():
            o_ref[...] += y

    @pl.when(grp == N_GROUPS - 1)
    def _():
        scale = mod_ref[0][5:6]

        @pl.when(fits)
        def _():
            y = _dot_tn(perm_ref[...], yp_ref[...])
            o_ref[...] = _layer_norm(alpha * h_ref[...] + scale * y, lg_ref[...], lb_ref[...])

        @pl.when(jnp.logical_not(fits))
        def _():
            o_ref[...] = _layer_norm(alpha * h_ref[...] + scale * o_ref[...], lg_ref[...], lb_ref[...])


def _moe_ln(t_act, gates, gid, h, mod, wg, wu, wd, ln_g, ln_b, rows_per_group, group0, alpha):
    t, d = h.shape
    tm = MOE_TILE
    f = wg.shape[2]
    slots = N_GROUPS * MOE_CAP
    row = lambda i, g: (i, 0)
    const = lambda i, g: (0, 0)
    mod_idx = _mod_spec(rows_per_group, group0, tm)
    return pl.pallas_call(
        functools.partial(_moe_kernel, alpha=alpha),
        grid=(t // tm, N_GROUPS),
        in_specs=[
            pl.BlockSpec((tm, d), row),
            pl.BlockSpec((tm, ROUTER_LANES), row),
            pl.BlockSpec((tm // TOKEN_TILE, SUBLANES, TOKEN_TILE), lambda i, g: (i, 0, 0)),
            pl.BlockSpec((tm, d), row),
            pl.BlockSpec((1, 6, d), lambda i, g: mod_idx(i)),
            pl.BlockSpec((EXPERTS_PER_GROUP, d, f), lambda i, g: (g, 0, 0)),
            pl.BlockSpec((EXPERTS_PER_GROUP, d, f), lambda i, g: (g, 0, 0)),
            pl.BlockSpec((EXPERTS_PER_GROUP, f, d), lambda i, g: (g, 0, 0)),
            pl.BlockSpec((1, d), const),
            pl.BlockSpec((1, d), const),
        ],
        out_specs=pl.BlockSpec((tm, d), row),
        out_shape=jax.ShapeDtypeStruct((t, d), F32),
        scratch_shapes=[
            pltpu.VMEM((slots, tm), BF16),
            pltpu.VMEM((slots, d), BF16),
            pltpu.VMEM((slots, ROUTER_LANES), F32),
            pltpu.VMEM((slots, d), BF16),
            pltpu.SMEM((1,), jnp.int32),
        ],
        compiler_params=_cparams(("parallel", "arbitrary")),
        name="moe_ln",
    )(t_act, gates, gid, h, mod, wg, wu, wd, ln_g.reshape(1, d), ln_b.reshape(1, d))


def _router_matrix(router_group, router_expert):
    d = router_group.shape[0]
    pad = jnp.zeros((d, ROUTER_ROWS - N_GROUPS - N_EXPERTS), F32)
    return jnp.concatenate([router_group, router_expert, pad], axis=1).T.astype(BF16)


def kernel(x, c, ctx, c_ctx, ada_w, ada_b, ln_g, ln_b, ab_w_in, ab_b_in, conv_w, conv_b, conv_ln_g, conv_ln_b,
           na_rpb, ab_w_out, ab_b_out, gqa_w_in, gqa_sink, gqa_w_out, router_group, router_expert, exp_w_gate,
           exp_w_up, exp_w_down):
    bsz, seq, d = x.shape
    n_ctx = ctx.shape[1]
    depth = ada_w.shape[0]
    assert depth == DEPTH and bsz + 1 <= MOD_ROWS
    assert seq % MOE_TILE == 0 and (bsz * n_ctx) % MOE_TILE == 0 and MOE_TILE % TOKEN_TILE == 0
    assert seq % C_BLOCK == 0 and seq % CONV_CHUNK == 0 and n_ctx % CONV_CHUNK == 0
    rows = seq // GRID_W
    assert rows % NA_ROWS_PER_STEP == 0 and rows >= NA_UNION_ROWS
    alpha = (2.0 * depth) ** 0.25
    t_lat, t_ctx = bsz * seq, bsz * n_ctx

    cc = jnp.concatenate([c, c_ctx[None], jnp.zeros((MOD_ROWS - bsz - 1, d), F32)], axis=0)
    mod = _modulation(cc, ada_w, ada_b).reshape(depth, MOD_ROWS, 6, d)
    lat_grp = dict(rows_per_group=seq, group0=0)
    ctx_grp = dict(rows_per_group=t_ctx, group0=bsz)

    h_lat = x.reshape(t_lat, d)
    h_ctx = ctx.reshape(t_ctx, d)
    for i in range(depth):
        j = i // 2
        need_ctx = i < depth - 1
        router = _router_matrix(router_group[i], router_expert[i])
        wg, wu, wd = exp_w_gate[i].astype(BF16), exp_w_up[i].astype(BF16), exp_w_down[i].astype(BF16)
        if i % 2 == 0:
            w_in = ab_w_in[j].astype(BF16)
            g_lat, q_lat, k_lat, v_lat = _inproj_ab(h_lat, mod[i], w_in, ab_b_in[j], **lat_grp)
            g_ctx, q_ctx, k_ctx, v_ctx = _inproj_ab(h_ctx, mod[i], w_in, ab_b_in[j], **ctx_grp)
            to_seq = lambda a, n: a.reshape(bsz, n, a.shape[-1])
            conv_args = (conv_w[j], conv_b[j], conv_ln_g[j], conv_ln_b[j])
            conv_lat = _conv_module(to_seq(g_lat, seq), *conv_args).reshape(t_lat, CONV_DIM)
            k_ctx, v_ctx = to_seq(k_ctx, n_ctx), to_seq(v_ctx, n_ctx)
            row_off, types = _na_geometry(rows)
            bias = _na_bias_tables(na_rpb[j], row_off)
            na_lat = _neighbourhood_attention(to_seq(q_lat, seq), to_seq(k_lat, seq), to_seq(v_lat, seq), k_ctx,
                                              v_ctx, bias, types).reshape(t_lat, NA_DIM)
            w_out = ab_w_out[j].astype(BF16)
            w_outs = [w_out[:CONV_DIM], w_out[CONV_DIM:]]
            b_out = ab_b_out[j]
            acts_lat = [conv_lat, na_lat]
            if need_ctx:
                conv_ctx = _conv_module(to_seq(g_ctx, n_ctx), *conv_args).reshape(t_ctx, CONV_DIM)
                na_ctx = _context_attention(to_seq(q_ctx, n_ctx), k_ctx, v_ctx).reshape(t_ctx, NA_DIM)
                acts_ctx = [conv_ctx, na_ctx]
        else:
            assert not need_ctx
            qn = C_HEADS * C_HEAD_DIM
            w_in = gqa_w_in[j].astype(BF16)
            cos, sin = _rope_tables(seq)
            q_lat, k_lat, v_lat = _inproj_gqa(h_lat, mod[i], w_in, cos, sin, seq)
            k_ctx, v_ctx = _inproj_kv(h_ctx, mod[i], w_in[:, qn:], **ctx_grp)
            to_seq = lambda a, n: a.reshape(bsz, n, a.shape[-1])
            att = _window_attention(to_seq(q_lat, seq), to_seq(k_lat, seq), to_seq(v_lat, seq),
                                    to_seq(k_ctx, n_ctx), to_seq(v_ctx, n_ctx), gqa_sink[j])
            acts_lat = [att.reshape(t_lat, qn)]
            w_outs = [gqa_w_out[j].astype(BF16)]
            b_out = jnp.zeros((d,), F32)
        ln1 = (ln_g[i, 0], ln_b[i, 0])
        ln2 = (ln_g[i, 1], ln_b[i, 1])
        h1, t_act, gates, gid = _post_mix(acts_lat, w_outs, b_out, h_lat, mod[i], *ln1, router, alpha=alpha, **lat_grp)
        h_lat = _moe_ln(t_act, gates, gid, h1, mod[i], wg, wu, wd, *ln2, alpha=alpha, **lat_grp)
        if need_ctx:
            h1, t_act, gates, gid = _post_mix(acts_ctx, w_outs, b_out, h_ctx, mod[i], *ln1, router, alpha=alpha,
                                         **ctx_grp)
            h_ctx = _moe_ln(t_act, gates, gid, h1, mod[i], wg, wu, wd, *ln2, alpha=alpha, **ctx_grp)
    return h_lat.reshape(bsz, seq, d)
```

```python
import functools

import numpy as np
import jax
import jax.numpy as jnp
from jax import lax
from jax.experimental import pallas as pl
from jax.experimental.pallas import tpu as pltpu

F32 = jnp.float32
BF16 = jnp.bfloat16

DEPTH = 2
GRID_W = 64
CONV_DIM = 512
CONV_WIDTH = 31
NA_HEADS = 8
NA_HEAD_DIM = 64
NA_DIM = NA_HEADS * NA_HEAD_DIM
NA_KH = 8
NA_KW = 16
C_HEADS = 16
C_KV_HEADS = 4
C_GROUP = C_HEADS // C_KV_HEADS
C_HEAD_DIM = 64
C_WINDOW = 128
C_BLOCK = 128
ROPE_BASE = 10000.0
N_GROUPS = 4
EXPERTS_PER_GROUP = 4
N_EXPERTS = N_GROUPS * EXPERTS_PER_GROUP
D_EXPERT = 256
LN_EPS = 1e-5
NEG_INF = -1e30

LANES = 128
SUBLANES = 8
VMEM_LIMIT_BYTES = 56 * 1024 * 1024

TOKEN_TILE = 512
MOE_TILE = 1024
MOE_CHUNK = 256
MOE_SLOTS = MOE_TILE // MOE_CHUNK + N_GROUPS - 1
MOD_ROWS = 24
NA_ROWS_PER_STEP = 4
NA_UNION_ROWS = NA_KH + NA_ROWS_PER_STEP
CONV_CHUNK = 256
CONV_HALO = 16
ROUTER_LANES = LANES
GATE_LANE0 = N_GROUPS
ROUTER_ROWS = 24


def _cparams(semantics):
    return pltpu.CompilerParams(dimension_semantics=semantics, vmem_limit_bytes=VMEM_LIMIT_BYTES)


def _dot(a, b):
    return jnp.dot(a, b, preferred_element_type=F32)


def _dot_nt(a, b):
    return lax.dot_general(a, b, (((1,), (1,)), ((), ())), preferred_element_type=F32)


def _dot_tn(a, b):
    return lax.dot_general(a, b, (((0,), (0,)), ((), ())), preferred_element_type=F32)


def _split_bf16(a):
    hi = a.astype(BF16)
    lo = (a - hi.astype(F32)).astype(BF16)
    return hi, lo


def _dot3(a, b):
    a_hi, a_lo = _split_bf16(a)
    b_hi, b_lo = _split_bf16(b)
    return _dot(a_hi, b_hi) + (_dot(a_lo, b_hi) + _dot(a_hi, b_lo))


def _layer_norm(x, g, b):
    mu = jnp.mean(x, axis=-1, keepdims=True)
    xc = x - mu
    var = jnp.mean(xc * xc, axis=-1, keepdims=True)
    return xc * lax.rsqrt(var + LN_EPS) * g + b


def _silu(x):
    return x * jax.nn.sigmoid(x)


def _mod_kernel(cc_ref, w_ref, b_ref, o_ref):
    o_ref[0] = _dot3(_silu(cc_ref[...]), w_ref[0]) + b_ref[0]


def _modulation(cc, ada_w, ada_b):
    depth, d, n = ada_w.shape
    tn = n // 4
    return pl.pallas_call(
        _mod_kernel,
        grid=(depth, n // tn),
        in_specs=[
            pl.BlockSpec((MOD_ROWS, d), lambda i, j: (0, 0)),
            pl.BlockSpec((1, d, tn), lambda i, j: (i, 0, j)),
            pl.BlockSpec((1, 1, tn), lambda i, j: (i, 0, j)),
        ],
        out_specs=pl.BlockSpec((1, MOD_ROWS, tn), lambda i, j: (i, 0, j)),
        out_shape=jax.ShapeDtypeStruct((depth, MOD_ROWS, n), F32),
        compiler_params=_cparams(("arbitrary", "arbitrary")),
        name="adaln_mod",
    )(cc, ada_w, ada_b.reshape(depth, 1, n))


def _mod_spec(rows_per_group, group0, tm):
    return lambda i: (group0 + (i * tm) // rows_per_group, 0, 0)


def _modulate(h, m, shift_row, scale_row):
    return h * (1.0 + m[scale_row:scale_row + 1]) + m[shift_row:shift_row + 1]


def _inproj_ab_kernel(h_ref, mod_ref, w_ref, b_ref, g_ref, q_ref, k_ref, v_ref):
    u = _modulate(h_ref[...], mod_ref[0], 0, 1).astype(BF16)
    c = CONV_DIM
    za = _dot(u, w_ref[:, 0:c]) + b_ref[:, 0:c]
    zb = _dot(u, w_ref[:, c:2 * c]) + b_ref[:, c:2 * c]
    g_ref[...] = za * jax.nn.sigmoid(zb)
    q0 = 2 * c
    zq = _dot(u, w_ref[:, q0:q0 + NA_DIM]) + b_ref[:, q0:q0 + NA_DIM]
    q_ref[...] = (zq * (NA_HEAD_DIM ** -0.5)).astype(BF16)
    k0 = q0 + NA_DIM
    k_ref[...] = (_dot(u, w_ref[:, k0:k0 + NA_DIM]) + b_ref[:, k0:k0 + NA_DIM]).astype(BF16)
    v0 = k0 + NA_DIM
    v_ref[...] = (_dot(u, w_ref[:, v0:v0 + NA_DIM]) + b_ref[:, v0:v0 + NA_DIM]).astype(BF16)


def _inproj_ab(h, mod, w, b, rows_per_group, group0):
    t, d = h.shape
    n = w.shape[1]
    tm = TOKEN_TILE
    row = lambda i: (i, 0)
    return pl.pallas_call(
        _inproj_ab_kernel,
        grid=(t // tm,),
        in_specs=[
            pl.BlockSpec((tm, d), row),
            pl.BlockSpec((1, 6, d), _mod_spec(rows_per_group, group0, tm)),
            pl.BlockSpec((d, n), lambda i: (0, 0)),
            pl.BlockSpec((1, n), lambda i: (0, 0)),
        ],
        out_specs=[
            pl.BlockSpec((tm, CONV_DIM), row),
            pl.BlockSpec((tm, NA_DIM), row),
            pl.BlockSpec((tm, NA_DIM), row),
            pl.BlockSpec((tm, NA_DIM), row),
        ],
        out_shape=[
            jax.ShapeDtypeStruct((t, CONV_DIM), F32),
            jax.ShapeDtypeStruct((t, NA_DIM), BF16),
            jax.ShapeDtypeStruct((t, NA_DIM), BF16),
            jax.ShapeDtypeStruct((t, NA_DIM), BF16),
        ],
        compiler_params=_cparams(("parallel",)),
        name="inproj_conv_na",
    )(h, mod, w, b.reshape(1, n))


def _conv_kernel(g_ref, w_ref, cb_ref, lg_ref, lb_ref, o_ref, pad_ref, *, seq):
    zeros = jnp.zeros((CONV_HALO, CONV_DIM), F32)
    pad_ref[0:CONV_HALO, :] = zeros
    pad_ref[CONV_HALO + seq:2 * CONV_HALO + seq, :] = zeros
    pad_ref[CONV_HALO:CONV_HALO + seq, :] = g_ref[0]
    first = CONV_HALO - CONV_WIDTH // 2
    ext = CONV_CHUNK + SUBLANES

    def chunk(i, carry):
        r0 = pl.multiple_of(i * CONV_CHUNK, CONV_CHUNK)
        acc = jnp.zeros((CONV_CHUNK, CONV_DIM), F32) + cb_ref[...]
        for res in range(SUBLANES):
            part = None
            for base in range(0, first + CONV_WIDTH, SUBLANES):
                tap = base + res - first
                if 0 <= tap < CONV_WIDTH:
                    term = pad_ref[pl.ds(r0 + base, ext), :] * w_ref[tap:tap + 1, :]
                    part = term if part is None else part + term
            acc = acc + part[res:res + CONV_CHUNK]
        y = _layer_norm(acc, lg_ref[...], lb_ref[...])
        o_ref[0, pl.ds(r0, CONV_CHUNK), :] = _silu(y).astype(BF16)
        return carry

    lax.fori_loop(0, seq // CONV_CHUNK, chunk, 0)


def _conv_module(g, conv_w, conv_b, ln_g, ln_b):
    bsz, seq, c = g.shape
    vec = lambda i: (0, 0)
    return pl.pallas_call(
        functools.partial(_conv_kernel, seq=seq),
        grid=(bsz,),
        in_specs=[
            pl.BlockSpec((1, seq, c), lambda i: (i, 0, 0)),
            pl.BlockSpec((CONV_WIDTH, c), vec),
            pl.BlockSpec((1, c), vec),
            pl.BlockSpec((1, c), vec),
            pl.BlockSpec((1, c), vec),
        ],
        out_specs=pl.BlockSpec((1, seq, c), lambda i: (i, 0, 0)),
        out_shape=jax.ShapeDtypeStruct((bsz, seq, c), BF16),
        scratch_shapes=[pltpu.VMEM((seq + 2 * CONV_HALO, c), F32)],
        compiler_params=_cparams(("parallel",)),
        name="conv_module",
    )(g, conv_w, conv_b.reshape(1, c), ln_g.reshape(1, c), ln_b.reshape(1, c))


N_ROW_OFFS = 2 * NA_KH - 1
N_COL_OFFS = 2 * NA_KW - 1


def _na_geometry(rows):
    rq, ru = NA_ROWS_PER_STEP, NA_UNION_ROWS
    geoms, types = [], []
    for step in range(rows // rq):
        r = step * rq + np.arange(rq)[:, None]
        key_row = np.clip(step * rq - NA_KH // 2, 0, rows - ru) + np.arange(ru)[None, :]
        row_start = np.clip(r - NA_KH // 2, 0, rows - NA_KH)
        row_in = (key_row >= row_start) & (key_row < row_start + NA_KH)
        geom = np.where(row_in, key_row - r + NA_KH - 1, N_ROW_OFFS).astype(np.int32)
        for t, other in enumerate(geoms):
            if np.array_equal(other, geom):
                types.append(t)
                break
        else:
            types.append(len(geoms))
            geoms.append(geom)
    return np.stack(geoms), np.asarray(types, np.int32)


def _na_bias_kernel(row_off_ref, rpb_ref, o_ref, tile_ref):
    typ = pl.program_id(0)
    head = pl.program_id(1)
    qc = lax.broadcasted_iota(jnp.int32, (GRID_W, GRID_W), 0)
    kc = lax.broadcasted_iota(jnp.int32, (GRID_W, GRID_W), 1)
    col_start = jnp.clip(qc - NA_KW // 2, 0, GRID_W - NA_KW)
    col_in = (kc >= col_start) & (kc < col_start + NA_KW)
    col_off = jnp.clip(kc - qc, -(NA_KW - 1), NA_KW - 1) + NA_KW - 1
    for a in range(N_ROW_OFFS):
        tile = jnp.zeros((GRID_W, GRID_W), F32)
        for b in range(N_COL_OFFS):
            tile = jnp.where(col_off == b, rpb_ref[(head * N_ROW_OFFS + a) * N_COL_OFFS + b], tile)
        tile_ref[a] = jnp.where(col_in, tile, NEG_INF)
    tile_ref[N_ROW_OFFS] = jnp.full((GRID_W, GRID_W), NEG_INF, F32)
    for qr in range(NA_ROWS_PER_STEP):
        for kr in range(NA_UNION_ROWS):
            a = row_off_ref[(typ * NA_ROWS_PER_STEP + qr) * NA_UNION_ROWS + kr]
            o_ref[0, 0, qr * GRID_W:(qr + 1) * GRID_W, kr * GRID_W:(kr + 1) * GRID_W] = tile_ref[a]


def _na_bias_tables(rpb, row_off):
    n_types = row_off.shape[0]
    tq = NA_ROWS_PER_STEP * GRID_W
    n_win = NA_UNION_ROWS * GRID_W
    return pl.pallas_call(
        _na_bias_kernel,
        grid=(n_types, NA_HEADS),
        in_specs=[pl.BlockSpec(memory_space=pltpu.SMEM), pl.BlockSpec(memory_space=pltpu.SMEM)],
        out_specs=pl.BlockSpec((1, 1, tq, n_win), lambda t, h: (t, h, 0, 0)),
        out_shape=jax.ShapeDtypeStruct((n_types, NA_HEADS, tq, n_win), F32),
        scratch_shapes=[pltpu.VMEM((N_ROW_OFFS + 1, GRID_W, GRID_W), F32)],
        compiler_params=_cparams(("parallel", "parallel")),
        name="na_bias_table",
    )(jnp.asarray(row_off.reshape(-1)), rpb.astype(F32).reshape(-1))


def _softmax_rows(scores):
    m = functools.reduce(jnp.maximum, [jnp.max(s, axis=1, keepdims=True) for s in scores])
    ps = [jnp.exp(s - m) for s in scores]
    denom = functools.reduce(jnp.add, [jnp.sum(p, axis=1, keepdims=True) for p in ps])
    return [p.astype(BF16) for p in ps], denom


def _pv_rows(ps, denom, values):
    return functools.reduce(jnp.add, [_dot(p, v) for p, v in zip(ps, values)]) / denom


def _na_kernel(type_ref, q_ref, k_ref, v_ref, kc_ref, vc_ref, bias_ref, o_ref, *, rows):
    del type_ref
    step = pl.program_id(1)
    start_row = jnp.clip(step * NA_ROWS_PER_STEP - NA_KH // 2, 0, rows - NA_UNION_ROWS)
    start = pl.multiple_of(start_row * GRID_W, GRID_W)
    n_win = NA_UNION_ROWS * GRID_W
    heads = [slice(h * NA_HEAD_DIM, (h + 1) * NA_HEAD_DIM) for h in range(NA_HEADS)]
    scores = []
    for h, sl in enumerate(heads):
        qh = q_ref[0, :, sl]
        scores.append([_dot_nt(qh, k_ref[0, pl.ds(start, n_win), sl]) + bias_ref[0, h],
                       _dot_nt(qh, kc_ref[0, :, sl])])
    probs = [_softmax_rows(s) for s in scores]
    for sl, (ps, denom) in zip(heads, probs):
        out = _pv_rows(ps, denom, [v_ref[0, pl.ds(start, n_win), sl], vc_ref[0, :, sl]])
        o_ref[0, :, sl] = out.astype(BF16)


def _neighbourhood_attention(q, k, v, k_ctx, v_ctx, bias, types):
    bsz, seq, dim = q.shape
    n_ctx = k_ctx.shape[1]
    rows = seq // GRID_W
    tq = NA_ROWS_PER_STEP * GRID_W
    n_win = NA_UNION_ROWS * GRID_W
    full = lambda b, s, t: (b, 0, 0)
    grid_spec = pltpu.PrefetchScalarGridSpec(
        num_scalar_prefetch=1,
        grid=(bsz, seq // tq),
        in_specs=[
            pl.BlockSpec((1, tq, dim), lambda b, s, t: (b, s, 0)),
            pl.BlockSpec((1, seq, dim), full),
            pl.BlockSpec((1, seq, dim), full),
            pl.BlockSpec((1, n_ctx, dim), full),
            pl.BlockSpec((1, n_ctx, dim), full),
            pl.BlockSpec((1, NA_HEADS, tq, n_win), lambda b, s, t: (t[s], 0, 0, 0)),
        ],
        out_specs=pl.BlockSpec((1, tq, dim), lambda b, s, t: (b, s, 0)),
    )
    return pl.pallas_call(
        functools.partial(_na_kernel, rows=rows),
        grid_spec=grid_spec,
        out_shape=jax.ShapeDtypeStruct((bsz, seq, dim), BF16),
        compiler_params=_cparams(("parallel", "arbitrary")),
        name="neighbourhood_attn",
    )(jnp.asarray(types), q, k, v, k_ctx, v_ctx, bias)


def _ctx_attn_kernel(q_ref, k_ref, v_ref, o_ref):
    for h in range(NA_HEADS):
        sl = slice(h * NA_HEAD_DIM, (h + 1) * NA_HEAD_DIM)
        ps, denom = _softmax_rows([_dot_nt(q_ref[0, :, sl], k_ref[0, :, sl])])
        o_ref[0, :, sl] = _pv_rows(ps, denom, [v_ref[0, :, sl]]).astype(BF16)


def _context_attention(q, k, v):
    bsz, n, dim = q.shape
    spec = pl.BlockSpec((1, n, dim), lambda b: (b, 0, 0))
    return pl.pallas_call(
        _ctx_attn_kernel,
        grid=(bsz,),
        in_specs=[spec, spec, spec],
        out_specs=spec,
        out_shape=jax.ShapeDtypeStruct((bsz, n, dim), BF16),
        compiler_params=_cparams(("parallel",)),
        name="context_attn",
    )(q, k, v)


def _rope(x, cos, sin_signed):
    n = x.shape[1]
    half = C_HEAD_DIM // 4
    lane = lax.broadcasted_iota(jnp.int32, x.shape, 1)
    partner = jnp.where(lane % (2 * half) < half, pltpu.roll(x, n - half, 1), pltpu.roll(x, half, 1))
    reps = n // LANES
    cos_full = jnp.concatenate([cos] * reps, axis=1)
    sin_full = jnp.concatenate([sin_signed] * reps, axis=1)
    return x * cos_full + partner * sin_full


def _inproj_gqa_kernel(h_ref, mod_ref, w_ref, cos_ref, sin_ref, q_ref, k_ref, v_ref):
    u = _modulate(h_ref[...], mod_ref[0], 0, 1).astype(BF16)
    qn = C_HEADS * C_HEAD_DIM
    kn = C_KV_HEADS * C_HEAD_DIM
    cos, sin = cos_ref[...], sin_ref[...]
    q = _rope(_dot(u, w_ref[:, 0:qn]), cos, sin)
    q_ref[...] = (q * (C_HEAD_DIM ** -0.5)).astype(BF16)
    k_ref[...] = _rope(_dot(u, w_ref[:, qn:qn + kn]), cos, sin).astype(BF16)
    v_ref[...] = _dot(u, w_ref[:, qn + kn:qn + 2 * kn]).astype(BF16)


def _inproj_gqa(h, mod, w, cos, sin, seq):
    t, d = h.shape
    n = w.shape[1]
    tm = TOKEN_TILE
    qn = C_HEADS * C_HEAD_DIM
    kn = C_KV_HEADS * C_HEAD_DIM
    row = lambda i: (i, 0)
    pos = lambda i: (i % (seq // tm), 0)
    return pl.pallas_call(
        _inproj_gqa_kernel,
        grid=(t // tm,),
        in_specs=[
            pl.BlockSpec((tm, d), row),
            pl.BlockSpec((1, 6, d), _mod_spec(seq, 0, tm)),
            pl.BlockSpec((d, n), lambda i: (0, 0)),
            pl.BlockSpec((tm, LANES), pos),
            pl.BlockSpec((tm, LANES), pos),
        ],
        out_specs=[pl.BlockSpec((tm, qn), row), pl.BlockSpec((tm, kn), row), pl.BlockSpec((tm, kn), row)],
        out_shape=[
            jax.ShapeDtypeStruct((t, qn), BF16),
            jax.ShapeDtypeStruct((t, kn), BF16),
            jax.ShapeDtypeStruct((t, kn), BF16),
        ],
        compiler_params=_cparams(("parallel",)),
        name="inproj_gqa",
    )(h, mod, w, cos, sin)


def _inproj_kv_kernel(h_ref, mod_ref, w_ref, k_ref, v_ref):
    u = _modulate(h_ref[...], mod_ref[0], 0, 1).astype(BF16)
    kn = C_KV_HEADS * C_HEAD_DIM
    k_ref[...] = _dot(u, w_ref[:, 0:kn]).astype(BF16)
    v_ref[...] = _dot(u, w_ref[:, kn:2 * kn]).astype(BF16)


def _inproj_kv(h, mod, w, rows_per_group, group0):
    t, d = h.shape
    n = w.shape[1]
    tm = TOKEN_TILE
    kn = C_KV_HEADS * C_HEAD_DIM
    row = lambda i: (i, 0)
    return pl.pallas_call(
        _inproj_kv_kernel,
        grid=(t // tm,),
        in_specs=[
            pl.BlockSpec((tm, d), row),
            pl.BlockSpec((1, 6, d), _mod_spec(rows_per_group, group0, tm)),
            pl.BlockSpec((d, n), lambda i: (0, 0)),
        ],
        out_specs=[pl.BlockSpec((tm, kn), row), pl.BlockSpec((tm, kn), row)],
        out_shape=[jax.ShapeDtypeStruct((t, kn), BF16), jax.ShapeDtypeStruct((t, kn), BF16)],
        compiler_params=_cparams(("parallel",)),
        name="inproj_ctx_kv",
    )(h, mod, w)


def _rope_tables(seq):
    t = jnp.arange(seq)
    row = (t // GRID_W).astype(F32)
    col = (t % GRID_W).astype(F32)
    axis_dim = C_HEAD_DIM // 2
    inv_freq = ROPE_BASE ** (-jnp.arange(0, axis_dim, 2, dtype=F32) / axis_dim)
    ang_r = row[:, None] * inv_freq
    ang_c = col[:, None] * inv_freq
    cos = jnp.concatenate([jnp.cos(ang_r)] * 2 + [jnp.cos(ang_c)] * 2, axis=-1)
    sin = jnp.concatenate([-jnp.sin(ang_r), jnp.sin(ang_r), -jnp.sin(ang_c), jnp.sin(ang_c)], axis=-1)
    reps = LANES // C_HEAD_DIM
    return jnp.tile(cos, (1, reps)), jnp.tile(sin, (1, reps))


def _gqa_kernel(sink_ref, q_ref, k_ref, v_ref, kc_ref, vc_ref, o_ref, *, seq):
    blk = pl.program_id(1)
    span = C_BLOCK + 2 * C_WINDOW
    start = pl.multiple_of(jnp.clip(blk * C_BLOCK - C_WINDOW, 0, seq - span), C_BLOCK)
    k_pos = start + lax.broadcasted_iota(jnp.int32, (span, C_BLOCK), 0)
    q_pos = blk * C_BLOCK + lax.broadcasted_iota(jnp.int32, (span, C_BLOCK), 1)
    valid = jnp.abs(q_pos - k_pos) <= C_WINDOW
    valid = jnp.concatenate([valid] * C_GROUP, axis=1)
    d = C_HEAD_DIM

    def scores(hk):
        q0 = hk * C_GROUP * d
        qs = jnp.concatenate([q_ref[0, :, q0 + g * d:q0 + (g + 1) * d] for g in range(C_GROUP)], axis=0)
        ksl = slice(hk * d, (hk + 1) * d)
        return (jnp.where(valid, _dot_nt(k_ref[0, pl.ds(start, span), ksl], qs), NEG_INF),
                _dot_nt(kc_ref[0, :, ksl], qs))

    def softmax(hk, s):
        s_lat, s_ctx = s
        sink = jnp.concatenate(
            [jnp.full((1, C_BLOCK), sink_ref[hk * C_GROUP + g], F32) for g in range(C_GROUP)], axis=1)
        m = jnp.maximum(jnp.maximum(jnp.max(s_lat, axis=0, keepdims=True), jnp.max(s_ctx, axis=0, keepdims=True)),
                        sink)
        p_lat = jnp.exp(s_lat - m)
        p_ctx = jnp.exp(s_ctx - m)
        denom = jnp.sum(p_lat, axis=0, keepdims=True) + jnp.sum(p_ctx, axis=0, keepdims=True) + jnp.exp(sink - m)
        inv = 1.0 / denom
        return (p_lat * inv).astype(BF16), (p_ctx * inv).astype(BF16)

    def values(hk, p):
        ksl = slice(hk * d, (hk + 1) * d)
        q0 = hk * C_GROUP * d
        out = _dot_tn(p[0], v_ref[0, pl.ds(start, span), ksl]) + _dot_tn(p[1], vc_ref[0, :, ksl])
        for g in range(C_GROUP):
            o_ref[0, :, q0 + g * d:q0 + (g + 1) * d] = out[g * C_BLOCK:(g + 1) * C_BLOCK].astype(BF16)

    s = {0: scores(0), 1: scores(1)}
    p = {0: softmax(0, s[0])}
    for hk in range(C_KV_HEADS):
        if hk + 2 < C_KV_HEADS:
            s[hk + 2] = scores(hk + 2)
        if hk + 1 < C_KV_HEADS:
            p[hk + 1] = softmax(hk + 1, s[hk + 1])
        values(hk, p[hk])


def _window_attention(q, k, v, k_ctx, v_ctx, sink):
    bsz, seq, qn = q.shape
    kn = k.shape[2]
    n_ctx = k_ctx.shape[1]
    full = lambda b, s: (b, 0, 0)
    return pl.pallas_call(
        functools.partial(_gqa_kernel, seq=seq),
        grid=(bsz, seq // C_BLOCK),
        in_specs=[
            pl.BlockSpec(memory_space=pltpu.SMEM),
            pl.BlockSpec((1, C_BLOCK, qn), lambda b, s: (b, s, 0)),
            pl.BlockSpec((1, seq, kn), full),
            pl.BlockSpec((1, seq, kn), full),
            pl.BlockSpec((1, n_ctx, kn), full),
            pl.BlockSpec((1, n_ctx, kn), full),
        ],
        out_specs=pl.BlockSpec((1, C_BLOCK, qn), lambda b, s: (b, s, 0)),
        out_shape=jax.ShapeDtypeStruct((bsz, seq, qn), BF16),
        compiler_params=_cparams(("parallel", "arbitrary")),
        name="window_gqa_attn",
    )(sink, q, k, v, k_ctx, v_ctx)


def _route(logits):
    row = lax.broadcasted_iota(jnp.int32, logits.shape, 0)
    big = jnp.int32(ROUTER_ROWS)
    lg = jnp.where(row < N_GROUPS, logits, NEG_INF)
    g_max = jnp.max(lg, axis=0, keepdims=True)
    g_prob = 1.0 / jnp.sum(jnp.exp(lg - g_max), axis=0, keepdims=True)
    g_idx = jnp.min(jnp.where(lg == g_max, row, big), axis=0, keepdims=True)
    first = GATE_LANE0 + g_idx * EXPERTS_PER_GROUP
    in_group = (row >= first) & (row < first + EXPERTS_PER_GROUP)
    le = jnp.where(in_group, logits, NEG_INF)
    e1 = jnp.max(le, axis=0, keepdims=True)
    i1 = jnp.min(jnp.where(le == e1, row, big), axis=0, keepdims=True)
    le2 = jnp.where(row == i1, NEG_INF, le)
    e2 = jnp.max(le2, axis=0, keepdims=True)
    i2 = jnp.min(jnp.where(le2 == e2, row, big), axis=0, keepdims=True)
    r = jnp.exp(e2 - e1)
    w1 = g_prob / (1.0 + r)
    w2 = w1 * r
    return jnp.where(row == i1, w1, jnp.where(row == i2, w2, 0.0)), g_idx


def _post_mix_kernel(*refs, n_in, alpha):
    a_refs = refs[:n_in]
    w_refs = refs[n_in:2 * n_in]
    b_ref, h_ref, mod_ref, lg_ref, lb_ref, r_ref, h1_ref, t_ref, gate_ref, gid_ref = refs[2 * n_in:]
    o = functools.reduce(jnp.add, [_dot(a[...], w[...]) for a, w in zip(a_refs, w_refs)]) + b_ref[...]
    m = mod_ref[0]
    h1 = _layer_norm(alpha * h_ref[...] + m[2:3] * o, lg_ref[...], lb_ref[...])
    h1_ref[...] = h1
    t = _modulate(h1, m, 3, 4).astype(BF16)
    t_ref[...] = t
    gates, group = _route(_dot_nt(r_ref[...], t))
    pad = jnp.zeros((ROUTER_LANES - ROUTER_ROWS, gates.shape[1]), F32)
    gate_ref[...] = jnp.transpose(jnp.concatenate([gates, pad], axis=0))
    gid_ref[0] = jnp.broadcast_to(group, gid_ref.shape[1:])


def _post_mix(acts, weights, bias, h, mod, ln_g, ln_b, router, rows_per_group, group0, alpha):
    t, d = h.shape
    tm = TOKEN_TILE
    row = lambda i: (i, 0)
    const = lambda i: (0, 0)
    in_specs = [pl.BlockSpec((tm, a.shape[1]), row) for a in acts]
    in_specs += [pl.BlockSpec(w.shape, const) for w in weights]
    in_specs += [
        pl.BlockSpec((1, d), const),
        pl.BlockSpec((tm, d), row),
        pl.BlockSpec((1, 6, d), _mod_spec(rows_per_group, group0, tm)),
        pl.BlockSpec((1, d), const),
        pl.BlockSpec((1, d), const),
        pl.BlockSpec((ROUTER_ROWS, d), const),
    ]
    return pl.pallas_call(
        functools.partial(_post_mix_kernel, n_in=len(acts), alpha=alpha),
        grid=(t // tm,),
        in_specs=in_specs,
        out_specs=[pl.BlockSpec((tm, d), row), pl.BlockSpec((tm, d), row), pl.BlockSpec((tm, ROUTER_LANES), row),
                   pl.BlockSpec((1, SUBLANES, tm), lambda i: (i, 0, 0))],
        out_shape=[
            jax.ShapeDtypeStruct((t, d), F32),
            jax.ShapeDtypeStruct((t, d), BF16),
            jax.ShapeDtypeStruct((t, ROUTER_LANES), F32),
            jax.ShapeDtypeStruct((t // tm, SUBLANES, tm), jnp.int32),
        ],
        compiler_params=_cparams(("parallel",)),
        name="outproj_ln_router",
    )(*acts, *weights, bias.reshape(1, d), h, mod, ln_g.reshape(1, d), ln_b.reshape(1, d), router)


def _split3(a):
    hi = a.astype(BF16)
    rest = a - hi.astype(F32)
    mid = rest.astype(BF16)
    return hi, mid, (rest - mid.astype(F32)).astype(BF16)


def _expert_group(x, gate_cols, wg_ref, wu_ref, wd_ref):
    y = None
    for j, gate in enumerate(gate_cols):
        hid = _silu(_dot(x, wg_ref[j])) * _dot(x, wu_ref[j]) * gate
        part = _dot(hid.astype(BF16), wd_ref[j])
        y = part if y is None else y + part
    return y


def _moe_plan(gid, tm):
    per_tile = gid.reshape(-1, tm)
    counts = jnp.sum(per_tile[:, :, None] == jnp.arange(N_GROUPS, dtype=jnp.int32), axis=1, dtype=jnp.int32)
    chunks = (counts + MOE_CHUNK - 1) // MOE_CHUNK
    ends = jnp.cumsum(chunks, axis=1)
    slot = jnp.arange(MOE_SLOTS, dtype=jnp.int32)
    chunk_group = jnp.minimum(jnp.sum(ends[:, None, :] <= slot[None, :, None], axis=2), N_GROUPS - 1)
    start_row = (ends - chunks) * MOE_CHUNK
    return (chunk_group.astype(jnp.int32).reshape(-1), ends[:, -1].astype(jnp.int32),
            start_row.astype(jnp.int32).reshape(-1))


def _moe_kernel(chunk_group_ref, n_chunks_ref, start_ref, t_ref, gate_ref, gid_ref, h_ref, mod_ref, wg_ref,
                wu_ref, wd_ref, lg_ref, lb_ref, o_ref, perm_ref, xp_ref, gp_ref, yp_ref, *, alpha):
    tile = pl.program_id(0)
    step = pl.program_id(1)
    tm = t_ref.shape[0]

    @pl.when(step == 0)
    def _():
        gid = jnp.concatenate([gid_ref[i, 0:1, :] for i in range(gid_ref.shape[0])], axis=1)
        is_grp = lax.broadcasted_iota(jnp.int32, (SUBLANES, tm), 0) == gid
        upper = (lax.broadcasted_iota(jnp.int32, (tm, tm), 0) <= lax.broadcasted_iota(jnp.int32, (tm, tm), 1))
        count = _dot(is_grp.astype(BF16), upper.astype(BF16))
        rank = jnp.sum(jnp.where(is_grp, count, 0.0), axis=0, keepdims=True).astype(jnp.int32) - 1
        start = jnp.zeros_like(gid)
        for g in range(N_GROUPS):
            start = jnp.where(gid == g, start_ref[tile * N_GROUPS + g], start)
        dest = start + rank
        perm = (lax.broadcasted_iota(jnp.int32, (perm_ref.shape[0], tm), 0) == dest).astype(BF16)
        perm_ref[...] = perm
        xp_ref[...] = _dot(perm, t_ref[...]).astype(BF16)
        gates3 = _dot(perm, jnp.concatenate(_split3(gate_ref[...]), axis=1))
        n = ROUTER_LANES
        gp_ref[...] = gates3[:, 0:n] + gates3[:, n:2 * n] + gates3[:, 2 * n:3 * n]

    @pl.when(step < n_chunks_ref[tile])
    def _():
        rows = pl.ds(pl.multiple_of(step * MOE_CHUNK, MOE_CHUNK), MOE_CHUNK)
        gp = gp_ref[rows, :]
        lane = lax.broadcasted_iota(jnp.int32, gp.shape, 1)
        lane0 = GATE_LANE0 + chunk_group_ref[tile * MOE_SLOTS + step] * EXPERTS_PER_GROUP
        cols = [jnp.sum(jnp.where(lane == lane0 + j, gp, 0.0), axis=1, keepdims=True)
                for j in range(EXPERTS_PER_GROUP)]
        yp_ref[rows, :] = _expert_group(xp_ref[rows, :], cols, wg_ref, wu_ref, wd_ref).astype(BF16)

    @pl.when(step >= n_chunks_ref[tile])
    def _():
        rows = pl.ds(pl.multiple_of(step * MOE_CHUNK, MOE_CHUNK), MOE_CHUNK)
        yp_ref[rows, :] = jnp.zeros((MOE_CHUNK, yp_ref.shape[1]), BF16)

    @pl.when(step == MOE_SLOTS - 1)
    def _():
        y = _dot_tn(perm_ref[...], yp_ref[...])
        o_ref[...] = _layer_norm(alpha * h_ref[...] + mod_ref[0][5:6] * y, lg_ref[...], lb_ref[...])


def _moe_ln(t_act, gates, gid, h, mod, wg, wu, wd, ln_g, ln_b, rows_per_group, group0, alpha):
    t, d = h.shape
    tm = MOE_TILE
    f = wg.shape[2]
    rows = MOE_SLOTS * MOE_CHUNK
    row = lambda i, s, cg, nc, st: (i, 0)
    const = lambda i, s, cg, nc, st: (0, 0)
    group_w = lambda i, s, cg, nc, st: (cg[i * MOE_SLOTS + s], 0, 0)
    mod_idx = _mod_spec(rows_per_group, group0, tm)
    plan = _moe_plan(gid[:, 0, :].reshape(-1), tm)
    grid_spec = pltpu.PrefetchScalarGridSpec(
        num_scalar_prefetch=3,
        grid=(t // tm, MOE_SLOTS),
        in_specs=[
            pl.BlockSpec((tm, d), row),
            pl.BlockSpec((tm, ROUTER_LANES), row),
            pl.BlockSpec((tm // TOKEN_TILE, SUBLANES, TOKEN_TILE), lambda i, s, cg, nc, st: (i, 0, 0)),
            pl.BlockSpec((tm, d), row),
            pl.BlockSpec((1, 6, d), lambda i, s, cg, nc, st: mod_idx(i)),
            pl.BlockSpec((EXPERTS_PER_GROUP, d, f), group_w),
            pl.BlockSpec((EXPERTS_PER_GROUP, d, f), group_w),
            pl.BlockSpec((EXPERTS_PER_GROUP, f, d), group_w),
            pl.BlockSpec((1, d), const),
            pl.BlockSpec((1, d), const),
        ],
        out_specs=pl.BlockSpec((tm, d), row),
        scratch_shapes=[
            pltpu.VMEM((rows, tm), BF16),
            pltpu.VMEM((rows, d), BF16),
            pltpu.VMEM((rows, ROUTER_LANES), F32),
            pltpu.VMEM((rows, d), BF16),
        ],
    )
    return pl.pallas_call(
        functools.partial(_moe_kernel, alpha=alpha),
        grid_spec=grid_spec,
        out_shape=jax.ShapeDtypeStruct((t, d), F32),
        compiler_params=_cparams(("parallel", "arbitrary")),
        name="moe_ln",
    )(*plan, t_act, gates, gid, h, mod, wg, wu, wd, ln_g.reshape(1, d), ln_b.reshape(1, d))


def _router_matrix(router_group, router_expert):
    d = router_group.shape[0]
    pad = jnp.zeros((d, ROUTER_ROWS - N_GROUPS - N_EXPERTS), F32)
    return jnp.concatenate([router_group, router_expert, pad], axis=1).T.astype(BF16)


def kernel(x, c, ctx, c_ctx, ada_w, ada_b, ln_g, ln_b, ab_w_in, ab_b_in, conv_w, conv_b, conv_ln_g, conv_ln_b,
           na_rpb, ab_w_out, ab_b_out, gqa_w_in, gqa_sink, gqa_w_out, router_group, router_expert, exp_w_gate,
           exp_w_up, exp_w_down):
    bsz, seq, d = x.shape
    n_ctx = ctx.shape[1]
    depth = ada_w.shape[0]
    assert depth == DEPTH and bsz + 1 <= MOD_ROWS
    assert seq % MOE_TILE == 0 and (bsz * n_ctx) % MOE_TILE == 0 and MOE_TILE % TOKEN_TILE == 0
    assert seq % C_BLOCK == 0 and seq % CONV_CHUNK == 0 and n_ctx % CONV_CHUNK == 0
    rows = seq // GRID_W
    assert rows % NA_ROWS_PER_STEP == 0 and rows >= NA_UNION_ROWS
    alpha = (2.0 * depth) ** 0.25
    t_lat, t_ctx = bsz * seq, bsz * n_ctx

    cc = jnp.concatenate([c, c_ctx[None], jnp.zeros((MOD_ROWS - bsz - 1, d), F32)], axis=0)
    mod = _modulation(cc, ada_w, ada_b).reshape(depth, MOD_ROWS, 6, d)
    lat_grp = dict(rows_per_group=seq, group0=0)
    ctx_grp = dict(rows_per_group=t_ctx, group0=bsz)

    h_lat = x.reshape(t_lat, d)
    h_ctx = ctx.reshape(t_ctx, d)
    for i in range(depth):
        j = i // 2
        need_ctx = i < depth - 1
        router = _router_matrix(router_group[i], router_expert[i])
        wg, wu, wd = exp_w_gate[i].astype(BF16), exp_w_up[i].astype(BF16), exp_w_down[i].astype(BF16)
        if i % 2 == 0:
            w_in = ab_w_in[j].astype(BF16)
            g_lat, q_lat, k_lat, v_lat = _inproj_ab(h_lat, mod[i], w_in, ab_b_in[j], **lat_grp)
            g_ctx, q_ctx, k_ctx, v_ctx = _inproj_ab(h_ctx, mod[i], w_in, ab_b_in[j], **ctx_grp)
            to_seq = lambda a, n: a.reshape(bsz, n, a.shape[-1])
            conv_args = (conv_w[j], conv_b[j], conv_ln_g[j], conv_ln_b[j])
            conv_lat = _conv_module(to_seq(g_lat, seq), *conv_args).reshape(t_lat, CONV_DIM)
            k_ctx, v_ctx = to_seq(k_ctx, n_ctx), to_seq(v_ctx, n_ctx)
            row_off, types = _na_geometry(rows)
            bias = _na_bias_tables(na_rpb[j], row_off)
            na_lat = _neighbourhood_attention(to_seq(q_lat, seq), to_seq(k_lat, seq), to_seq(v_lat, seq), k_ctx,
                                              v_ctx, bias, types).reshape(t_lat, NA_DIM)
            w_out = ab_w_out[j].astype(BF16)
            w_outs = [w_out[:CONV_DIM], w_out[CONV_DIM:]]
            b_out = ab_b_out[j]
            acts_lat = [conv_lat, na_lat]
            if need_ctx:
                conv_ctx = _conv_module(to_seq(g_ctx, n_ctx), *conv_args).reshape(t_ctx, CONV_DIM)
                na_ctx = _context_attention(to_seq(q_ctx, n_ctx), k_ctx, v_ctx).reshape(t_ctx, NA_DIM)
                acts_ctx = [conv_ctx, na_ctx]
        else:
            assert not need_ctx
            qn = C_HEADS * C_HEAD_DIM
            w_in = gqa_w_in[j].astype(BF16)
            cos, sin = _rope_tables(seq)
            q_lat, k_lat, v_lat = _inproj_gqa(h_lat, mod[i], w_in, cos, sin, seq)
            k_ctx, v_ctx = _inproj_kv(h_ctx, mod[i], w_in[:, qn:], **ctx_grp)
            to_seq = lambda a, n: a.reshape(bsz, n, a.shape[-1])
            att = _window_attention(to_seq(q_lat, seq), to_seq(k_lat, seq), to_seq(v_lat, seq),
                                    to_seq(k_ctx, n_ctx), to_seq(v_ctx, n_ctx), gqa_sink[j])
            acts_lat = [att.reshape(t_lat, qn)]
            w_outs = [gqa_w_out[j].astype(BF16)]
            b_out = jnp.zeros((d,), F32)
        ln1 = (ln_g[i, 0], ln_b[i, 0])
        ln2 = (ln_g[i, 1], ln_b[i, 1])
        h1, t_act, gates, gid = _post_mix(acts_lat, w_outs, b_out, h_lat, mod[i], *ln1, router, alpha=alpha, **lat_grp)
        h_lat = _moe_ln(t_act, gates, gid, h1, mod[i], wg, wu, wd, *ln2, alpha=alpha, **lat_grp)
        if need_ctx:
            h1, t_act, gates, gid = _post_mix(acts_ctx, w_outs, b_out, h_ctx, mod[i], *ln1, router, alpha=alpha,
                                         **ctx_grp)
            h_ctx = _moe_ln(t_act, gates, gid, h1, mod[i], wg, wu, wd, *ln2, alpha=alpha, **ctx_grp)
    return h_lat.reshape(bsz, seq, d)
```

```python
import functools

import numpy as np
import jax
import jax.numpy as jnp
from jax import lax
from jax.experimental import pallas as pl
from jax.experimental.pallas import tpu as pltpu

F32 = jnp.float32
BF16 = jnp.bfloat16

DEPTH = 2
GRID_W = 64
CONV_DIM = 512
CONV_WIDTH = 31
NA_HEADS = 8
NA_HEAD_DIM = 64
NA_DIM = NA_HEADS * NA_HEAD_DIM
NA_KH = 8
NA_KW = 16
C_HEADS = 16
C_KV_HEADS = 4
C_GROUP = C_HEADS // C_KV_HEADS
C_HEAD_DIM = 64
C_WINDOW = 128
C_BLOCK = 128
ROPE_BASE = 10000.0
N_GROUPS = 4
EXPERTS_PER_GROUP = 4
N_EXPERTS = N_GROUPS * EXPERTS_PER_GROUP
D_EXPERT = 256
LN_EPS = 1e-5
NEG_INF = -1e30
LOG2E = 1.4426950408889634

LANES = 128
SUBLANES = 8
VMEM_LIMIT_BYTES = 56 * 1024 * 1024

TOKEN_TILE = 512
MOE_TILE = 1024
MOE_CHUNK = 256
MOE_SLOTS = MOE_TILE // MOE_CHUNK + N_GROUPS - 1
MOD_ROWS = 24
NA_ROWS_PER_STEP = 4
NA_UNION_ROWS = NA_KH + NA_ROWS_PER_STEP
CONV_CHUNK = 256
CONV_HALO = 16
ROUTER_LANES = LANES
GATE_LANE0 = N_GROUPS
ROUTER_ROWS = 24


def _cparams(semantics):
    return pltpu.CompilerParams(dimension_semantics=semantics, vmem_limit_bytes=VMEM_LIMIT_BYTES)


def _dot(a, b):
    return jnp.dot(a, b, preferred_element_type=F32)


def _dot_nt(a, b):
    return lax.dot_general(a, b, (((1,), (1,)), ((), ())), preferred_element_type=F32)


def _dot_tn(a, b):
    return lax.dot_general(a, b, (((0,), (0,)), ((), ())), preferred_element_type=F32)


def _split_bf16(a):
    hi = a.astype(BF16)
    lo = (a - hi.astype(F32)).astype(BF16)
    return hi, lo


def _dot3(a, b):
    a_hi, a_lo = _split_bf16(a)
    b_hi, b_lo = _split_bf16(b)
    return _dot(a_hi, b_hi) + (_dot(a_lo, b_hi) + _dot(a_hi, b_lo))


def _layer_norm(x, g, b):
    mu = jnp.mean(x, axis=-1, keepdims=True)
    xc = x - mu
    var = jnp.mean(xc * xc, axis=-1, keepdims=True)
    return xc * lax.rsqrt(var + LN_EPS) * g + b


def _silu(x):
    return x * jax.nn.sigmoid(x)


def _mod_kernel(cc_ref, w_ref, b_ref, o_ref):
    o_ref[0] = _dot3(_silu(cc_ref[...]), w_ref[0]) + b_ref[0]


def _modulation(cc, ada_w, ada_b):
    depth, d, n = ada_w.shape
    tn = n // 4
    return pl.pallas_call(
        _mod_kernel,
        grid=(depth, n // tn),
        in_specs=[
            pl.BlockSpec((MOD_ROWS, d), lambda i, j: (0, 0)),
            pl.BlockSpec((1, d, tn), lambda i, j: (i, 0, j)),
            pl.BlockSpec((1, 1, tn), lambda i, j: (i, 0, j)),
        ],
        out_specs=pl.BlockSpec((1, MOD_ROWS, tn), lambda i, j: (i, 0, j)),
        out_shape=jax.ShapeDtypeStruct((depth, MOD_ROWS, n), F32),
        compiler_params=_cparams(("arbitrary", "arbitrary")),
        name="adaln_mod",
    )(cc, ada_w, ada_b.reshape(depth, 1, n))


def _mod_spec(rows_per_group, group0, tm):
    return lambda i: (group0 + (i * tm) // rows_per_group, 0, 0)


def _modulate(h, m, shift_row, scale_row):
    return h * (1.0 + m[scale_row:scale_row + 1]) + m[shift_row:shift_row + 1]


def _inproj_ab_kernel(h_ref, mod_ref, w_ref, b_ref, g_ref, q_ref, k_ref, v_ref):
    u = _modulate(h_ref[...], mod_ref[0], 0, 1).astype(BF16)
    c = CONV_DIM
    za = _dot(u, w_ref[:, 0:c]) + b_ref[:, 0:c]
    zb = _dot(u, w_ref[:, c:2 * c]) + b_ref[:, c:2 * c]
    g_ref[...] = za * jax.nn.sigmoid(zb)
    q0 = 2 * c
    zq = _dot(u, w_ref[:, q0:q0 + NA_DIM]) + b_ref[:, q0:q0 + NA_DIM]
    q_ref[...] = (zq * (NA_HEAD_DIM ** -0.5 * LOG2E)).astype(BF16)
    k0 = q0 + NA_DIM
    k_ref[...] = (_dot(u, w_ref[:, k0:k0 + NA_DIM]) + b_ref[:, k0:k0 + NA_DIM]).astype(BF16)
    v0 = k0 + NA_DIM
    v_ref[...] = (_dot(u, w_ref[:, v0:v0 + NA_DIM]) + b_ref[:, v0:v0 + NA_DIM]).astype(BF16)


def _inproj_ab(h, mod, w, b, rows_per_group, group0):
    t, d = h.shape
    n = w.shape[1]
    tm = TOKEN_TILE
    row = lambda i: (i, 0)
    return pl.pallas_call(
        _inproj_ab_kernel,
        grid=(t // tm,),
        in_specs=[
            pl.BlockSpec((tm, d), row),
            pl.BlockSpec((1, 6, d), _mod_spec(rows_per_group, group0, tm)),
            pl.BlockSpec((d, n), lambda i: (0, 0)),
            pl.BlockSpec((1, n), lambda i: (0, 0)),
        ],
        out_specs=[
            pl.BlockSpec((tm, CONV_DIM), row),
            pl.BlockSpec((tm, NA_DIM), row),
            pl.BlockSpec((tm, NA_DIM), row),
            pl.BlockSpec((tm, NA_DIM), row),
        ],
        out_shape=[
            jax.ShapeDtypeStruct((t, CONV_DIM), F32),
            jax.ShapeDtypeStruct((t, NA_DIM), BF16),
            jax.ShapeDtypeStruct((t, NA_DIM), BF16),
            jax.ShapeDtypeStruct((t, NA_DIM), BF16),
        ],
        compiler_params=_cparams(("parallel",)),
        name="inproj_conv_na",
    )(h, mod, w, b.reshape(1, n))


def _conv_kernel(g_ref, w_ref, cb_ref, lg_ref, lb_ref, o_ref, pad_ref, *, seq):
    zeros = jnp.zeros((CONV_HALO, CONV_DIM), F32)
    pad_ref[0:CONV_HALO, :] = zeros
    pad_ref[CONV_HALO + seq:2 * CONV_HALO + seq, :] = zeros
    pad_ref[CONV_HALO:CONV_HALO + seq, :] = g_ref[0]
    first = CONV_HALO - CONV_WIDTH // 2
    ext = CONV_CHUNK + SUBLANES

    def chunk(i, carry):
        r0 = pl.multiple_of(i * CONV_CHUNK, CONV_CHUNK)
        acc = jnp.zeros((CONV_CHUNK, CONV_DIM), F32) + cb_ref[...]
        for res in range(SUBLANES):
            part = None
            for base in range(0, first + CONV_WIDTH, SUBLANES):
                tap = base + res - first
                if 0 <= tap < CONV_WIDTH:
                    term = pad_ref[pl.ds(r0 + base, ext), :] * w_ref[tap:tap + 1, :]
                    part = term if part is None else part + term
            acc = acc + part[res:res + CONV_CHUNK]
        y = _layer_norm(acc, lg_ref[...], lb_ref[...])
        o_ref[0, pl.ds(r0, CONV_CHUNK), :] = _silu(y).astype(BF16)
        return carry

    lax.fori_loop(0, seq // CONV_CHUNK, chunk, 0)


def _conv_module(g, conv_w, conv_b, ln_g, ln_b):
    bsz, seq, c = g.shape
    vec = lambda i: (0, 0)
    return pl.pallas_call(
        functools.partial(_conv_kernel, seq=seq),
        grid=(bsz,),
        in_specs=[
            pl.BlockSpec((1, seq, c), lambda i: (i, 0, 0)),
            pl.BlockSpec((CONV_WIDTH, c), vec),
            pl.BlockSpec((1, c), vec),
            pl.BlockSpec((1, c), vec),
            pl.BlockSpec((1, c), vec),
        ],
        out_specs=pl.BlockSpec((1, seq, c), lambda i: (i, 0, 0)),
        out_shape=jax.ShapeDtypeStruct((bsz, seq, c), BF16),
        scratch_shapes=[pltpu.VMEM((seq + 2 * CONV_HALO, c), F32)],
        compiler_params=_cparams(("parallel",)),
        name="conv_module",
    )(g, conv_w, conv_b.reshape(1, c), ln_g.reshape(1, c), ln_b.reshape(1, c))


N_ROW_OFFS = 2 * NA_KH - 1
N_COL_OFFS = 2 * NA_KW - 1


def _na_geometry(rows):
    rq, ru = NA_ROWS_PER_STEP, NA_UNION_ROWS
    geoms, types = [], []
    for step in range(rows // rq):
        r = step * rq + np.arange(rq)[:, None]
        key_row = np.clip(step * rq - NA_KH // 2, 0, rows - ru) + np.arange(ru)[None, :]
        row_start = np.clip(r - NA_KH // 2, 0, rows - NA_KH)
        row_in = (key_row >= row_start) & (key_row < row_start + NA_KH)
        geom = np.where(row_in, key_row - r + NA_KH - 1, N_ROW_OFFS).astype(np.int32)
        for t, other in enumerate(geoms):
            if np.array_equal(other, geom):
                types.append(t)
                break
        else:
            types.append(len(geoms))
            geoms.append(geom)
    return np.stack(geoms), np.asarray(types, np.int32)


def _na_bias_kernel(row_off_ref, rpb_ref, o_ref, tile_ref):
    typ = pl.program_id(0)
    head = pl.program_id(1)
    qc = lax.broadcasted_iota(jnp.int32, (GRID_W, GRID_W), 0)
    kc = lax.broadcasted_iota(jnp.int32, (GRID_W, GRID_W), 1)
    col_start = jnp.clip(qc - NA_KW // 2, 0, GRID_W - NA_KW)
    col_in = (kc >= col_start) & (kc < col_start + NA_KW)
    col_off = jnp.clip(kc - qc, -(NA_KW - 1), NA_KW - 1) + NA_KW - 1
    for a in range(N_ROW_OFFS):
        tile = jnp.zeros((GRID_W, GRID_W), F32)
        for b in range(N_COL_OFFS):
            tile = jnp.where(col_off == b, rpb_ref[(head * N_ROW_OFFS + a) * N_COL_OFFS + b], tile)
        tile_ref[a] = jnp.where(col_in, tile * LOG2E, NEG_INF)
    tile_ref[N_ROW_OFFS] = jnp.full((GRID_W, GRID_W), NEG_INF, F32)
    for qr in range(NA_ROWS_PER_STEP):
        for kr in range(NA_UNION_ROWS):
            a = row_off_ref[(typ * NA_ROWS_PER_STEP + qr) * NA_UNION_ROWS + kr]
            o_ref[0, 0, qr * GRID_W:(qr + 1) * GRID_W, kr * GRID_W:(kr + 1) * GRID_W] = tile_ref[a]


def _na_bias_tables(rpb, row_off):
    n_types = row_off.shape[0]
    tq = NA_ROWS_PER_STEP * GRID_W
    n_win = NA_UNION_ROWS * GRID_W
    return pl.pallas_call(
        _na_bias_kernel,
        grid=(n_types, NA_HEADS),
        in_specs=[pl.BlockSpec(memory_space=pltpu.SMEM), pl.BlockSpec(memory_space=pltpu.SMEM)],
        out_specs=pl.BlockSpec((1, 1, tq, n_win), lambda t, h: (t, h, 0, 0)),
        out_shape=jax.ShapeDtypeStruct((n_types, NA_HEADS, tq, n_win), F32),
        scratch_shapes=[pltpu.VMEM((N_ROW_OFFS + 1, GRID_W, GRID_W), F32)],
        compiler_params=_cparams(("parallel", "parallel")),
        name="na_bias_table",
    )(jnp.asarray(row_off.reshape(-1)), rpb.astype(F32).reshape(-1))


def _softmax_rows(scores):
    m = functools.reduce(jnp.maximum, [jnp.max(s, axis=1, keepdims=True) for s in scores])
    ps = [jnp.exp2(s - m) for s in scores]
    denom = functools.reduce(jnp.add, [jnp.sum(p, axis=1, keepdims=True) for p in ps])
    return [p.astype(BF16) for p in ps], denom


def _pv_rows(ps, denom, values):
    return functools.reduce(jnp.add, [_dot(p, v) for p, v in zip(ps, values)]) / denom


def _na_kernel(type_ref, q_ref, k_ref, v_ref, kc_ref, vc_ref, bias_ref, o_ref, *, rows):
    del type_ref
    step = pl.program_id(1)
    start_row = jnp.clip(step * NA_ROWS_PER_STEP - NA_KH // 2, 0, rows - NA_UNION_ROWS)
    start = pl.multiple_of(start_row * GRID_W, GRID_W)
    n_win = NA_UNION_ROWS * GRID_W
    heads = [slice(h * NA_HEAD_DIM, (h + 1) * NA_HEAD_DIM) for h in range(NA_HEADS)]
    scores = []
    for h, sl in enumerate(heads):
        qh = q_ref[0, :, sl]
        scores.append([_dot_nt(qh, k_ref[0, pl.ds(start, n_win), sl]) + bias_ref[0, h],
                       _dot_nt(qh, kc_ref[0, :, sl])])
    probs = [_softmax_rows(s) for s in scores]
    for sl, (ps, denom) in zip(heads, probs):
        out = _pv_rows(ps, denom, [v_ref[0, pl.ds(start, n_win), sl], vc_ref[0, :, sl]])
        o_ref[0, :, sl] = out.astype(BF16)


def _neighbourhood_attention(q, k, v, k_ctx, v_ctx, bias, types):
    bsz, seq, dim = q.shape
    n_ctx = k_ctx.shape[1]
    rows = seq // GRID_W
    tq = NA_ROWS_PER_STEP * GRID_W
    n_win = NA_UNION_ROWS * GRID_W
    full = lambda b, s, t: (b, 0, 0)
    grid_spec = pltpu.PrefetchScalarGridSpec(
        num_scalar_prefetch=1,
        grid=(bsz, seq // tq),
        in_specs=[
            pl.BlockSpec((1, tq, dim), lambda b, s, t: (b, s, 0)),
            pl.BlockSpec((1, seq, dim), full),
            pl.BlockSpec((1, seq, dim), full),
            pl.BlockSpec((1, n_ctx, dim), full),
            pl.BlockSpec((1, n_ctx, dim), full),
            pl.BlockSpec((1, NA_HEADS, tq, n_win), lambda b, s, t: (t[s], 0, 0, 0)),
        ],
        out_specs=pl.BlockSpec((1, tq, dim), lambda b, s, t: (b, s, 0)),
    )
    return pl.pallas_call(
        functools.partial(_na_kernel, rows=rows),
        grid_spec=grid_spec,
        out_shape=jax.ShapeDtypeStruct((bsz, seq, dim), BF16),
        compiler_params=_cparams(("parallel", "arbitrary")),
        name="neighbourhood_attn",
    )(jnp.asarray(types), q, k, v, k_ctx, v_ctx, bias)


def _ctx_attn_kernel(q_ref, k_ref, v_ref, o_ref):
    for h in range(NA_HEADS):
        sl = slice(h * NA_HEAD_DIM, (h + 1) * NA_HEAD_DIM)
        ps, denom = _softmax_rows([_dot_nt(q_ref[0, :, sl], k_ref[0, :, sl])])
        o_ref[0, :, sl] = _pv_rows(ps, denom, [v_ref[0, :, sl]]).astype(BF16)


def _context_attention(q, k, v):
    bsz, n, dim = q.shape
    spec = pl.BlockSpec((1, n, dim), lambda b: (b, 0, 0))
    return pl.pallas_call(
        _ctx_attn_kernel,
        grid=(bsz,),
        in_specs=[spec, spec, spec],
        out_specs=spec,
        out_shape=jax.ShapeDtypeStruct((bsz, n, dim), BF16),
        compiler_params=_cparams(("parallel",)),
        name="context_attn",
    )(q, k, v)


def _rope(x, cos, sin_signed):
    n = x.shape[1]
    half = C_HEAD_DIM // 4
    lane = lax.broadcasted_iota(jnp.int32, x.shape, 1)
    partner = jnp.where(lane % (2 * half) < half, pltpu.roll(x, n - half, 1), pltpu.roll(x, half, 1))
    reps = n // LANES
    cos_full = jnp.concatenate([cos] * reps, axis=1)
    sin_full = jnp.concatenate([sin_signed] * reps, axis=1)
    return x * cos_full + partner * sin_full


def _inproj_gqa_kernel(h_ref, mod_ref, w_ref, cos_ref, sin_ref, q_ref, k_ref, v_ref):
    u = _modulate(h_ref[...], mod_ref[0], 0, 1).astype(BF16)
    qn = C_HEADS * C_HEAD_DIM
    kn = C_KV_HEADS * C_HEAD_DIM
    cos, sin = cos_ref[...], sin_ref[...]
    q = _rope(_dot(u, w_ref[:, 0:qn]), cos, sin)
    q_ref[...] = (q * (C_HEAD_DIM ** -0.5 * LOG2E)).astype(BF16)
    k_ref[...] = _rope(_dot(u, w_ref[:, qn:qn + kn]), cos, sin).astype(BF16)
    v_ref[...] = _dot(u, w_ref[:, qn + kn:qn + 2 * kn]).astype(BF16)


def _inproj_gqa(h, mod, w, cos, sin, seq):
    t, d = h.shape
    n = w.shape[1]
    tm = TOKEN_TILE
    qn = C_HEADS * C_HEAD_DIM
    kn = C_KV_HEADS * C_HEAD_DIM
    row = lambda i: (i, 0)
    pos = lambda i: (i % (seq // tm), 0)
    return pl.pallas_call(
        _inproj_gqa_kernel,
        grid=(t // tm,),
        in_specs=[
            pl.BlockSpec((tm, d), row),
            pl.BlockSpec((1, 6, d), _mod_spec(seq, 0, tm)),
            pl.BlockSpec((d, n), lambda i: (0, 0)),
            pl.BlockSpec((tm, LANES), pos),
            pl.BlockSpec((tm, LANES), pos),
        ],
        out_specs=[pl.BlockSpec((tm, qn), row), pl.BlockSpec((tm, kn), row), pl.BlockSpec((tm, kn), row)],
        out_shape=[
            jax.ShapeDtypeStruct((t, qn), BF16),
            jax.ShapeDtypeStruct((t, kn), BF16),
            jax.ShapeDtypeStruct((t, kn), BF16),
        ],
        compiler_params=_cparams(("parallel",)),
        name="inproj_gqa",
    )(h, mod, w, cos, sin)


def _inproj_kv_kernel(h_ref, mod_ref, w_ref, k_ref, v_ref):
    u = _modulate(h_ref[...], mod_ref[0], 0, 1).astype(BF16)
    kn = C_KV_HEADS * C_HEAD_DIM
    k_ref[...] = _dot(u, w_ref[:, 0:kn]).astype(BF16)
    v_ref[...] = _dot(u, w_ref[:, kn:2 * kn]).astype(BF16)


def _inproj_kv(h, mod, w, rows_per_group, group0):
    t, d = h.shape
    n = w.shape[1]
    tm = TOKEN_TILE
    kn = C_KV_HEADS * C_HEAD_DIM
    row = lambda i: (i, 0)
    return pl.pallas_call(
        _inproj_kv_kernel,
        grid=(t // tm,),
        in_specs=[
            pl.BlockSpec((tm, d), row),
            pl.BlockSpec((1, 6, d), _mod_spec(rows_per_group, group0, tm)),
            pl.BlockSpec((d, n), lambda i: (0, 0)),
        ],
        out_specs=[pl.BlockSpec((tm, kn), row), pl.BlockSpec((tm, kn), row)],
        out_shape=[jax.ShapeDtypeStruct((t, kn), BF16), jax.ShapeDtypeStruct((t, kn), BF16)],
        compiler_params=_cparams(("parallel",)),
        name="inproj_ctx_kv",
    )(h, mod, w)


def _rope_tables(seq):
    t = jnp.arange(seq)
    row = (t // GRID_W).astype(F32)
    col = (t % GRID_W).astype(F32)
    axis_dim = C_HEAD_DIM // 2
    inv_freq = ROPE_BASE ** (-jnp.arange(0, axis_dim, 2, dtype=F32) / axis_dim)
    ang_r = row[:, None] * inv_freq
    ang_c = col[:, None] * inv_freq
    cos = jnp.concatenate([jnp.cos(ang_r)] * 2 + [jnp.cos(ang_c)] * 2, axis=-1)
    sin = jnp.concatenate([-jnp.sin(ang_r), jnp.sin(ang_r), -jnp.sin(ang_c), jnp.sin(ang_c)], axis=-1)
    reps = LANES // C_HEAD_DIM
    return jnp.tile(cos, (1, reps)), jnp.tile(sin, (1, reps))


def _gqa_kernel(sink_ref, q_ref, k_ref, v_ref, kc_ref, vc_ref, o_ref, *, seq):
    blk = pl.program_id(1)
    span = C_BLOCK + 2 * C_WINDOW
    start = pl.multiple_of(jnp.clip(blk * C_BLOCK - C_WINDOW, 0, seq - span), C_BLOCK)
    k_pos = start + lax.broadcasted_iota(jnp.int32, (span, C_BLOCK), 0)
    q_pos = blk * C_BLOCK + lax.broadcasted_iota(jnp.int32, (span, C_BLOCK), 1)
    valid = jnp.abs(q_pos - k_pos) <= C_WINDOW
    valid = jnp.concatenate([valid] * C_GROUP, axis=1)
    d = C_HEAD_DIM

    def scores(hk):
        q0 = hk * C_GROUP * d
        qs = jnp.concatenate([q_ref[0, :, q0 + g * d:q0 + (g + 1) * d] for g in range(C_GROUP)], axis=0)
        ksl = slice(hk * d, (hk + 1) * d)
        return (jnp.where(valid, _dot_nt(k_ref[0, pl.ds(start, span), ksl], qs), NEG_INF),
                _dot_nt(kc_ref[0, :, ksl], qs))

    def softmax(hk, s):
        s_lat, s_ctx = s
        sink = jnp.concatenate(
            [jnp.full((1, C_BLOCK), sink_ref[hk * C_GROUP + g] * LOG2E, F32) for g in range(C_GROUP)], axis=1)
        m = jnp.maximum(jnp.maximum(jnp.max(s_lat, axis=0, keepdims=True), jnp.max(s_ctx, axis=0, keepdims=True)),
                        sink)
        p_lat = jnp.exp2(s_lat - m)
        p_ctx = jnp.exp2(s_ctx - m)
        denom = jnp.sum(p_lat, axis=0, keepdims=True) + jnp.sum(p_ctx, axis=0, keepdims=True) + jnp.exp2(sink - m)
        inv = 1.0 / denom
        return (p_lat * inv).astype(BF16), (p_ctx * inv).astype(BF16)

    def values(hk, p):
        ksl = slice(hk * d, (hk + 1) * d)
        q0 = hk * C_GROUP * d
        out = _dot_tn(p[0], v_ref[0, pl.ds(start, span), ksl]) + _dot_tn(p[1], vc_ref[0, :, ksl])
        for g in range(C_GROUP):
            o_ref[0, :, q0 + g * d:q0 + (g + 1) * d] = out[g * C_BLOCK:(g + 1) * C_BLOCK].astype(BF16)

    s = {0: scores(0), 1: scores(1)}
    p = {0: softmax(0, s[0])}
    for hk in range(C_KV_HEADS):
        if hk + 2 < C_KV_HEADS:
            s[hk + 2] = scores(hk + 2)
        if hk + 1 < C_KV_HEADS:
            p[hk + 1] = softmax(hk + 1, s[hk + 1])
        values(hk, p[hk])


def _window_attention(q, k, v, k_ctx, v_ctx, sink):
    bsz, seq, qn = q.shape
    kn = k.shape[2]
    n_ctx = k_ctx.shape[1]
    full = lambda b, s: (b, 0, 0)
    return pl.pallas_call(
        functools.partial(_gqa_kernel, seq=seq),
        grid=(bsz, seq // C_BLOCK),
        in_specs=[
            pl.BlockSpec(memory_space=pltpu.SMEM),
            pl.BlockSpec((1, C_BLOCK, qn), lambda b, s: (b, s, 0)),
            pl.BlockSpec((1, seq, kn), full),
            pl.BlockSpec((1, seq, kn), full),
            pl.BlockSpec((1, n_ctx, kn), full),
            pl.BlockSpec((1, n_ctx, kn), full),
        ],
        out_specs=pl.BlockSpec((1, C_BLOCK, qn), lambda b, s: (b, s, 0)),
        out_shape=jax.ShapeDtypeStruct((bsz, seq, qn), BF16),
        compiler_params=_cparams(("parallel", "arbitrary")),
        name="window_gqa_attn",
    )(sink, q, k, v, k_ctx, v_ctx)


def _route(logits):
    row = lax.broadcasted_iota(jnp.int32, logits.shape, 0)
    big = jnp.int32(ROUTER_ROWS)
    lg = jnp.where(row < N_GROUPS, logits, NEG_INF)
    g_max = jnp.max(lg, axis=0, keepdims=True)
    g_prob = 1.0 / jnp.sum(jnp.exp(lg - g_max), axis=0, keepdims=True)
    g_idx = jnp.min(jnp.where(lg == g_max, row, big), axis=0, keepdims=True)
    first = GATE_LANE0 + g_idx * EXPERTS_PER_GROUP
    in_group = (row >= first) & (row < first + EXPERTS_PER_GROUP)
    le = jnp.where(in_group, logits, NEG_INF)
    e1 = jnp.max(le, axis=0, keepdims=True)
    i1 = jnp.min(jnp.where(le == e1, row, big), axis=0, keepdims=True)
    le2 = jnp.where(row == i1, NEG_INF, le)
    e2 = jnp.max(le2, axis=0, keepdims=True)
    i2 = jnp.min(jnp.where(le2 == e2, row, big), axis=0, keepdims=True)
    r = jnp.exp(e2 - e1)
    w1 = g_prob / (1.0 + r)
    w2 = w1 * r
    return jnp.where(row == i1, w1, jnp.where(row == i2, w2, 0.0)), g_idx


def _post_mix_kernel(*refs, n_in, alpha):
    a_refs = refs[:n_in]
    w_refs = refs[n_in:2 * n_in]
    b_ref, h_ref, mod_ref, lg_ref, lb_ref, r_ref, h1_ref, t_ref, gate_ref, gid_ref = refs[2 * n_in:]
    o = functools.reduce(jnp.add, [_dot(a[...], w[...]) for a, w in zip(a_refs, w_refs)]) + b_ref[...]
    m = mod_ref[0]
    h1 = _layer_norm(alpha * h_ref[...] + m[2:3] * o, lg_ref[...], lb_ref[...])
    h1_ref[...] = h1
    t = _modulate(h1, m, 3, 4).astype(BF16)
    t_ref[...] = t
    gates, group = _route(_dot_nt(r_ref[...], t))
    pad = jnp.zeros((ROUTER_LANES - ROUTER_ROWS, gates.shape[1]), F32)
    gate_ref[...] = jnp.transpose(jnp.concatenate([gates, pad], axis=0))
    gid_ref[0] = jnp.broadcast_to(group, gid_ref.shape[1:])


def _post_mix(acts, weights, bias, h, mod, ln_g, ln_b, router, rows_per_group, group0, alpha):
    t, d = h.shape
    tm = TOKEN_TILE
    row = lambda i: (i, 0)
    const = lambda i: (0, 0)
    in_specs = [pl.BlockSpec((tm, a.shape[1]), row) for a in acts]
    in_specs += [pl.BlockSpec(w.shape, const) for w in weights]
    in_specs += [
        pl.BlockSpec((1, d), const),
        pl.BlockSpec((tm, d), row),
        pl.BlockSpec((1, 6, d), _mod_spec(rows_per_group, group0, tm)),
        pl.BlockSpec((1, d), const),
        pl.BlockSpec((1, d), const),
        pl.BlockSpec((ROUTER_ROWS, d), const),
    ]
    return pl.pallas_call(
        functools.partial(_post_mix_kernel, n_in=len(acts), alpha=alpha),
        grid=(t // tm,),
        in_specs=in_specs,
        out_specs=[pl.BlockSpec((tm, d), row), pl.BlockSpec((tm, d), row), pl.BlockSpec((tm, ROUTER_LANES), row),
                   pl.BlockSpec((1, SUBLANES, tm), lambda i: (i, 0, 0))],
        out_shape=[
            jax.ShapeDtypeStruct((t, d), F32),
            jax.ShapeDtypeStruct((t, d), BF16),
            jax.ShapeDtypeStruct((t, ROUTER_LANES), F32),
            jax.ShapeDtypeStruct((t // tm, SUBLANES, tm), jnp.int32),
        ],
        compiler_params=_cparams(("parallel",)),
        name="outproj_ln_router",
    )(*acts, *weights, bias.reshape(1, d), h, mod, ln_g.reshape(1, d), ln_b.reshape(1, d), router)


def _split3(a):
    hi = a.astype(BF16)
    rest = a - hi.astype(F32)
    mid = rest.astype(BF16)
    return hi, mid, (rest - mid.astype(F32)).astype(BF16)


def _expert_group(x, gate_cols, wg_ref, wu_ref, wd_ref):
    y = None
    for j, gate in enumerate(gate_cols):
        hid = _silu(_dot(x, wg_ref[j])) * _dot(x, wu_ref[j]) * gate
        part = _dot(hid.astype(BF16), wd_ref[j])
        y = part if y is None else y + part
    return y


def _moe_plan(gid, tm):
    per_tile = gid.reshape(-1, tm)
    counts = jnp.sum(per_tile[:, :, None] == jnp.arange(N_GROUPS, dtype=jnp.int32), axis=1, dtype=jnp.int32)
    chunks = (counts + MOE_CHUNK - 1) // MOE_CHUNK
    ends = jnp.cumsum(chunks, axis=1)
    slot = jnp.arange(MOE_SLOTS, dtype=jnp.int32)
    chunk_group = jnp.minimum(jnp.sum(ends[:, None, :] <= slot[None, :, None], axis=2), N_GROUPS - 1)
    start_row = (ends - chunks) * MOE_CHUNK
    return (chunk_group.astype(jnp.int32).reshape(-1), ends[:, -1].astype(jnp.int32),
            start_row.astype(jnp.int32).reshape(-1))


def _moe_kernel(chunk_group_ref, n_chunks_ref, start_ref, t_ref, gate_ref, gid_ref, h_ref, mod_ref, wg_ref,
                wu_ref, wd_ref, lg_ref, lb_ref, o_ref, perm_ref, xp_ref, gp_ref, yp_ref, *, alpha):
    tile = pl.program_id(0)
    step = pl.program_id(1)
    tm = t_ref.shape[0]

    @pl.when(step == 0)
    def _():
        gid = jnp.concatenate([gid_ref[i, 0:1, :] for i in range(gid_ref.shape[0])], axis=1)
        is_grp = lax.broadcasted_iota(jnp.int32, (SUBLANES, tm), 0) == gid
        upper = (lax.broadcasted_iota(jnp.int32, (tm, tm), 0) <= lax.broadcasted_iota(jnp.int32, (tm, tm), 1))
        count = _dot(is_grp.astype(BF16), upper.astype(BF16))
        rank = jnp.sum(jnp.where(is_grp, count, 0.0), axis=0, keepdims=True).astype(jnp.int32) - 1
        start = jnp.zeros_like(gid)
        for g in range(N_GROUPS):
            start = jnp.where(gid == g, start_ref[tile * N_GROUPS + g], start)
        dest = start + rank
        perm = (lax.broadcasted_iota(jnp.int32, (perm_ref.shape[0], tm), 0) == dest).astype(BF16)
        perm_ref[...] = perm
        xp_ref[...] = _dot(perm, t_ref[...]).astype(BF16)
        gates3 = _dot(perm, jnp.concatenate(_split3(gate_ref[...]), axis=1))
        n = ROUTER_LANES
        gp_ref[...] = gates3[:, 0:n] + gates3[:, n:2 * n] + gates3[:, 2 * n:3 * n]

    @pl.when(step < n_chunks_ref[tile])
    def _():
        rows = pl.ds(pl.multiple_of(step * MOE_CHUNK, MOE_CHUNK), MOE_CHUNK)
        gp = gp_ref[rows, :]
        lane = lax.broadcasted_iota(jnp.int32, gp.shape, 1)
        lane0 = GATE_LANE0 + chunk_group_ref[tile * MOE_SLOTS + step] * EXPERTS_PER_GROUP
        cols = [jnp.sum(jnp.where(lane == lane0 + j, gp, 0.0), axis=1, keepdims=True)
                for j in range(EXPERTS_PER_GROUP)]
        yp_ref[rows, :] = _expert_group(xp_ref[rows, :], cols, wg_ref, wu_ref, wd_ref).astype(BF16)

    @pl.when(step >= n_chunks_ref[tile])
    def _():
        rows = pl.ds(pl.multiple_of(step * MOE_CHUNK, MOE_CHUNK), MOE_CHUNK)
        yp_ref[rows, :] = jnp.zeros((MOE_CHUNK, yp_ref.shape[1]), BF16)

    @pl.when(step == MOE_SLOTS - 1)
    def _():
        y = _dot_tn(perm_ref[...], yp_ref[...])
        o_ref[...] = _layer_norm(alpha * h_ref[...] + mod_ref[0][5:6] * y, lg_ref[...], lb_ref[...])


def _moe_ln(t_act, gates, gid, h, mod, wg, wu, wd, ln_g, ln_b, rows_per_group, group0, alpha):
    t, d = h.shape
    tm = MOE_TILE
    f = wg.shape[2]
    rows = MOE_SLOTS * MOE_CHUNK
    row = lambda i, s, cg, nc, st: (i, 0)
    const = lambda i, s, cg, nc, st: (0, 0)
    group_w = lambda i, s, cg, nc, st: (cg[i * MOE_SLOTS + s], 0, 0)
    mod_idx = _mod_spec(rows_per_group, group0, tm)
    plan = _moe_plan(gid[:, 0, :].reshape(-1), tm)
    grid_spec = pltpu.PrefetchScalarGridSpec(
        num_scalar_prefetch=3,
        grid=(t // tm, MOE_SLOTS),
        in_specs=[
            pl.BlockSpec((tm, d), row),
            pl.BlockSpec((tm, ROUTER_LANES), row),
            pl.BlockSpec((tm // TOKEN_TILE, SUBLANES, TOKEN_TILE), lambda i, s, cg, nc, st: (i, 0, 0)),
            pl.BlockSpec((tm, d), row),
            pl.BlockSpec((1, 6, d), lambda i, s, cg, nc, st: mod_idx(i)),
            pl.BlockSpec((EXPERTS_PER_GROUP, d, f), group_w),
            pl.BlockSpec((EXPERTS_PER_GROUP, d, f), group_w),
            pl.BlockSpec((EXPERTS_PER_GROUP, f, d), group_w),
            pl.BlockSpec((1, d), const),
            pl.BlockSpec((1, d), const),
        ],
        out_specs=pl.BlockSpec((tm, d), row),
        scratch_shapes=[
            pltpu.VMEM((rows, tm), BF16),
            pltpu.VMEM((rows, d), BF16),
            pltpu.VMEM((rows, ROUTER_LANES), F32),
            pltpu.VMEM((rows, d), BF16),
        ],
    )
    return pl.pallas_call(
        functools.partial(_moe_kernel, alpha=alpha),
        grid_spec=grid_spec,
        out_shape=jax.ShapeDtypeStruct((t, d), F32),
        compiler_params=_cparams(("parallel", "arbitrary")),
        name="moe_ln",
    )(*plan, t_act, gates, gid, h, mod, wg, wu, wd, ln_g.reshape(1, d), ln_b.reshape(1, d))


def _router_matrix(router_group, router_expert):
    d = router_group.shape[0]
    pad = jnp.zeros((d, ROUTER_ROWS - N_GROUPS - N_EXPERTS), F32)
    return jnp.concatenate([router_group, router_expert, pad], axis=1).T.astype(BF16)


def kernel(x, c, ctx, c_ctx, ada_w, ada_b, ln_g, ln_b, ab_w_in, ab_b_in, conv_w, conv_b, conv_ln_g, conv_ln_b,
           na_rpb, ab_w_out, ab_b_out, gqa_w_in, gqa_sink, gqa_w_out, router_group, router_expert, exp_w_gate,
           exp_w_up, exp_w_down):
    bsz, seq, d = x.shape
    n_ctx = ctx.shape[1]
    depth = ada_w.shape[0]
    assert depth == DEPTH and bsz + 1 <= MOD_ROWS
    assert seq % MOE_TILE == 0 and (bsz * n_ctx) % MOE_TILE == 0 and MOE_TILE % TOKEN_TILE == 0
    assert seq % C_BLOCK == 0 and seq % CONV_CHUNK == 0 and n_ctx % CONV_CHUNK == 0
    rows = seq // GRID_W
    assert rows % NA_ROWS_PER_STEP == 0 and rows >= NA_UNION_ROWS
    alpha = (2.0 * depth) ** 0.25
    t_lat, t_ctx = bsz * seq, bsz * n_ctx

    cc = jnp.concatenate([c, c_ctx[None], jnp.zeros((MOD_ROWS - bsz - 1, d), F32)], axis=0)
    mod = _modulation(cc, ada_w, ada_b).reshape(depth, MOD_ROWS, 6, d)
    lat_grp = dict(rows_per_group=seq, group0=0)
    ctx_grp = dict(rows_per_group=t_ctx, group0=bsz)

    h_lat = x.reshape(t_lat, d)
    h_ctx = ctx.reshape(t_ctx, d)
    for i in range(depth):
        j = i // 2
        need_ctx = i < depth - 1
        router = _router_matrix(router_group[i], router_expert[i])
        wg, wu, wd = exp_w_gate[i].astype(BF16), exp_w_up[i].astype(BF16), exp_w_down[i].astype(BF16)
        if i % 2 == 0:
            w_in = ab_w_in[j].astype(BF16)
            g_lat, q_lat, k_lat, v_lat = _inproj_ab(h_lat, mod[i], w_in, ab_b_in[j], **lat_grp)
            g_ctx, q_ctx, k_ctx, v_ctx = _inproj_ab(h_ctx, mod[i], w_in, ab_b_in[j], **ctx_grp)
            to_seq = lambda a, n: a.reshape(bsz, n, a.shape[-1])
            conv_args = (conv_w[j], conv_b[j], conv_ln_g[j], conv_ln_b[j])
            conv_lat = _conv_module(to_seq(g_lat, seq), *conv_args).reshape(t_lat, CONV_DIM)
            k_ctx, v_ctx = to_seq(k_ctx, n_ctx), to_seq(v_ctx, n_ctx)
            row_off, types = _na_geometry(rows)
            bias = _na_bias_tables(na_rpb[j], row_off)
            na_lat = _neighbourhood_attention(to_seq(q_lat, seq), to_seq(k_lat, seq), to_seq(v_lat, seq), k_ctx,
                                              v_ctx, bias, types).reshape(t_lat, NA_DIM)
            w_out = ab_w_out[j].astype(BF16)
            w_outs = [w_out[:CONV_DIM], w_out[CONV_DIM:]]
            b_out = ab_b_out[j]
            acts_lat = [conv_lat, na_lat]
            if need_ctx:
                conv_ctx = _conv_module(to_seq(g_ctx, n_ctx), *conv_args).reshape(t_ctx, CONV_DIM)
                na_ctx = _context_attention(to_seq(q_ctx, n_ctx), k_ctx, v_ctx).reshape(t_ctx, NA_DIM)
                acts_ctx = [conv_ctx, na_ctx]
        else:
            assert not need_ctx
            qn = C_HEADS * C_HEAD_DIM
            w_in = gqa_w_in[j].astype(BF16)
            cos, sin = _rope_tables(seq)
            q_lat, k_lat, v_lat = _inproj_gqa(h_lat, mod[i], w_in, cos, sin, seq)
            k_ctx, v_ctx = _inproj_kv(h_ctx, mod[i], w_in[:, qn:], **ctx_grp)
            to_seq = lambda a, n: a.reshape(bsz, n, a.shape[-1])
            att = _window_attention(to_seq(q_lat, seq), to_seq(k_lat, seq), to_seq(v_lat, seq),
                                    to_seq(k_ctx, n_ctx), to_seq(v_ctx, n_ctx), gqa_sink[j])
            acts_lat = [att.reshape(t_lat, qn)]
            w_outs = [gqa_w_out[j].astype(BF16)]
            b_out = jnp.zeros((d,), F32)
        ln1 = (ln_g[i, 0], ln_b[i, 0])
        ln2 = (ln_g[i, 1], ln_b[i, 1])
        h1, t_act, gates, gid = _post_mix(acts_lat, w_outs, b_out, h_lat, mod[i], *ln1, router, alpha=alpha, **lat_grp)
        h_lat = _moe_ln(t_act, gates, gid, h1, mod[i], wg, wu, wd, *ln2, alpha=alpha, **lat_grp)
        if need_ctx:
            h1, t_act, gates, gid = _post_mix(acts_ctx, w_outs, b_out, h_ctx, mod[i], *ln1, router, alpha=alpha,
                                         **ctx_grp)
            h_ctx = _moe_ln(t_act, gates, gid, h1, mod[i], wg, wu, wd, *ln2, alpha=alpha, **ctx_grp)
    return h_lat.reshape(bsz, seq, d)
```

```python
import functools

import numpy as np
import jax
import jax.numpy as jnp
from jax import lax
from jax.experimental import pallas as pl
from jax.experimental.pallas import tpu as pltpu

F32 = jnp.float32
BF16 = jnp.bfloat16

DEPTH = 2
GRID_W = 64
CONV_DIM = 512
CONV_WIDTH = 31
NA_HEADS = 8
NA_HEAD_DIM = 64
NA_DIM = NA_HEADS * NA_HEAD_DIM
NA_KH = 8
NA_KW = 16
C_HEADS = 16
C_KV_HEADS = 4
C_GROUP = C_HEADS // C_KV_HEADS
C_HEAD_DIM = 64
C_WINDOW = 128
C_BLOCK = 128
ROPE_BASE = 10000.0
N_GROUPS = 4
EXPERTS_PER_GROUP = 4
N_EXPERTS = N_GROUPS * EXPERTS_PER_GROUP
D_EXPERT = 256
LN_EPS = 1e-5
NEG_INF = -1e30
LOG2E = 1.4426950408889634

LANES = 128
SUBLANES = 8
VMEM_LIMIT_BYTES = 56 * 1024 * 1024

TOKEN_TILE = 512
MOE_TILE = 1024
MOE_CHUNK = 256
MOE_SLOTS = MOE_TILE // MOE_CHUNK + N_GROUPS - 1
MOE_BASE_CHUNKS = MOE_TILE // MOE_CHUNK + 1
MOD_ROWS = 24
NA_ROWS_PER_STEP = 4
NA_UNION_ROWS = NA_KH + NA_ROWS_PER_STEP
CONV_CHUNK = 256
CONV_HALO = 16
ROUTER_LANES = LANES
GATE_LANE0 = N_GROUPS
ROUTER_ROWS = 24


def _cparams(semantics):
    return pltpu.CompilerParams(dimension_semantics=semantics, vmem_limit_bytes=VMEM_LIMIT_BYTES)


def _dot(a, b):
    return jnp.dot(a, b, preferred_element_type=F32)


def _dot_nt(a, b):
    return lax.dot_general(a, b, (((1,), (1,)), ((), ())), preferred_element_type=F32)


def _dot_tn(a, b):
    return lax.dot_general(a, b, (((0,), (0,)), ((), ())), preferred_element_type=F32)


def _split_bf16(a):
    hi = a.astype(BF16)
    lo = (a - hi.astype(F32)).astype(BF16)
    return hi, lo


def _dot3(a, b):
    a_hi, a_lo = _split_bf16(a)
    b_hi, b_lo = _split_bf16(b)
    return _dot(a_hi, b_hi) + (_dot(a_lo, b_hi) + _dot(a_hi, b_lo))


def _layer_norm(x, g, b):
    mu = jnp.mean(x, axis=-1, keepdims=True)
    xc = x - mu
    var = jnp.mean(xc * xc, axis=-1, keepdims=True)
    return xc * lax.rsqrt(var + LN_EPS) * g + b


def _silu(x):
    return x * jax.nn.sigmoid(x)


def _mod_kernel(cc_ref, w_ref, b_ref, o_ref):
    o_ref[0] = _dot3(_silu(cc_ref[...]), w_ref[0]) + b_ref[0]


def _modulation(cc, ada_w, ada_b):
    depth, d, n = ada_w.shape
    tn = n // 4
    return pl.pallas_call(
        _mod_kernel,
        grid=(depth, n // tn),
        in_specs=[
            pl.BlockSpec((MOD_ROWS, d), lambda i, j: (0, 0)),
            pl.BlockSpec((1, d, tn), lambda i, j: (i, 0, j)),
            pl.BlockSpec((1, 1, tn), lambda i, j: (i, 0, j)),
        ],
        out_specs=pl.BlockSpec((1, MOD_ROWS, tn), lambda i, j: (i, 0, j)),
        out_shape=jax.ShapeDtypeStruct((depth, MOD_ROWS, n), F32),
        compiler_params=_cparams(("arbitrary", "arbitrary")),
        name="adaln_mod",
    )(cc, ada_w, ada_b.reshape(depth, 1, n))


def _mod_spec(rows_per_group, group0, tm):
    return lambda i: (group0 + (i * tm) // rows_per_group, 0, 0)


def _modulate(h, m, shift_row, scale_row):
    return h * (1.0 + m[scale_row:scale_row + 1]) + m[shift_row:shift_row + 1]


def _inproj_ab_kernel(h_ref, mod_ref, w_ref, b_ref, g_ref, q_ref, k_ref, v_ref):
    u = _modulate(h_ref[...], mod_ref[0], 0, 1).astype(BF16)
    c = CONV_DIM
    za = _dot(u, w_ref[:, 0:c]) + b_ref[:, 0:c]
    zb = _dot(u, w_ref[:, c:2 * c]) + b_ref[:, c:2 * c]
    g_ref[...] = za * jax.nn.sigmoid(zb)
    q0 = 2 * c
    zq = _dot(u, w_ref[:, q0:q0 + NA_DIM]) + b_ref[:, q0:q0 + NA_DIM]
    q_ref[...] = (zq * (NA_HEAD_DIM ** -0.5 * LOG2E)).astype(BF16)
    k0 = q0 + NA_DIM
    k_ref[...] = (_dot(u, w_ref[:, k0:k0 + NA_DIM]) + b_ref[:, k0:k0 + NA_DIM]).astype(BF16)
    v0 = k0 + NA_DIM
    v_ref[...] = (_dot(u, w_ref[:, v0:v0 + NA_DIM]) + b_ref[:, v0:v0 + NA_DIM]).astype(BF16)


def _inproj_ab(h, mod, w, b, rows_per_group, group0):
    t, d = h.shape
    n = w.shape[1]
    tm = TOKEN_TILE
    row = lambda i: (i, 0)
    return pl.pallas_call(
        _inproj_ab_kernel,
        grid=(t // tm,),
        in_specs=[
            pl.BlockSpec((tm, d), row),
            pl.BlockSpec((1, 6, d), _mod_spec(rows_per_group, group0, tm)),
            pl.BlockSpec((d, n), lambda i: (0, 0)),
            pl.BlockSpec((1, n), lambda i: (0, 0)),
        ],
        out_specs=[
            pl.BlockSpec((tm, CONV_DIM), row),
            pl.BlockSpec((tm, NA_DIM), row),
            pl.BlockSpec((tm, NA_DIM), row),
            pl.BlockSpec((tm, NA_DIM), row),
        ],
        out_shape=[
            jax.ShapeDtypeStruct((t, CONV_DIM), F32),
            jax.ShapeDtypeStruct((t, NA_DIM), BF16),
            jax.ShapeDtypeStruct((t, NA_DIM), BF16),
            jax.ShapeDtypeStruct((t, NA_DIM), BF16),
        ],
        compiler_params=_cparams(("parallel",)),
        name="inproj_conv_na",
    )(h, mod, w, b.reshape(1, n))


def _conv_kernel(g_ref, w_ref, cb_ref, lg_ref, lb_ref, o_ref, pad_ref, *, seq):
    zeros = jnp.zeros((CONV_HALO, CONV_DIM), F32)
    pad_ref[0:CONV_HALO, :] = zeros
    pad_ref[CONV_HALO + seq:2 * CONV_HALO + seq, :] = zeros
    pad_ref[CONV_HALO:CONV_HALO + seq, :] = g_ref[0]
    first = CONV_HALO - CONV_WIDTH // 2
    ext = CONV_CHUNK + SUBLANES

    def chunk(i, carry):
        r0 = pl.multiple_of(i * CONV_CHUNK, CONV_CHUNK)
        acc = jnp.zeros((CONV_CHUNK, CONV_DIM), F32) + cb_ref[...]
        for res in range(SUBLANES):
            part = None
            for base in range(0, first + CONV_WIDTH, SUBLANES):
                tap = base + res - first
                if 0 <= tap < CONV_WIDTH:
                    term = pad_ref[pl.ds(r0 + base, ext), :] * w_ref[tap:tap + 1, :]
                    part = term if part is None else part + term
            acc = acc + part[res:res + CONV_CHUNK]
        y = _layer_norm(acc, lg_ref[...], lb_ref[...])
        o_ref[0, pl.ds(r0, CONV_CHUNK), :] = _silu(y).astype(BF16)
        return carry

    lax.fori_loop(0, seq // CONV_CHUNK, chunk, 0)


def _conv_module(g, conv_w, conv_b, ln_g, ln_b):
    bsz, seq, c = g.shape
    vec = lambda i: (0, 0)
    return pl.pallas_call(
        functools.partial(_conv_kernel, seq=seq),
        grid=(bsz,),
        in_specs=[
            pl.BlockSpec((1, seq, c), lambda i: (i, 0, 0)),
            pl.BlockSpec((CONV_WIDTH, c), vec),
            pl.BlockSpec((1, c), vec),
            pl.BlockSpec((1, c), vec),
            pl.BlockSpec((1, c), vec),
        ],
        out_specs=pl.BlockSpec((1, seq, c), lambda i: (i, 0, 0)),
        out_shape=jax.ShapeDtypeStruct((bsz, seq, c), BF16),
        scratch_shapes=[pltpu.VMEM((seq + 2 * CONV_HALO, c), F32)],
        compiler_params=_cparams(("parallel",)),
        name="conv_module",
    )(g, conv_w, conv_b.reshape(1, c), ln_g.reshape(1, c), ln_b.reshape(1, c))


N_ROW_OFFS = 2 * NA_KH - 1
N_COL_OFFS = 2 * NA_KW - 1


def _na_geometry(rows):
    rq, ru = NA_ROWS_PER_STEP, NA_UNION_ROWS
    geoms, types = [], []
    for step in range(rows // rq):
        r = step * rq + np.arange(rq)[:, None]
        key_row = np.clip(step * rq - NA_KH // 2, 0, rows - ru) + np.arange(ru)[None, :]
        row_start = np.clip(r - NA_KH // 2, 0, rows - NA_KH)
        row_in = (key_row >= row_start) & (key_row < row_start + NA_KH)
        geom = np.where(row_in, key_row - r + NA_KH - 1, N_ROW_OFFS).astype(np.int32)
        for t, other in enumerate(geoms):
            if np.array_equal(other, geom):
                types.append(t)
                break
        else:
            types.append(len(geoms))
            geoms.append(geom)
    return np.stack(geoms), np.asarray(types, np.int32)


def _na_bias_kernel(row_off_ref, rpb_ref, o_ref, tile_ref):
    typ = pl.program_id(0)
    head = pl.program_id(1)
    qc = lax.broadcasted_iota(jnp.int32, (GRID_W, GRID_W), 0)
    kc = lax.broadcasted_iota(jnp.int32, (GRID_W, GRID_W), 1)
    col_start = jnp.clip(qc - NA_KW // 2, 0, GRID_W - NA_KW)
    col_in = (kc >= col_start) & (kc < col_start + NA_KW)
    col_off = jnp.clip(kc - qc, -(NA_KW - 1), NA_KW - 1) + NA_KW - 1
    for a in range(N_ROW_OFFS):
        tile = jnp.zeros((GRID_W, GRID_W), F32)
        for b in range(N_COL_OFFS):
            tile = jnp.where(col_off == b, rpb_ref[(head * N_ROW_OFFS + a) * N_COL_OFFS + b], tile)
        tile_ref[a] = jnp.where(col_in, tile * LOG2E, NEG_INF)
    tile_ref[N_ROW_OFFS] = jnp.full((GRID_W, GRID_W), NEG_INF, F32)
    for qr in range(NA_ROWS_PER_STEP):
        for kr in range(NA_UNION_ROWS):
            a = row_off_ref[(typ * NA_ROWS_PER_STEP + qr) * NA_UNION_ROWS + kr]
            o_ref[0, 0, qr * GRID_W:(qr + 1) * GRID_W, kr * GRID_W:(kr + 1) * GRID_W] = tile_ref[a]


def _na_bias_tables(rpb, row_off):
    n_types = row_off.shape[0]
    tq = NA_ROWS_PER_STEP * GRID_W
    n_win = NA_UNION_ROWS * GRID_W
    return pl.pallas_call(
        _na_bias_kernel,
        grid=(n_types, NA_HEADS),
        in_specs=[pl.BlockSpec(memory_space=pltpu.SMEM), pl.BlockSpec(memory_space=pltpu.SMEM)],
        out_specs=pl.BlockSpec((1, 1, tq, n_win), lambda t, h: (t, h, 0, 0)),
        out_shape=jax.ShapeDtypeStruct((n_types, NA_HEADS, tq, n_win), F32),
        scratch_shapes=[pltpu.VMEM((N_ROW_OFFS + 1, GRID_W, GRID_W), F32)],
        compiler_params=_cparams(("parallel", "parallel")),
        name="na_bias_table",
    )(jnp.asarray(row_off.reshape(-1)), rpb.astype(F32).reshape(-1))


def _softmax_rows(scores):
    m = functools.reduce(jnp.maximum, [jnp.max(s, axis=1, keepdims=True) for s in scores])
    ps = [jnp.exp2(s - m) for s in scores]
    denom = functools.reduce(jnp.add, [jnp.sum(p, axis=1, keepdims=True) for p in ps])
    return [p.astype(BF16) for p in ps], denom


def _pv_rows(ps, denom, values):
    return functools.reduce(jnp.add, [_dot(p, v) for p, v in zip(ps, values)]) / denom


def _na_kernel(type_ref, q_ref, k_ref, v_ref, kc_ref, vc_ref, bias_ref, o_ref, *, rows):
    del type_ref
    step = pl.program_id(1)
    start_row = jnp.clip(step * NA_ROWS_PER_STEP - NA_KH // 2, 0, rows - NA_UNION_ROWS)
    start = pl.multiple_of(start_row * GRID_W, GRID_W)
    n_win = NA_UNION_ROWS * GRID_W
    heads = [slice(h * NA_HEAD_DIM, (h + 1) * NA_HEAD_DIM) for h in range(NA_HEADS)]
    scores = []
    for h, sl in enumerate(heads):
        qh = q_ref[0, :, sl]
        scores.append([_dot_nt(qh, k_ref[0, pl.ds(start, n_win), sl]) + bias_ref[0, h],
                       _dot_nt(qh, kc_ref[0, :, sl])])
    probs = [_softmax_rows(s) for s in scores]
    for sl, (ps, denom) in zip(heads, probs):
        out = _pv_rows(ps, denom, [v_ref[0, pl.ds(start, n_win), sl], vc_ref[0, :, sl]])
        o_ref[0, :, sl] = out.astype(BF16)


def _neighbourhood_attention(q, k, v, k_ctx, v_ctx, bias, types):
    bsz, seq, dim = q.shape
    n_ctx = k_ctx.shape[1]
    rows = seq // GRID_W
    tq = NA_ROWS_PER_STEP * GRID_W
    n_win = NA_UNION_ROWS * GRID_W
    full = lambda b, s, t: (b, 0, 0)
    grid_spec = pltpu.PrefetchScalarGridSpec(
        num_scalar_prefetch=1,
        grid=(bsz, seq // tq),
        in_specs=[
            pl.BlockSpec((1, tq, dim), lambda b, s, t: (b, s, 0)),
            pl.BlockSpec((1, seq, dim), full),
            pl.BlockSpec((1, seq, dim), full),
            pl.BlockSpec((1, n_ctx, dim), full),
            pl.BlockSpec((1, n_ctx, dim), full),
            pl.BlockSpec((1, NA_HEADS, tq, n_win), lambda b, s, t: (t[s], 0, 0, 0)),
        ],
        out_specs=pl.BlockSpec((1, tq, dim), lambda b, s, t: (b, s, 0)),
    )
    return pl.pallas_call(
        functools.partial(_na_kernel, rows=rows),
        grid_spec=grid_spec,
        out_shape=jax.ShapeDtypeStruct((bsz, seq, dim), BF16),
        compiler_params=_cparams(("parallel", "arbitrary")),
        name="neighbourhood_attn",
    )(jnp.asarray(types), q, k, v, k_ctx, v_ctx, bias)


def _ctx_attn_kernel(q_ref, k_ref, v_ref, o_ref):
    for h in range(NA_HEADS):
        sl = slice(h * NA_HEAD_DIM, (h + 1) * NA_HEAD_DIM)
        ps, denom = _softmax_rows([_dot_nt(q_ref[0, :, sl], k_ref[0, :, sl])])
        o_ref[0, :, sl] = _pv_rows(ps, denom, [v_ref[0, :, sl]]).astype(BF16)


def _context_attention(q, k, v):
    bsz, n, dim = q.shape
    spec = pl.BlockSpec((1, n, dim), lambda b: (b, 0, 0))
    return pl.pallas_call(
        _ctx_attn_kernel,
        grid=(bsz,),
        in_specs=[spec, spec, spec],
        out_specs=spec,
        out_shape=jax.ShapeDtypeStruct((bsz, n, dim), BF16),
        compiler_params=_cparams(("parallel",)),
        name="context_attn",
    )(q, k, v)


def _rope(x, cos, sin_signed):
    n = x.shape[1]
    half = C_HEAD_DIM // 4
    lane = lax.broadcasted_iota(jnp.int32, x.shape, 1)
    partner = jnp.where(lane % (2 * half) < half, pltpu.roll(x, n - half, 1), pltpu.roll(x, half, 1))
    reps = n // LANES
    cos_full = jnp.concatenate([cos] * reps, axis=1)
    sin_full = jnp.concatenate([sin_signed] * reps, axis=1)
    return x * cos_full + partner * sin_full


def _inproj_gqa_kernel(h_ref, mod_ref, w_ref, cos_ref, sin_ref, q_ref, k_ref, v_ref):
    u = _modulate(h_ref[...], mod_ref[0], 0, 1).astype(BF16)
    qn = C_HEADS * C_HEAD_DIM
    kn = C_KV_HEADS * C_HEAD_DIM
    cos, sin = cos_ref[...], sin_ref[...]
    q = _rope(_dot(u, w_ref[:, 0:qn]), cos, sin)
    q_ref[...] = (q * (C_HEAD_DIM ** -0.5 * LOG2E)).astype(BF16)
    k_ref[...] = _rope(_dot(u, w_ref[:, qn:qn + kn]), cos, sin).astype(BF16)
    v_ref[...] = _dot(u, w_ref[:, qn + kn:qn + 2 * kn]).astype(BF16)


def _inproj_gqa(h, mod, w, cos, sin, seq):
    t, d = h.shape
    n = w.shape[1]
    tm = TOKEN_TILE
    qn = C_HEADS * C_HEAD_DIM
    kn = C_KV_HEADS * C_HEAD_DIM
    row = lambda i: (i, 0)
    pos = lambda i: (i % (seq // tm), 0)
    return pl.pallas_call(
        _inproj_gqa_kernel,
        grid=(t // tm,),
        in_specs=[
            pl.BlockSpec((tm, d), row),
            pl.BlockSpec((1, 6, d), _mod_spec(seq, 0, tm)),
            pl.BlockSpec((d, n), lambda i: (0, 0)),
            pl.BlockSpec((tm, LANES), pos),
            pl.BlockSpec((tm, LANES), pos),
        ],
        out_specs=[pl.BlockSpec((tm, qn), row), pl.BlockSpec((tm, kn), row), pl.BlockSpec((tm, kn), row)],
        out_shape=[
            jax.ShapeDtypeStruct((t, qn), BF16),
            jax.ShapeDtypeStruct((t, kn), BF16),
            jax.ShapeDtypeStruct((t, kn), BF16),
        ],
        compiler_params=_cparams(("parallel",)),
        name="inproj_gqa",
    )(h, mod, w, cos, sin)


def _inproj_kv_kernel(h_ref, mod_ref, w_ref, k_ref, v_ref):
    u = _modulate(h_ref[...], mod_ref[0], 0, 1).astype(BF16)
    kn = C_KV_HEADS * C_HEAD_DIM
    k_ref[...] = _dot(u, w_ref[:, 0:kn]).astype(BF16)
    v_ref[...] = _dot(u, w_ref[:, kn:2 * kn]).astype(BF16)


def _inproj_kv(h, mod, w, rows_per_group, group0):
    t, d = h.shape
    n = w.shape[1]
    tm = TOKEN_TILE
    kn = C_KV_HEADS * C_HEAD_DIM
    row = lambda i: (i, 0)
    return pl.pallas_call(
        _inproj_kv_kernel,
        grid=(t // tm,),
        in_specs=[
            pl.BlockSpec((tm, d), row),
            pl.BlockSpec((1, 6, d), _mod_spec(rows_per_group, group0, tm)),
            pl.BlockSpec((d, n), lambda i: (0, 0)),
        ],
        out_specs=[pl.BlockSpec((tm, kn), row), pl.BlockSpec((tm, kn), row)],
        out_shape=[jax.ShapeDtypeStruct((t, kn), BF16), jax.ShapeDtypeStruct((t, kn), BF16)],
        compiler_params=_cparams(("parallel",)),
        name="inproj_ctx_kv",
    )(h, mod, w)


def _rope_tables(seq):
    t = jnp.arange(seq)
    row = (t // GRID_W).astype(F32)
    col = (t % GRID_W).astype(F32)
    axis_dim = C_HEAD_DIM // 2
    inv_freq = ROPE_BASE ** (-jnp.arange(0, axis_dim, 2, dtype=F32) / axis_dim)
    ang_r = row[:, None] * inv_freq
    ang_c = col[:, None] * inv_freq
    cos = jnp.concatenate([jnp.cos(ang_r)] * 2 + [jnp.cos(ang_c)] * 2, axis=-1)
    sin = jnp.concatenate([-jnp.sin(ang_r), jnp.sin(ang_r), -jnp.sin(ang_c), jnp.sin(ang_c)], axis=-1)
    reps = LANES // C_HEAD_DIM
    return jnp.tile(cos, (1, reps)), jnp.tile(sin, (1, reps))


def _gqa_kernel(sink_ref, q_ref, k_ref, v_ref, kc_ref, vc_ref, o_ref, *, seq):
    blk = pl.program_id(1)
    span = C_BLOCK + 2 * C_WINDOW
    start = pl.multiple_of(jnp.clip(blk * C_BLOCK - C_WINDOW, 0, seq - span), C_BLOCK)
    k_pos = start + lax.broadcasted_iota(jnp.int32, (span, C_BLOCK), 0)
    q_pos = blk * C_BLOCK + lax.broadcasted_iota(jnp.int32, (span, C_BLOCK), 1)
    valid = jnp.abs(q_pos - k_pos) <= C_WINDOW
    valid = jnp.concatenate([valid] * C_GROUP, axis=1)
    d = C_HEAD_DIM

    def scores(hk):
        q0 = hk * C_GROUP * d
        qs = jnp.concatenate([q_ref[0, :, q0 + g * d:q0 + (g + 1) * d] for g in range(C_GROUP)], axis=0)
        ksl = slice(hk * d, (hk + 1) * d)
        return (jnp.where(valid, _dot_nt(k_ref[0, pl.ds(start, span), ksl], qs), NEG_INF),
                _dot_nt(kc_ref[0, :, ksl], qs))

    def softmax(hk, s):
        s_lat, s_ctx = s
        sink = jnp.concatenate(
            [jnp.full((1, C_BLOCK), sink_ref[hk * C_GROUP + g] * LOG2E, F32) for g in range(C_GROUP)], axis=1)
        m = jnp.maximum(jnp.maximum(jnp.max(s_lat, axis=0, keepdims=True), jnp.max(s_ctx, axis=0, keepdims=True)),
                        sink)
        p_lat = jnp.exp2(s_lat - m)
        p_ctx = jnp.exp2(s_ctx - m)
        denom = jnp.sum(p_lat, axis=0, keepdims=True) + jnp.sum(p_ctx, axis=0, keepdims=True) + jnp.exp2(sink - m)
        inv = 1.0 / denom
        return (p_lat * inv).astype(BF16), (p_ctx * inv).astype(BF16)

    def values(hk, p):
        ksl = slice(hk * d, (hk + 1) * d)
        q0 = hk * C_GROUP * d
        out = _dot_tn(p[0], v_ref[0, pl.ds(start, span), ksl]) + _dot_tn(p[1], vc_ref[0, :, ksl])
        for g in range(C_GROUP):
            o_ref[0, :, q0 + g * d:q0 + (g + 1) * d] = out[g * C_BLOCK:(g + 1) * C_BLOCK].astype(BF16)

    s = {0: scores(0), 1: scores(1)}
    p = {0: softmax(0, s[0])}
    for hk in range(C_KV_HEADS):
        if hk + 2 < C_KV_HEADS:
            s[hk + 2] = scores(hk + 2)
        if hk + 1 < C_KV_HEADS:
            p[hk + 1] = softmax(hk + 1, s[hk + 1])
        values(hk, p[hk])


def _window_attention(q, k, v, k_ctx, v_ctx, sink):
    bsz, seq, qn = q.shape
    kn = k.shape[2]
    n_ctx = k_ctx.shape[1]
    full = lambda b, s: (b, 0, 0)
    return pl.pallas_call(
        functools.partial(_gqa_kernel, seq=seq),
        grid=(bsz, seq // C_BLOCK),
        in_specs=[
            pl.BlockSpec(memory_space=pltpu.SMEM),
            pl.BlockSpec((1, C_BLOCK, qn), lambda b, s: (b, s, 0)),
            pl.BlockSpec((1, seq, kn), full),
            pl.BlockSpec((1, seq, kn), full),
            pl.BlockSpec((1, n_ctx, kn), full),
            pl.BlockSpec((1, n_ctx, kn), full),
        ],
        out_specs=pl.BlockSpec((1, C_BLOCK, qn), lambda b, s: (b, s, 0)),
        out_shape=jax.ShapeDtypeStruct((bsz, seq, qn), BF16),
        compiler_params=_cparams(("parallel", "arbitrary")),
        name="window_gqa_attn",
    )(sink, q, k, v, k_ctx, v_ctx)


def _route(logits):
    row = lax.broadcasted_iota(jnp.int32, logits.shape, 0)
    big = jnp.int32(ROUTER_ROWS)
    lg = jnp.where(row < N_GROUPS, logits, NEG_INF)
    g_max = jnp.max(lg, axis=0, keepdims=True)
    g_prob = 1.0 / jnp.sum(jnp.exp(lg - g_max), axis=0, keepdims=True)
    g_idx = jnp.min(jnp.where(lg == g_max, row, big), axis=0, keepdims=True)
    first = GATE_LANE0 + g_idx * EXPERTS_PER_GROUP
    in_group = (row >= first) & (row < first + EXPERTS_PER_GROUP)
    le = jnp.where(in_group, logits, NEG_INF)
    e1 = jnp.max(le, axis=0, keepdims=True)
    i1 = jnp.min(jnp.where(le == e1, row, big), axis=0, keepdims=True)
    le2 = jnp.where(row == i1, NEG_INF, le)
    e2 = jnp.max(le2, axis=0, keepdims=True)
    i2 = jnp.min(jnp.where(le2 == e2, row, big), axis=0, keepdims=True)
    r = jnp.exp(e2 - e1)
    w1 = g_prob / (1.0 + r)
    w2 = w1 * r
    own = [jnp.where(i1 == first + j, w1, jnp.where(i2 == first + j, w2, 0.0)) for j in range(EXPERTS_PER_GROUP)]
    return g_idx, own


def _post_mix_kernel(*refs, n_in, alpha):
    a_refs = refs[:n_in]
    w_refs = refs[n_in:2 * n_in]
    b_ref, h_ref, mod_ref, lg_ref, lb_ref, r_ref, h1_ref, t_ref, gate_ref, gid_ref = refs[2 * n_in:]
    o = functools.reduce(jnp.add, [_dot(a[...], w[...]) for a, w in zip(a_refs, w_refs)]) + b_ref[...]
    m = mod_ref[0]
    h1 = _layer_norm(alpha * h_ref[...] + m[2:3] * o, lg_ref[...], lb_ref[...])
    h1_ref[...] = h1
    t = _modulate(h1, m, 3, 4).astype(BF16)
    t_ref[...] = t
    group, own = _route(_dot_nt(r_ref[...], t))
    parts = _split3(jnp.concatenate(own, axis=0))
    rows = jnp.concatenate([p.astype(F32) for p in parts], axis=0)
    pad = jnp.zeros((ROUTER_LANES - rows.shape[0], rows.shape[1]), F32)
    gate_ref[...] = jnp.transpose(jnp.concatenate([rows, pad], axis=0)).astype(BF16)
    gid_ref[0] = jnp.broadcast_to(group, gid_ref.shape[1:])


def _post_mix(acts, weights, bias, h, mod, ln_g, ln_b, router, rows_per_group, group0, alpha):
    t, d = h.shape
    tm = TOKEN_TILE
    row = lambda i: (i, 0)
    const = lambda i: (0, 0)
    in_specs = [pl.BlockSpec((tm, a.shape[1]), row) for a in acts]
    in_specs += [pl.BlockSpec(w.shape, const) for w in weights]
    in_specs += [
        pl.BlockSpec((1, d), const),
        pl.BlockSpec((tm, d), row),
        pl.BlockSpec((1, 6, d), _mod_spec(rows_per_group, group0, tm)),
        pl.BlockSpec((1, d), const),
        pl.BlockSpec((1, d), const),
        pl.BlockSpec((ROUTER_ROWS, d), const),
    ]
    return pl.pallas_call(
        functools.partial(_post_mix_kernel, n_in=len(acts), alpha=alpha),
        grid=(t // tm,),
        in_specs=in_specs,
        out_specs=[pl.BlockSpec((tm, d), row), pl.BlockSpec((tm, d), row), pl.BlockSpec((tm, ROUTER_LANES), row),
                   pl.BlockSpec((1, SUBLANES, tm), lambda i: (i, 0, 0))],
        out_shape=[
            jax.ShapeDtypeStruct((t, d), F32),
            jax.ShapeDtypeStruct((t, d), BF16),
            jax.ShapeDtypeStruct((t, ROUTER_LANES), BF16),
            jax.ShapeDtypeStruct((t // tm, SUBLANES, tm), jnp.int32),
        ],
        compiler_params=_cparams(("parallel",)),
        name="outproj_ln_router",
    )(*acts, *weights, bias.reshape(1, d), h, mod, ln_g.reshape(1, d), ln_b.reshape(1, d), router)


def _split3(a):
    hi = a.astype(BF16)
    rest = a - hi.astype(F32)
    mid = rest.astype(BF16)
    return hi, mid, (rest - mid.astype(F32)).astype(BF16)


def _expert_group(x, gate_cols, wg_ref, wu_ref, wd_ref):
    y = None
    for j, gate in enumerate(gate_cols):
        hid = _silu(_dot(x, wg_ref[j])) * _dot(x, wu_ref[j]) * gate
        part = _dot(hid.astype(BF16), wd_ref[j])
        y = part if y is None else y + part
    return y


def _moe_plan(gid, tm):
    per_tile = gid.reshape(-1, tm)
    counts = jnp.sum(per_tile[:, :, None] == jnp.arange(N_GROUPS, dtype=jnp.int32), axis=1, dtype=jnp.int32)
    chunks = (counts + MOE_CHUNK - 1) // MOE_CHUNK
    ends = jnp.cumsum(chunks, axis=1)
    slot = jnp.arange(MOE_SLOTS, dtype=jnp.int32)
    chunk_group = jnp.minimum(jnp.sum(ends[:, None, :] <= slot[None, :, None], axis=2), N_GROUPS - 1)
    start_row = (ends - chunks) * MOE_CHUNK
    return (chunk_group.astype(jnp.int32).reshape(-1), ends[:, -1].astype(jnp.int32),
            start_row.astype(jnp.int32).reshape(-1))


def _moe_kernel(chunk_group_ref, n_chunks_ref, start_ref, t_ref, gate_ref, gid_ref, h_ref, mod_ref, wg_ref,
                wu_ref, wd_ref, lg_ref, lb_ref, o_ref, perm_ref, xp_ref, gp_ref, yp_ref, *, alpha):
    tile = pl.program_id(0)
    step = pl.program_id(1)
    tm = t_ref.shape[0]

    @pl.when(step == 0)
    def _():
        gid = jnp.concatenate([gid_ref[i, 0:1, :] for i in range(gid_ref.shape[0])], axis=1)
        is_grp = lax.broadcasted_iota(jnp.int32, (SUBLANES, tm), 0) == gid
        upper = (lax.broadcasted_iota(jnp.int32, (tm, tm), 0) <= lax.broadcasted_iota(jnp.int32, (tm, tm), 1))
        count = _dot(is_grp.astype(BF16), upper.astype(BF16))
        rank = jnp.sum(jnp.where(is_grp, count, 0.0), axis=0, keepdims=True).astype(jnp.int32) - 1
        start = jnp.zeros_like(gid)
        for g in range(N_GROUPS):
            start = jnp.where(gid == g, start_ref[tile * N_GROUPS + g], start)
        dest = start + rank
        perm_ref[...] = (lax.broadcasted_iota(jnp.int32, (perm_ref.shape[0], tm), 0) == dest).astype(BF16)

        def move(rows):
            perm = perm_ref[rows, :]
            xp_ref[rows, :] = _dot(perm, t_ref[...]).astype(BF16)
            gp_ref[rows, :] = _dot(perm, gate_ref[...])

        move(slice(0, MOE_BASE_CHUNKS * MOE_CHUNK))
        for s in range(MOE_BASE_CHUNKS, MOE_SLOTS):
            pl.when(s < n_chunks_ref[tile])(functools.partial(move, slice(s * MOE_CHUNK, (s + 1) * MOE_CHUNK)))

    @pl.when(step < n_chunks_ref[tile])
    def _():
        rows = pl.ds(pl.multiple_of(step * MOE_CHUNK, MOE_CHUNK), MOE_CHUNK)
        gp = gp_ref[rows, :]
        e = EXPERTS_PER_GROUP
        cols = [gp[:, j:j + 1] + gp[:, e + j:e + j + 1] + gp[:, 2 * e + j:2 * e + j + 1] for j in range(e)]
        yp_ref[rows, :] = _expert_group(xp_ref[rows, :], cols, wg_ref, wu_ref, wd_ref).astype(BF16)

    @pl.when((step >= n_chunks_ref[tile]) & (step < MOE_BASE_CHUNKS))
    def _():
        rows = pl.ds(pl.multiple_of(step * MOE_CHUNK, MOE_CHUNK), MOE_CHUNK)
        yp_ref[rows, :] = jnp.zeros((MOE_CHUNK, yp_ref.shape[1]), BF16)

    @pl.when(step == MOE_SLOTS - 1)
    def _():
        base = slice(0, MOE_BASE_CHUNKS * MOE_CHUNK)
        o_ref[...] = _dot_tn(perm_ref[base, :], yp_ref[base, :])
        for s in range(MOE_BASE_CHUNKS, MOE_SLOTS):
            @pl.when(s < n_chunks_ref[tile])
            def _():
                rows = slice(s * MOE_CHUNK, (s + 1) * MOE_CHUNK)
                o_ref[...] += _dot_tn(perm_ref[rows, :], yp_ref[rows, :])
        o_ref[...] = _layer_norm(alpha * h_ref[...] + mod_ref[0][5:6] * o_ref[...], lg_ref[...], lb_ref[...])


def _moe_ln(t_act, gates, gid, h, mod, wg, wu, wd, ln_g, ln_b, rows_per_group, group0, alpha):
    t, d = h.shape
    tm = MOE_TILE
    f = wg.shape[2]
    rows = MOE_SLOTS * MOE_CHUNK
    row = lambda i, s, cg, nc, st: (i, 0)
    const = lambda i, s, cg, nc, st: (0, 0)
    group_w = lambda i, s, cg, nc, st: (cg[i * MOE_SLOTS + s], 0, 0)
    mod_idx = _mod_spec(rows_per_group, group0, tm)
    plan = _moe_plan(gid[:, 0, :].reshape(-1), tm)
    grid_spec = pltpu.PrefetchScalarGridSpec(
        num_scalar_prefetch=3,
        grid=(t // tm, MOE_SLOTS),
        in_specs=[
            pl.BlockSpec((tm, d), row),
            pl.BlockSpec((tm, ROUTER_LANES), row),
            pl.BlockSpec((tm // TOKEN_TILE, SUBLANES, TOKEN_TILE), lambda i, s, cg, nc, st: (i, 0, 0)),
            pl.BlockSpec((tm, d), row),
            pl.BlockSpec((1, 6, d), lambda i, s, cg, nc, st: mod_idx(i)),
            pl.BlockSpec((EXPERTS_PER_GROUP, d, f), group_w),
            pl.BlockSpec((EXPERTS_PER_GROUP, d, f), group_w),
            pl.BlockSpec((EXPERTS_PER_GROUP, f, d), group_w),
            pl.BlockSpec((1, d), const),
            pl.BlockSpec((1, d), const),
        ],
        out_specs=pl.BlockSpec((tm, d), row),
        scratch_shapes=[
            pltpu.VMEM((rows, tm), BF16),
            pltpu.VMEM((rows, d), BF16),
            pltpu.VMEM((rows, ROUTER_LANES), F32),
            pltpu.VMEM((rows, d), BF16),
        ],
    )
    return pl.pallas_call(
        functools.partial(_moe_kernel, alpha=alpha),
        grid_spec=grid_spec,
        out_shape=jax.ShapeDtypeStruct((t, d), F32),
        compiler_params=_cparams(("parallel", "arbitrary")),
        name="moe_ln",
    )(*plan, t_act, gates, gid, h, mod, wg, wu, wd, ln_g.reshape(1, d), ln_b.reshape(1, d))


def _router_matrix(router_group, router_expert):
    d = router_group.shape[0]
    pad = jnp.zeros((d, ROUTER_ROWS - N_GROUPS - N_EXPERTS), F32)
    return jnp.concatenate([router_group, router_expert, pad], axis=1).T.astype(BF16)


def kernel(x, c, ctx, c_ctx, ada_w, ada_b, ln_g, ln_b, ab_w_in, ab_b_in, conv_w, conv_b, conv_ln_g, conv_ln_b,
           na_rpb, ab_w_out, ab_b_out, gqa_w_in, gqa_sink, gqa_w_out, router_group, router_expert, exp_w_gate,
           exp_w_up, exp_w_down):
    bsz, seq, d = x.shape
    n_ctx = ctx.shape[1]
    depth = ada_w.shape[0]
    assert depth == DEPTH and bsz + 1 <= MOD_ROWS
    assert seq % MOE_TILE == 0 and (bsz * n_ctx) % MOE_TILE == 0 and MOE_TILE % TOKEN_TILE == 0
    assert seq % C_BLOCK == 0 and seq % CONV_CHUNK == 0 and n_ctx % CONV_CHUNK == 0
    rows = seq // GRID_W
    assert rows % NA_ROWS_PER_STEP == 0 and rows >= NA_UNION_ROWS
    alpha = (2.0 * depth) ** 0.25
    t_lat, t_ctx = bsz * seq, bsz * n_ctx

    cc = jnp.concatenate([c, c_ctx[None], jnp.zeros((MOD_ROWS - bsz - 1, d), F32)], axis=0)
    mod = _modulation(cc, ada_w, ada_b).reshape(depth, MOD_ROWS, 6, d)
    lat_grp = dict(rows_per_group=seq, group0=0)
    ctx_grp = dict(rows_per_group=t_ctx, group0=bsz)

    h_lat = x.reshape(t_lat, d)
    h_ctx = ctx.reshape(t_ctx, d)
    for i in range(depth):
        j = i // 2
        need_ctx = i < depth - 1
        router = _router_matrix(router_group[i], router_expert[i])
        wg, wu, wd = exp_w_gate[i].astype(BF16), exp_w_up[i].astype(BF16), exp_w_down[i].astype(BF16)
        if i % 2 == 0:
            w_in = ab_w_in[j].astype(BF16)
            g_lat, q_lat, k_lat, v_lat = _inproj_ab(h_lat, mod[i], w_in, ab_b_in[j], **lat_grp)
            g_ctx, q_ctx, k_ctx, v_ctx = _inproj_ab(h_ctx, mod[i], w_in, ab_b_in[j], **ctx_grp)
            to_seq = lambda a, n: a.reshape(bsz, n, a.shape[-1])
            conv_args = (conv_w[j], conv_b[j], conv_ln_g[j], conv_ln_b[j])
            conv_lat = _conv_module(to_seq(g_lat, seq), *conv_args).reshape(t_lat, CONV_DIM)
            k_ctx, v_ctx = to_seq(k_ctx, n_ctx), to_seq(v_ctx, n_ctx)
            row_off, types = _na_geometry(rows)
            bias = _na_bias_tables(na_rpb[j], row_off)
            na_lat = _neighbourhood_attention(to_seq(q_lat, seq), to_seq(k_lat, seq), to_seq(v_lat, seq), k_ctx,
                                              v_ctx, bias, types).reshape(t_lat, NA_DIM)
            w_out = ab_w_out[j].astype(BF16)
            w_outs = [w_out[:CONV_DIM], w_out[CONV_DIM:]]
            b_out = ab_b_out[j]
            acts_lat = [conv_lat, na_lat]
            if need_ctx:
                conv_ctx = _conv_module(to_seq(g_ctx, n_ctx), *conv_args).reshape(t_ctx, CONV_DIM)
                na_ctx = _context_attention(to_seq(q_ctx, n_ctx), k_ctx, v_ctx).reshape(t_ctx, NA_DIM)
                acts_ctx = [conv_ctx, na_ctx]
        else:
            assert not need_ctx
            qn = C_HEADS * C_HEAD_DIM
            w_in = gqa_w_in[j].astype(BF16)
            cos, sin = _rope_tables(seq)
            q_lat, k_lat, v_lat = _inproj_gqa(h_lat, mod[i], w_in, cos, sin, seq)
            k_ctx, v_ctx = _inproj_kv(h_ctx, mod[i], w_in[:, qn:], **ctx_grp)
            to_seq = lambda a, n: a.reshape(bsz, n, a.shape[-1])
            att = _window_attention(to_seq(q_lat, seq), to_seq(k_lat, seq), to_seq(v_lat, seq),
                                    to_seq(k_ctx, n_ctx), to_seq(v_ctx, n_ctx), gqa_sink[j])
            acts_lat = [att.reshape(t_lat, qn)]
            w_outs = [gqa_w_out[j].astype(BF16)]
            b_out = jnp.zeros((d,), F32)
        ln1 = (ln_g[i, 0], ln_b[i, 0])
        ln2 = (ln_g[i, 1], ln_b[i, 1])
        h1, t_act, gates, gid = _post_mix(acts_lat, w_outs, b_out, h_lat, mod[i], *ln1, router, alpha=alpha, **lat_grp)
        h_lat = _moe_ln(t_act, gates, gid, h1, mod[i], wg, wu, wd, *ln2, alpha=alpha, **lat_grp)
        if need_ctx:
            h1, t_act, gates, gid = _post_mix(acts_ctx, w_outs, b_out, h_ctx, mod[i], *ln1, router, alpha=alpha,
                                         **ctx_grp)
            h_ctx = _moe_ln(t_act, gates, gid, h1, mod[i], wg, wu, wd, *ln2, alpha=alpha, **ctx_grp)
    return h_lat.reshape(bsz, seq, d)
```

```python
import functools

import numpy as np
import jax
import jax.numpy as jnp
from jax import lax
from jax.experimental import pallas as pl
from jax.experimental.pallas import tpu as pltpu

F32 = jnp.float32
BF16 = jnp.bfloat16

DEPTH = 2
GRID_W = 64
CONV_DIM = 512
CONV_WIDTH = 31
NA_HEADS = 8
NA_HEAD_DIM = 64
NA_DIM = NA_HEADS * NA_HEAD_DIM
NA_KH = 8
NA_KW = 16
C_HEADS = 16
C_KV_HEADS = 4
C_GROUP = C_HEADS // C_KV_HEADS
C_HEAD_DIM = 64
C_WINDOW = 128
C_BLOCK = 128
ROPE_BASE = 10000.0
N_GROUPS = 4
EXPERTS_PER_GROUP = 4
N_EXPERTS = N_GROUPS * EXPERTS_PER_GROUP
D_EXPERT = 256
LN_EPS = 1e-5
NEG_INF = -1e30
LOG2E = 1.4426950408889634

LANES = 128
SUBLANES = 8
VMEM_LIMIT_BYTES = 56 * 1024 * 1024

TOKEN_TILE = 512
MOE_TILE = 1024
MOE_CHUNK = 256
MOE_SLOTS = MOE_TILE // MOE_CHUNK + N_GROUPS - 1
MOE_BASE_CHUNKS = MOE_TILE // MOE_CHUNK + 1
MOD_ROWS = 24
NA_ROWS_PER_STEP = 4
NA_UNION_ROWS = NA_KH + NA_ROWS_PER_STEP
CONV_CHUNK = 256
CONV_HALO = 16
ROUTER_LANES = LANES
GATE_LANE0 = N_GROUPS
ROUTER_ROWS = 24


def _cparams(semantics):
    return pltpu.CompilerParams(dimension_semantics=semantics, vmem_limit_bytes=VMEM_LIMIT_BYTES)


def _dot(a, b):
    return jnp.dot(a, b, preferred_element_type=F32)


def _dot_nt(a, b):
    return lax.dot_general(a, b, (((1,), (1,)), ((), ())), preferred_element_type=F32)


def _dot_tn(a, b):
    return lax.dot_general(a, b, (((0,), (0,)), ((), ())), preferred_element_type=F32)


def _split_bf16(a):
    hi = a.astype(BF16)
    lo = (a - hi.astype(F32)).astype(BF16)
    return hi, lo


def _dot3(a, b):
    a_hi, a_lo = _split_bf16(a)
    b_hi, b_lo = _split_bf16(b)
    return _dot(a_hi, b_hi) + (_dot(a_lo, b_hi) + _dot(a_hi, b_lo))


def _layer_norm(x, g, b):
    mu = jnp.mean(x, axis=-1, keepdims=True)
    xc = x - mu
    var = jnp.mean(xc * xc, axis=-1, keepdims=True)
    return xc * lax.rsqrt(var + LN_EPS) * g + b


def _silu(x):
    return x * jax.nn.sigmoid(x)


def _mod_kernel(cc_ref, w_ref, b_ref, o_ref):
    o_ref[0] = _dot3(_silu(cc_ref[...]), w_ref[0]) + b_ref[0]


def _modulation(cc, ada_w, ada_b):
    depth, d, n = ada_w.shape
    tn = n // 4
    return pl.pallas_call(
        _mod_kernel,
        grid=(depth, n // tn),
        in_specs=[
            pl.BlockSpec((MOD_ROWS, d), lambda i, j: (0, 0)),
            pl.BlockSpec((1, d, tn), lambda i, j: (i, 0, j)),
            pl.BlockSpec((1, 1, tn), lambda i, j: (i, 0, j)),
        ],
        out_specs=pl.BlockSpec((1, MOD_ROWS, tn), lambda i, j: (i, 0, j)),
        out_shape=jax.ShapeDtypeStruct((depth, MOD_ROWS, n), F32),
        compiler_params=_cparams(("arbitrary", "arbitrary")),
        name="adaln_mod",
    )(cc, ada_w, ada_b.reshape(depth, 1, n))


def _mod_spec(rows_per_group, group0, tm):
    return lambda i: (group0 + (i * tm) // rows_per_group, 0, 0)


def _modulate(h, m, shift_row, scale_row):
    return h * (1.0 + m[scale_row:scale_row + 1]) + m[shift_row:shift_row + 1]


def _inproj_ab_kernel(h_ref, mod_ref, w_ref, b_ref, g_ref, q_ref, k_ref, v_ref):
    u = _modulate(h_ref[...], mod_ref[0], 0, 1).astype(BF16)
    c = CONV_DIM
    za = _dot(u, w_ref[:, 0:c]) + b_ref[:, 0:c]
    zb = _dot(u, w_ref[:, c:2 * c]) + b_ref[:, c:2 * c]
    g_ref[...] = za * jax.nn.sigmoid(zb)
    q0 = 2 * c
    zq = _dot(u, w_ref[:, q0:q0 + NA_DIM]) + b_ref[:, q0:q0 + NA_DIM]
    q_ref[...] = (zq * (NA_HEAD_DIM ** -0.5 * LOG2E)).astype(BF16)
    k0 = q0 + NA_DIM
    k_ref[...] = (_dot(u, w_ref[:, k0:k0 + NA_DIM]) + b_ref[:, k0:k0 + NA_DIM]).astype(BF16)
    v0 = k0 + NA_DIM
    v_ref[...] = (_dot(u, w_ref[:, v0:v0 + NA_DIM]) + b_ref[:, v0:v0 + NA_DIM]).astype(BF16)


def _inproj_ab(h, mod, w, b, rows_per_group, group0):
    t, d = h.shape
    n = w.shape[1]
    tm = TOKEN_TILE
    row = lambda i: (i, 0)
    return pl.pallas_call(
        _inproj_ab_kernel,
        grid=(t // tm,),
        in_specs=[
            pl.BlockSpec((tm, d), row),
            pl.BlockSpec((1, 6, d), _mod_spec(rows_per_group, group0, tm)),
            pl.BlockSpec((d, n), lambda i: (0, 0)),
            pl.BlockSpec((1, n), lambda i: (0, 0)),
        ],
        out_specs=[
            pl.BlockSpec((tm, CONV_DIM), row),
            pl.BlockSpec((tm, NA_DIM), row),
            pl.BlockSpec((tm, NA_DIM), row),
            pl.BlockSpec((tm, NA_DIM), row),
        ],
        out_shape=[
            jax.ShapeDtypeStruct((t, CONV_DIM), F32),
            jax.ShapeDtypeStruct((t, NA_DIM), BF16),
            jax.ShapeDtypeStruct((t, NA_DIM), BF16),
            jax.ShapeDtypeStruct((t, NA_DIM), BF16),
        ],
        compiler_params=_cparams(("parallel",)),
        name="inproj_conv_na",
    )(h, mod, w, b.reshape(1, n))


def _conv_kernel(g_ref, w_ref, cb_ref, lg_ref, lb_ref, o_ref, pad_ref, *, seq):
    zeros = jnp.zeros((CONV_HALO, CONV_DIM), F32)
    pad_ref[0:CONV_HALO, :] = zeros
    pad_ref[CONV_HALO + seq:2 * CONV_HALO + seq, :] = zeros
    pad_ref[CONV_HALO:CONV_HALO + seq, :] = g_ref[0]
    first = CONV_HALO - CONV_WIDTH // 2
    ext = CONV_CHUNK + SUBLANES

    def chunk(i, carry):
        r0 = pl.multiple_of(i * CONV_CHUNK, CONV_CHUNK)
        acc = jnp.zeros((CONV_CHUNK, CONV_DIM), F32) + cb_ref[...]
        for res in range(SUBLANES):
            part = None
            for base in range(0, first + CONV_WIDTH, SUBLANES):
                tap = base + res - first
                if 0 <= tap < CONV_WIDTH:
                    term = pad_ref[pl.ds(r0 + base, ext), :] * w_ref[tap:tap + 1, :]
                    part = term if part is None else part + term
            acc = acc + part[res:res + CONV_CHUNK]
        y = _layer_norm(acc, lg_ref[...], lb_ref[...])
        o_ref[0, pl.ds(r0, CONV_CHUNK), :] = _silu(y).astype(BF16)
        return carry

    lax.fori_loop(0, seq // CONV_CHUNK, chunk, 0)


def _conv_module(g, conv_w, conv_b, ln_g, ln_b):
    bsz, seq, c = g.shape
    vec = lambda i: (0, 0)
    return pl.pallas_call(
        functools.partial(_conv_kernel, seq=seq),
        grid=(bsz,),
        in_specs=[
            pl.BlockSpec((1, seq, c), lambda i: (i, 0, 0)),
            pl.BlockSpec((CONV_WIDTH, c), vec),
            pl.BlockSpec((1, c), vec),
            pl.BlockSpec((1, c), vec),
            pl.BlockSpec((1, c), vec),
        ],
        out_specs=pl.BlockSpec((1, seq, c), lambda i: (i, 0, 0)),
        out_shape=jax.ShapeDtypeStruct((bsz, seq, c), BF16),
        scratch_shapes=[pltpu.VMEM((seq + 2 * CONV_HALO, c), F32)],
        compiler_params=_cparams(("parallel",)),
        name="conv_module",
    )(g, conv_w, conv_b.reshape(1, c), ln_g.reshape(1, c), ln_b.reshape(1, c))


N_ROW_OFFS = 2 * NA_KH - 1
N_COL_OFFS = 2 * NA_KW - 1


def _na_geometry(rows):
    rq, ru = NA_ROWS_PER_STEP, NA_UNION_ROWS
    geoms, types = [], []
    for step in range(rows // rq):
        r = step * rq + np.arange(rq)[:, None]
        key_row = np.clip(step * rq - NA_KH // 2, 0, rows - ru) + np.arange(ru)[None, :]
        row_start = np.clip(r - NA_KH // 2, 0, rows - NA_KH)
        row_in = (key_row >= row_start) & (key_row < row_start + NA_KH)
        geom = np.where(row_in, key_row - r + NA_KH - 1, N_ROW_OFFS).astype(np.int32)
        for t, other in enumerate(geoms):
            if np.array_equal(other, geom):
                types.append(t)
                break
        else:
            types.append(len(geoms))
            geoms.append(geom)
    return np.stack(geoms), np.asarray(types, np.int32)


def _na_bias_kernel(row_off_ref, rpb_ref, o_ref, tile_ref):
    head = pl.program_id(0)
    qc = lax.broadcasted_iota(jnp.int32, (GRID_W, GRID_W), 0)
    kc = lax.broadcasted_iota(jnp.int32, (GRID_W, GRID_W), 1)
    col_start = jnp.clip(qc - NA_KW // 2, 0, GRID_W - NA_KW)
    col_in = (kc >= col_start) & (kc < col_start + NA_KW)
    col_off = jnp.clip(kc - qc, -(NA_KW - 1), NA_KW - 1) + NA_KW - 1
    for a in range(N_ROW_OFFS):
        tile = jnp.zeros((GRID_W, GRID_W), F32)
        for b in range(N_COL_OFFS):
            tile = jnp.where(col_off == b, rpb_ref[(head * N_ROW_OFFS + a) * N_COL_OFFS + b], tile)
        tile_ref[a] = jnp.where(col_in, tile * LOG2E, NEG_INF)
    tile_ref[N_ROW_OFFS] = jnp.full((GRID_W, GRID_W), NEG_INF, F32)
    for typ in range(o_ref.shape[0]):
        for qr in range(NA_ROWS_PER_STEP):
            for kr in range(NA_UNION_ROWS):
                a = row_off_ref[(typ * NA_ROWS_PER_STEP + qr) * NA_UNION_ROWS + kr]
                o_ref[typ, 0, qr * GRID_W:(qr + 1) * GRID_W, kr * GRID_W:(kr + 1) * GRID_W] = tile_ref[a]


def _na_bias_tables(rpb, row_off):
    n_types = row_off.shape[0]
    tq = NA_ROWS_PER_STEP * GRID_W
    n_win = NA_UNION_ROWS * GRID_W
    return pl.pallas_call(
        _na_bias_kernel,
        grid=(NA_HEADS,),
        in_specs=[pl.BlockSpec(memory_space=pltpu.SMEM), pl.BlockSpec(memory_space=pltpu.SMEM)],
        out_specs=pl.BlockSpec((n_types, 1, tq, n_win), lambda h: (0, h, 0, 0)),
        out_shape=jax.ShapeDtypeStruct((n_types, NA_HEADS, tq, n_win), F32),
        scratch_shapes=[pltpu.VMEM((N_ROW_OFFS + 1, GRID_W, GRID_W), F32)],
        compiler_params=_cparams(("parallel",)),
        name="na_bias_table",
    )(jnp.asarray(row_off.reshape(-1)), rpb.astype(F32).reshape(-1))


def _softmax_rows(scores):
    m = functools.reduce(jnp.maximum, [jnp.max(s, axis=1, keepdims=True) for s in scores])
    ps = [jnp.exp2(s - m) for s in scores]
    denom = functools.reduce(jnp.add, [jnp.sum(p, axis=1, keepdims=True) for p in ps])
    return [p.astype(BF16) for p in ps], denom


def _pv_rows(ps, denom, values):
    return functools.reduce(jnp.add, [_dot(p, v) for p, v in zip(ps, values)]) / denom


def _na_kernel(type_ref, q_ref, k_ref, v_ref, kc_ref, vc_ref, bias_ref, o_ref, *, rows):
    del type_ref
    step = pl.program_id(1)
    start_row = jnp.clip(step * NA_ROWS_PER_STEP - NA_KH // 2, 0, rows - NA_UNION_ROWS)
    start = pl.multiple_of(start_row * GRID_W, GRID_W)
    n_win = NA_UNION_ROWS * GRID_W
    heads = [slice(h * NA_HEAD_DIM, (h + 1) * NA_HEAD_DIM) for h in range(NA_HEADS)]
    scores = []
    for h, sl in enumerate(heads):
        qh = q_ref[0, :, sl]
        scores.append([_dot_nt(qh, k_ref[0, pl.ds(start, n_win), sl]) + bias_ref[0, h],
                       _dot_nt(qh, kc_ref[0, :, sl])])
    probs = [_softmax_rows(s) for s in scores]
    for sl, (ps, denom) in zip(heads, probs):
        out = _pv_rows(ps, denom, [v_ref[0, pl.ds(start, n_win), sl], vc_ref[0, :, sl]])
        o_ref[0, :, sl] = out.astype(BF16)


def _neighbourhood_attention(q, k, v, k_ctx, v_ctx, bias, types):
    bsz, seq, dim = q.shape
    n_ctx = k_ctx.shape[1]
    rows = seq // GRID_W
    tq = NA_ROWS_PER_STEP * GRID_W
    n_win = NA_UNION_ROWS * GRID_W
    full = lambda b, s, t: (b, 0, 0)
    grid_spec = pltpu.PrefetchScalarGridSpec(
        num_scalar_prefetch=1,
        grid=(bsz, seq // tq),
        in_specs=[
            pl.BlockSpec((1, tq, dim), lambda b, s, t: (b, s, 0)),
            pl.BlockSpec((1, seq, dim), full),
            pl.BlockSpec((1, seq, dim), full),
            pl.BlockSpec((1, n_ctx, dim), full),
            pl.BlockSpec((1, n_ctx, dim), full),
            pl.BlockSpec((1, NA_HEADS, tq, n_win), lambda b, s, t: (t[s], 0, 0, 0)),
        ],
        out_specs=pl.BlockSpec((1, tq, dim), lambda b, s, t: (b, s, 0)),
    )
    return pl.pallas_call(
        functools.partial(_na_kernel, rows=rows),
        grid_spec=grid_spec,
        out_shape=jax.ShapeDtypeStruct((bsz, seq, dim), BF16),
        compiler_params=_cparams(("parallel", "arbitrary")),
        name="neighbourhood_attn",
    )(jnp.asarray(types), q, k, v, k_ctx, v_ctx, bias)


def _ctx_attn_kernel(q_ref, k_ref, v_ref, o_ref):
    for h in range(NA_HEADS):
        sl = slice(h * NA_HEAD_DIM, (h + 1) * NA_HEAD_DIM)
        ps, denom = _softmax_rows([_dot_nt(q_ref[0, :, sl], k_ref[0, :, sl])])
        o_ref[0, :, sl] = _pv_rows(ps, denom, [v_ref[0, :, sl]]).astype(BF16)


def _context_attention(q, k, v):
    bsz, n, dim = q.shape
    spec = pl.BlockSpec((1, n, dim), lambda b: (b, 0, 0))
    return pl.pallas_call(
        _ctx_attn_kernel,
        grid=(bsz,),
        in_specs=[spec, spec, spec],
        out_specs=spec,
        out_shape=jax.ShapeDtypeStruct((bsz, n, dim), BF16),
        compiler_params=_cparams(("parallel",)),
        name="context_attn",
    )(q, k, v)


def _rope(x, cos, sin_signed):
    n = x.shape[1]
    half = C_HEAD_DIM // 4
    lane = lax.broadcasted_iota(jnp.int32, x.shape, 1)
    partner = jnp.where(lane % (2 * half) < half, pltpu.roll(x, n - half, 1), pltpu.roll(x, half, 1))
    reps = n // LANES
    cos_full = jnp.concatenate([cos] * reps, axis=1)
    sin_full = jnp.concatenate([sin_signed] * reps, axis=1)
    return x * cos_full + partner * sin_full


def _inproj_gqa_kernel(h_ref, mod_ref, w_ref, cos_ref, sin_ref, q_ref, k_ref, v_ref):
    u = _modulate(h_ref[...], mod_ref[0], 0, 1).astype(BF16)
    qn = C_HEADS * C_HEAD_DIM
    kn = C_KV_HEADS * C_HEAD_DIM
    cos, sin = cos_ref[...], sin_ref[...]
    q = _rope(_dot(u, w_ref[:, 0:qn]), cos, sin)
    q_ref[...] = (q * (C_HEAD_DIM ** -0.5 * LOG2E)).astype(BF16)
    k_ref[...] = _rope(_dot(u, w_ref[:, qn:qn + kn]), cos, sin).astype(BF16)
    v_ref[...] = _dot(u, w_ref[:, qn + kn:qn + 2 * kn]).astype(BF16)


def _inproj_gqa(h, mod, w, cos, sin, seq):
    t, d = h.shape
    n = w.shape[1]
    tm = TOKEN_TILE
    qn = C_HEADS * C_HEAD_DIM
    kn = C_KV_HEADS * C_HEAD_DIM
    row = lambda i: (i, 0)
    pos = lambda i: (i % (seq // tm), 0)
    return pl.pallas_call(
        _inproj_gqa_kernel,
        grid=(t // tm,),
        in_specs=[
            pl.BlockSpec((tm, d), row),
            pl.BlockSpec((1, 6, d), _mod_spec(seq, 0, tm)),
            pl.BlockSpec((d, n), lambda i: (0, 0)),
            pl.BlockSpec((tm, LANES), pos),
            pl.BlockSpec((tm, LANES), pos),
        ],
        out_specs=[pl.BlockSpec((tm, qn), row), pl.BlockSpec((tm, kn), row), pl.BlockSpec((tm, kn), row)],
        out_shape=[
            jax.ShapeDtypeStruct((t, qn), BF16),
            jax.ShapeDtypeStruct((t, kn), BF16),
            jax.ShapeDtypeStruct((t, kn), BF16),
        ],
        compiler_params=_cparams(("parallel",)),
        name="inproj_gqa",
    )(h, mod, w, cos, sin)


def _inproj_kv_kernel(h_ref, mod_ref, w_ref, k_ref, v_ref):
    u = _modulate(h_ref[...], mod_ref[0], 0, 1).astype(BF16)
    kn = C_KV_HEADS * C_HEAD_DIM
    k_ref[...] = _dot(u, w_ref[:, 0:kn]).astype(BF16)
    v_ref[...] = _dot(u, w_ref[:, kn:2 * kn]).astype(BF16)


def _inproj_kv(h, mod, w, rows_per_group, group0):
    t, d = h.shape
    n = w.shape[1]
    tm = TOKEN_TILE
    kn = C_KV_HEADS * C_HEAD_DIM
    row = lambda i: (i, 0)
    return pl.pallas_call(
        _inproj_kv_kernel,
        grid=(t // tm,),
        in_specs=[
            pl.BlockSpec((tm, d), row),
            pl.BlockSpec((1, 6, d), _mod_spec(rows_per_group, group0, tm)),
            pl.BlockSpec((d, n), lambda i: (0, 0)),
        ],
        out_specs=[pl.BlockSpec((tm, kn), row), pl.BlockSpec((tm, kn), row)],
        out_shape=[jax.ShapeDtypeStruct((t, kn), BF16), jax.ShapeDtypeStruct((t, kn), BF16)],
        compiler_params=_cparams(("parallel",)),
        name="inproj_ctx_kv",
    )(h, mod, w)


def _rope_tables(seq):
    t = jnp.arange(seq)
    row = (t // GRID_W).astype(F32)
    col = (t % GRID_W).astype(F32)
    axis_dim = C_HEAD_DIM // 2
    inv_freq = ROPE_BASE ** (-jnp.arange(0, axis_dim, 2, dtype=F32) / axis_dim)
    ang_r = row[:, None] * inv_freq
    ang_c = col[:, None] * inv_freq
    cos = jnp.concatenate([jnp.cos(ang_r)] * 2 + [jnp.cos(ang_c)] * 2, axis=-1)
    sin = jnp.concatenate([-jnp.sin(ang_r), jnp.sin(ang_r), -jnp.sin(ang_c), jnp.sin(ang_c)], axis=-1)
    reps = LANES // C_HEAD_DIM
    return jnp.tile(cos, (1, reps)), jnp.tile(sin, (1, reps))


def _gqa_kernel(sink_ref, q_ref, k_ref, v_ref, kc_ref, vc_ref, o_ref, *, seq):
    blk = pl.program_id(1)
    span = C_BLOCK + 2 * C_WINDOW
    start = pl.multiple_of(jnp.clip(blk * C_BLOCK - C_WINDOW, 0, seq - span), C_BLOCK)
    k_pos = start + lax.broadcasted_iota(jnp.int32, (span, C_BLOCK), 0)
    q_pos = blk * C_BLOCK + lax.broadcasted_iota(jnp.int32, (span, C_BLOCK), 1)
    valid = jnp.abs(q_pos - k_pos) <= C_WINDOW
    valid = jnp.concatenate([valid] * C_GROUP, axis=1)
    d = C_HEAD_DIM

    def scores(hk):
        q0 = hk * C_GROUP * d
        qs = jnp.concatenate([q_ref[0, :, q0 + g * d:q0 + (g + 1) * d] for g in range(C_GROUP)], axis=0)
        ksl = slice(hk * d, (hk + 1) * d)
        return (jnp.where(valid, _dot_nt(k_ref[0, pl.ds(start, span), ksl], qs), NEG_INF),
                _dot_nt(kc_ref[0, :, ksl], qs))

    def softmax(hk, s):
        s_lat, s_ctx = s
        sink = jnp.concatenate(
            [jnp.full((1, C_BLOCK), sink_ref[hk * C_GROUP + g] * LOG2E, F32) for g in range(C_GROUP)], axis=1)
        m = jnp.maximum(jnp.maximum(jnp.max(s_lat, axis=0, keepdims=True), jnp.max(s_ctx, axis=0, keepdims=True)),
                        sink)
        p_lat = jnp.exp2(s_lat - m)
        p_ctx = jnp.exp2(s_ctx - m)
        denom = jnp.sum(p_lat, axis=0, keepdims=True) + jnp.sum(p_ctx, axis=0, keepdims=True) + jnp.exp2(sink - m)
        inv = 1.0 / denom
        return (p_lat * inv).astype(BF16), (p_ctx * inv).astype(BF16)

    def values(hk, p):
        ksl = slice(hk * d, (hk + 1) * d)
        q0 = hk * C_GROUP * d
        out = _dot_tn(p[0], v_ref[0, pl.ds(start, span), ksl]) + _dot_tn(p[1], vc_ref[0, :, ksl])
        for g in range(C_GROUP):
            o_ref[0, :, q0 + g * d:q0 + (g + 1) * d] = out[g * C_BLOCK:(g + 1) * C_BLOCK].astype(BF16)

    s = {0: scores(0), 1: scores(1)}
    p = {0: softmax(0, s[0])}
    for hk in range(C_KV_HEADS):
        if hk + 2 < C_KV_HEADS:
            s[hk + 2] = scores(hk + 2)
        if hk + 1 < C_KV_HEADS:
            p[hk + 1] = softmax(hk + 1, s[hk + 1])
        values(hk, p[hk])


def _window_attention(q, k, v, k_ctx, v_ctx, sink):
    bsz, seq, qn = q.shape
    kn = k.shape[2]
    n_ctx = k_ctx.shape[1]
    full = lambda b, s: (b, 0, 0)
    return pl.pallas_call(
        functools.partial(_gqa_kernel, seq=seq),
        grid=(bsz, seq // C_BLOCK),
        in_specs=[
            pl.BlockSpec(memory_space=pltpu.SMEM),
            pl.BlockSpec((1, C_BLOCK, qn), lambda b, s: (b, s, 0)),
            pl.BlockSpec((1, seq, kn), full),
            pl.BlockSpec((1, seq, kn), full),
            pl.BlockSpec((1, n_ctx, kn), full),
            pl.BlockSpec((1, n_ctx, kn), full),
        ],
        out_specs=pl.BlockSpec((1, C_BLOCK, qn), lambda b, s: (b, s, 0)),
        out_shape=jax.ShapeDtypeStruct((bsz, seq, qn), BF16),
        compiler_params=_cparams(("parallel", "arbitrary")),
        name="window_gqa_attn",
    )(sink, q, k, v, k_ctx, v_ctx)


def _route(logits):
    row = lax.broadcasted_iota(jnp.int32, logits.shape, 0)
    big = jnp.int32(ROUTER_ROWS)
    lg = jnp.where(row < N_GROUPS, logits, NEG_INF)
    g_max = jnp.max(lg, axis=0, keepdims=True)
    g_prob = 1.0 / jnp.sum(jnp.exp(lg - g_max), axis=0, keepdims=True)
    g_idx = jnp.min(jnp.where(lg == g_max, row, big), axis=0, keepdims=True)
    first = GATE_LANE0 + g_idx * EXPERTS_PER_GROUP
    in_group = (row >= first) & (row < first + EXPERTS_PER_GROUP)
    le = jnp.where(in_group, logits, NEG_INF)
    e1 = jnp.max(le, axis=0, keepdims=True)
    i1 = jnp.min(jnp.where(le == e1, row, big), axis=0, keepdims=True)
    le2 = jnp.where(row == i1, NEG_INF, le)
    e2 = jnp.max(le2, axis=0, keepdims=True)
    i2 = jnp.min(jnp.where(le2 == e2, row, big), axis=0, keepdims=True)
    r = jnp.exp(e2 - e1)
    w1 = g_prob / (1.0 + r)
    w2 = w1 * r
    own = [jnp.where(i1 == first + j, w1, jnp.where(i2 == first + j, w2, 0.0)) for j in range(EXPERTS_PER_GROUP)]
    return g_idx, own


def _post_mix_kernel(*refs, n_in, alpha):
    a_refs = refs[:n_in]
    w_refs = refs[n_in:2 * n_in]
    b_ref, h_ref, mod_ref, lg_ref, lb_ref, r_ref, h1_ref, t_ref, gate_ref, gid_ref = refs[2 * n_in:]
    o = functools.reduce(jnp.add, [_dot(a[...], w[...]) for a, w in zip(a_refs, w_refs)]) + b_ref[...]
    m = mod_ref[0]
    h1 = _layer_norm(alpha * h_ref[...] + m[2:3] * o, lg_ref[...], lb_ref[...])
    h1_ref[...] = h1
    t = _modulate(h1, m, 3, 4).astype(BF16)
    t_ref[...] = t
    group, own = _route(_dot_nt(r_ref[...], t))
    parts = _split3(jnp.concatenate(own, axis=0))
    rows = jnp.concatenate([p.astype(F32) for p in parts], axis=0)
    pad = jnp.zeros((ROUTER_LANES - rows.shape[0], rows.shape[1]), F32)
    gate_ref[...] = jnp.transpose(jnp.concatenate([rows, pad], axis=0)).astype(BF16)
    gid_ref[0] = jnp.broadcast_to(group, gid_ref.shape[1:])


def _post_mix(acts, weights, bias, h, mod, ln_g, ln_b, router, rows_per_group, group0, alpha):
    t, d = h.shape
    tm = TOKEN_TILE
    row = lambda i: (i, 0)
    const = lambda i: (0, 0)
    in_specs = [pl.BlockSpec((tm, a.shape[1]), row) for a in acts]
    in_specs += [pl.BlockSpec(w.shape, const) for w in weights]
    in_specs += [
        pl.BlockSpec((1, d), const),
        pl.BlockSpec((tm, d), row),
        pl.BlockSpec((1, 6, d), _mod_spec(rows_per_group, group0, tm)),
        pl.BlockSpec((1, d), const),
        pl.BlockSpec((1, d), const),
        pl.BlockSpec((ROUTER_ROWS, d), const),
    ]
    return pl.pallas_call(
        functools.partial(_post_mix_kernel, n_in=len(acts), alpha=alpha),
        grid=(t // tm,),
        in_specs=in_specs,
        out_specs=[pl.BlockSpec((tm, d), row), pl.BlockSpec((tm, d), row), pl.BlockSpec((tm, ROUTER_LANES), row),
                   pl.BlockSpec((1, SUBLANES, tm), lambda i: (i, 0, 0))],
        out_shape=[
            jax.ShapeDtypeStruct((t, d), F32),
            jax.ShapeDtypeStruct((t, d), BF16),
            jax.ShapeDtypeStruct((t, ROUTER_LANES), BF16),
            jax.ShapeDtypeStruct((t // tm, SUBLANES, tm), jnp.int32),
        ],
        compiler_params=_cparams(("parallel",)),
        name="outproj_ln_router",
    )(*acts, *weights, bias.reshape(1, d), h, mod, ln_g.reshape(1, d), ln_b.reshape(1, d), router)


def _split3(a):
    hi = a.astype(BF16)
    rest = a - hi.astype(F32)
    mid = rest.astype(BF16)
    return hi, mid, (rest - mid.astype(F32)).astype(BF16)


def _expert_group(x, gate_cols, wgu_ref, wd_ref, hid_ref):
    f = D_EXPERT
    pre = _dot(x, wgu_ref[0])
    for j, gate in enumerate(gate_cols):
        a = pre[:, 2 * f * j:2 * f * j + f]
        b = pre[:, 2 * f * j + f:2 * f * (j + 1)]
        hid_ref[:, f * j:f * (j + 1)] = (_silu(a) * b * gate).astype(BF16)
    return _dot(hid_ref[...], wd_ref[0])


def _expert_weights(w_gate, w_up, w_down):
    e, d, f = w_gate.shape
    gu = jnp.concatenate([w_gate.astype(BF16), w_up.astype(BF16)], axis=2)
    gu = gu.reshape(N_GROUPS, EXPERTS_PER_GROUP, d, 2 * f).transpose(0, 2, 1, 3)
    return (gu.reshape(N_GROUPS, d, EXPERTS_PER_GROUP * 2 * f),
            w_down.astype(BF16).reshape(N_GROUPS, EXPERTS_PER_GROUP * f, d))


def _moe_plan(gid, tm):
    per_tile = gid.reshape(-1, tm)
    counts = jnp.sum(per_tile[:, :, None] == jnp.arange(N_GROUPS, dtype=jnp.int32), axis=1, dtype=jnp.int32)
    chunks = (counts + MOE_CHUNK - 1) // MOE_CHUNK
    ends = jnp.cumsum(chunks, axis=1)
    slot = jnp.arange(MOE_SLOTS, dtype=jnp.int32)
    chunk_group = jnp.minimum(jnp.sum(ends[:, None, :] <= slot[None, :, None], axis=2), N_GROUPS - 1)
    start_row = (ends - chunks) * MOE_CHUNK
    return (chunk_group.astype(jnp.int32).reshape(-1), ends[:, -1].astype(jnp.int32),
            start_row.astype(jnp.int32).reshape(-1))


def _moe_kernel(chunk_group_ref, n_chunks_ref, start_ref, t_ref, gate_ref, gid_ref, h_ref, mod_ref, wgu_ref,
                wd_ref, lg_ref, lb_ref, o_ref, perm_ref, xp_ref, gp_ref, yp_ref, hid_ref, *, alpha):
    tile = pl.program_id(0)
    step = pl.program_id(1)
    tm = t_ref.shape[0]

    @pl.when(step == 0)
    def _():
        gid = jnp.concatenate([gid_ref[i, 0:1, :] for i in range(gid_ref.shape[0])], axis=1)
        is_grp = lax.broadcasted_iota(jnp.int32, (SUBLANES, tm), 0) == gid
        upper = (lax.broadcasted_iota(jnp.int32, (tm, tm), 0) <= lax.broadcasted_iota(jnp.int32, (tm, tm), 1))
        count = _dot(is_grp.astype(BF16), upper.astype(BF16))
        rank = jnp.sum(jnp.where(is_grp, count, 0.0), axis=0, keepdims=True).astype(jnp.int32) - 1
        start = jnp.zeros_like(gid)
        for g in range(N_GROUPS):
            start = jnp.where(gid == g, start_ref[tile * N_GROUPS + g], start)
        dest = start + rank
        perm_ref[...] = (lax.broadcasted_iota(jnp.int32, (perm_ref.shape[0], tm), 0) == dest).astype(BF16)

        def move(rows):
            perm = perm_ref[rows, :]
            xp_ref[rows, :] = _dot(perm, t_ref[...]).astype(BF16)
            gp_ref[rows, :] = _dot(perm, gate_ref[...])

        move(slice(0, MOE_BASE_CHUNKS * MOE_CHUNK))
        for s in range(MOE_BASE_CHUNKS, MOE_SLOTS):
            pl.when(s < n_chunks_ref[tile])(functools.partial(move, slice(s * MOE_CHUNK, (s + 1) * MOE_CHUNK)))

    @pl.when(step < n_chunks_ref[tile])
    def _():
        rows = pl.ds(pl.multiple_of(step * MOE_CHUNK, MOE_CHUNK), MOE_CHUNK)
        gp = gp_ref[rows, :]
        e = EXPERTS_PER_GROUP
        cols = [gp[:, j:j + 1] + gp[:, e + j:e + j + 1] + gp[:, 2 * e + j:2 * e + j + 1] for j in range(e)]
        yp_ref[rows, :] = _expert_group(xp_ref[rows, :], cols, wgu_ref, wd_ref, hid_ref).astype(BF16)

    @pl.when((step >= n_chunks_ref[tile]) & (step < MOE_BASE_CHUNKS))
    def _():
        rows = pl.ds(pl.multiple_of(step * MOE_CHUNK, MOE_CHUNK), MOE_CHUNK)
        yp_ref[rows, :] = jnp.zeros((MOE_CHUNK, yp_ref.shape[1]), BF16)

    @pl.when(step == MOE_SLOTS - 1)
    def _():
        base = slice(0, MOE_BASE_CHUNKS * MOE_CHUNK)
        o_ref[...] = _dot_tn(perm_ref[base, :], yp_ref[base, :])
        for s in range(MOE_BASE_CHUNKS, MOE_SLOTS):
            @pl.when(s < n_chunks_ref[tile])
            def _():
                rows = slice(s * MOE_CHUNK, (s + 1) * MOE_CHUNK)
                o_ref[...] += _dot_tn(perm_ref[rows, :], yp_ref[rows, :])
        o_ref[...] = _layer_norm(alpha * h_ref[...] + mod_ref[0][5:6] * o_ref[...], lg_ref[...], lb_ref[...])


def _moe_ln(t_act, gates, gid, h, mod, wgu, wd, ln_g, ln_b, rows_per_group, group0, alpha):
    t, d = h.shape
    tm = MOE_TILE
    rows = MOE_SLOTS * MOE_CHUNK
    row = lambda i, s, cg, nc, st: (i, 0)
    const = lambda i, s, cg, nc, st: (0, 0)
    group_w = lambda i, s, cg, nc, st: (cg[i * MOE_SLOTS + s], 0, 0)
    mod_idx = _mod_spec(rows_per_group, group0, tm)
    plan = _moe_plan(gid[:, 0, :].reshape(-1), tm)
    grid_spec = pltpu.PrefetchScalarGridSpec(
        num_scalar_prefetch=3,
        grid=(t // tm, MOE_SLOTS),
        in_specs=[
            pl.BlockSpec((tm, d), row),
            pl.BlockSpec((tm, ROUTER_LANES), row),
            pl.BlockSpec((tm // TOKEN_TILE, SUBLANES, TOKEN_TILE), lambda i, s, cg, nc, st: (i, 0, 0)),
            pl.BlockSpec((tm, d), row),
            pl.BlockSpec((1, 6, d), lambda i, s, cg, nc, st: mod_idx(i)),
            pl.BlockSpec((1,) + wgu.shape[1:], group_w),
            pl.BlockSpec((1,) + wd.shape[1:], group_w),
            pl.BlockSpec((1, d), const),
            pl.BlockSpec((1, d), const),
        ],
        out_specs=pl.BlockSpec((tm, d), row),
        scratch_shapes=[
            pltpu.VMEM((rows, tm), BF16),
            pltpu.VMEM((rows, d), BF16),
            pltpu.VMEM((rows, ROUTER_LANES), F32),
            pltpu.VMEM((rows, d), BF16),
            pltpu.VMEM((MOE_CHUNK, wd.shape[1]), BF16),
        ],
    )
    return pl.pallas_call(
        functools.partial(_moe_kernel, alpha=alpha),
        grid_spec=grid_spec,
        out_shape=jax.ShapeDtypeStruct((t, d), F32),
        compiler_params=_cparams(("parallel", "arbitrary")),
        name="moe_ln",
    )(*plan, t_act, gates, gid, h, mod, wgu, wd, ln_g.reshape(1, d), ln_b.reshape(1, d))


def _router_matrix(router_group, router_expert):
    d = router_group.shape[0]
    pad = jnp.zeros((d, ROUTER_ROWS - N_GROUPS - N_EXPERTS), F32)
    return jnp.concatenate([router_group, router_expert, pad], axis=1).T.astype(BF16)


def kernel(x, c, ctx, c_ctx, ada_w, ada_b, ln_g, ln_b, ab_w_in, ab_b_in, conv_w, conv_b, conv_ln_g, conv_ln_b,
           na_rpb, ab_w_out, ab_b_out, gqa_w_in, gqa_sink, gqa_w_out, router_group, router_expert, exp_w_gate,
           exp_w_up, exp_w_down):
    bsz, seq, d = x.shape
    n_ctx = ctx.shape[1]
    depth = ada_w.shape[0]
    assert depth == DEPTH and bsz + 1 <= MOD_ROWS
    assert seq % MOE_TILE == 0 and (bsz * n_ctx) % MOE_TILE == 0 and MOE_TILE % TOKEN_TILE == 0
    assert seq % C_BLOCK == 0 and seq % CONV_CHUNK == 0 and n_ctx % CONV_CHUNK == 0
    rows = seq // GRID_W
    assert rows % NA_ROWS_PER_STEP == 0 and rows >= NA_UNION_ROWS
    alpha = (2.0 * depth) ** 0.25
    t_lat, t_ctx = bsz * seq, bsz * n_ctx

    cc = jnp.concatenate([c, c_ctx[None], jnp.zeros((MOD_ROWS - bsz - 1, d), F32)], axis=0)
    mod = _modulation(cc, ada_w, ada_b).reshape(depth, MOD_ROWS, 6, d)
    lat_grp = dict(rows_per_group=seq, group0=0)
    ctx_grp = dict(rows_per_group=t_ctx, group0=bsz)

    h_lat = x.reshape(t_lat, d)
    h_ctx = ctx.reshape(t_ctx, d)
    for i in range(depth):
        j = i // 2
        need_ctx = i < depth - 1
        router = _router_matrix(router_group[i], router_expert[i])
        wgu, wd = _expert_weights(exp_w_gate[i], exp_w_up[i], exp_w_down[i])
        if i % 2 == 0:
            w_in = ab_w_in[j].astype(BF16)
            g_lat, q_lat, k_lat, v_lat = _inproj_ab(h_lat, mod[i], w_in, ab_b_in[j], **lat_grp)
            g_ctx, q_ctx, k_ctx, v_ctx = _inproj_ab(h_ctx, mod[i], w_in, ab_b_in[j], **ctx_grp)
            to_seq = lambda a, n: a.reshape(bsz, n, a.shape[-1])
            conv_args = (conv_w[j], conv_b[j], conv_ln_g[j], conv_ln_b[j])
            conv_lat = _conv_module(to_seq(g_lat, seq), *conv_args).reshape(t_lat, CONV_DIM)
            k_ctx, v_ctx = to_seq(k_ctx, n_ctx), to_seq(v_ctx, n_ctx)
            row_off, types = _na_geometry(rows)
            bias = _na_bias_tables(na_rpb[j], row_off)
            na_lat = _neighbourhood_attention(to_seq(q_lat, seq), to_seq(k_lat, seq), to_seq(v_lat, seq), k_ctx,
                                              v_ctx, bias, types).reshape(t_lat, NA_DIM)
            w_out = ab_w_out[j].astype(BF16)
            w_outs = [w_out[:CONV_DIM], w_out[CONV_DIM:]]
            b_out = ab_b_out[j]
            acts_lat = [conv_lat, na_lat]
            if need_ctx:
                conv_ctx = _conv_module(to_seq(g_ctx, n_ctx), *conv_args).reshape(t_ctx, CONV_DIM)
                na_ctx = _context_attention(to_seq(q_ctx, n_ctx), k_ctx, v_ctx).reshape(t_ctx, NA_DIM)
                acts_ctx = [conv_ctx, na_ctx]
        else:
            assert not need_ctx
            qn = C_HEADS * C_HEAD_DIM
            w_in = gqa_w_in[j].astype(BF16)
            cos, sin = _rope_tables(seq)
            q_lat, k_lat, v_lat = _inproj_gqa(h_lat, mod[i], w_in, cos, sin, seq)
            k_ctx, v_ctx = _inproj_kv(h_ctx, mod[i], w_in[:, qn:], **ctx_grp)
            to_seq = lambda a, n: a.reshape(bsz, n, a.shape[-1])
            att = _window_attention(to_seq(q_lat, seq), to_seq(k_lat, seq), to_seq(v_lat, seq),
                                    to_seq(k_ctx, n_ctx), to_seq(v_ctx, n_ctx), gqa_sink[j])
            acts_lat = [att.reshape(t_lat, qn)]
            w_outs = [gqa_w_out[j].astype(BF16)]
            b_out = jnp.zeros((d,), F32)
        ln1 = (ln_g[i, 0], ln_b[i, 0])
        ln2 = (ln_g[i, 1], ln_b[i, 1])
        h1, t_act, gates, gid = _post_mix(acts_lat, w_outs, b_out, h_lat, mod[i], *ln1, router, alpha=alpha, **lat_grp)
        h_lat = _moe_ln(t_act, gates, gid, h1, mod[i], wgu, wd, *ln2, alpha=alpha, **lat_grp)
        if need_ctx:
            h1, t_act, gates, gid = _post_mix(acts_ctx, w_outs, b_out, h_ctx, mod[i], *ln1, router, alpha=alpha,
                                         **ctx_grp)
            h_ctx = _moe_ln(t_act, gates, gid, h1, mod[i], wgu, wd, *ln2, alpha=alpha, **ctx_grp)
    return h_lat.reshape(bsz, seq, d)
```

```python
import functools

import numpy as np
import jax
import jax.numpy as jnp
from jax import lax
from jax.experimental import pallas as pl
from jax.experimental.pallas import tpu as pltpu

F32 = jnp.float32
BF16 = jnp.bfloat16

DEPTH = 2
GRID_W = 64
CONV_DIM = 512
CONV_WIDTH = 31
NA_HEADS = 8
NA_HEAD_DIM = 64
NA_DIM = NA_HEADS * NA_HEAD_DIM
NA_KH = 8
NA_KW = 16
C_HEADS = 16
C_KV_HEADS = 4
C_GROUP = C_HEADS // C_KV_HEADS
C_HEAD_DIM = 64
C_WINDOW = 128
C_BLOCK = 128
ROPE_BASE = 10000.0
N_GROUPS = 4
EXPERTS_PER_GROUP = 4
N_EXPERTS = N_GROUPS * EXPERTS_PER_GROUP
D_EXPERT = 256
LN_EPS = 1e-5
NEG_INF = -1e30
LOG2E = 1.4426950408889634

LANES = 128
SUBLANES = 8
VMEM_LIMIT_BYTES = 56 * 1024 * 1024

TOKEN_TILE = 512
MOE_TILE = 1024
MOE_CHUNK = 256
MOE_SLOTS = MOE_TILE // MOE_CHUNK + N_GROUPS - 1
MOE_BASE_CHUNKS = MOE_TILE // MOE_CHUNK + 1
MOD_ROWS = 24
NA_ROWS_PER_STEP = 4
NA_UNION_ROWS = NA_KH + NA_ROWS_PER_STEP
CONV_CHUNK = 256
CONV_HALO = 16
ROUTER_LANES = LANES
GATE_LANE0 = N_GROUPS
ROUTER_ROWS = 24


def _cparams(semantics):
    return pltpu.CompilerParams(dimension_semantics=semantics, vmem_limit_bytes=VMEM_LIMIT_BYTES)


def _dot(a, b):
    return jnp.dot(a, b, preferred_element_type=F32)


def _dot_nt(a, b):
    return lax.dot_general(a, b, (((1,), (1,)), ((), ())), preferred_element_type=F32)


def _dot_tn(a, b):
    return lax.dot_general(a, b, (((0,), (0,)), ((), ())), preferred_element_type=F32)


def _split_bf16(a):
    hi = a.astype(BF16)
    lo = (a - hi.astype(F32)).astype(BF16)
    return hi, lo


def _dot3(a, b):
    a_hi, a_lo = _split_bf16(a)
    b_hi, b_lo = _split_bf16(b)
    return _dot(a_hi, b_hi) + (_dot(a_lo, b_hi) + _dot(a_hi, b_lo))


def _layer_norm(x, g, b):
    mu = jnp.mean(x, axis=-1, keepdims=True)
    xc = x - mu
    var = jnp.mean(xc * xc, axis=-1, keepdims=True)
    return xc * lax.rsqrt(var + LN_EPS) * g + b


def _silu(x):
    return x * jax.nn.sigmoid(x)


def _mod_kernel(cc_ref, w_ref, b_ref, o_ref):
    o_ref[0] = _dot3(_silu(cc_ref[...]), w_ref[0]) + b_ref[0]


def _modulation(cc, ada_w, ada_b):
    depth, d, n = ada_w.shape
    tn = n // 4
    return pl.pallas_call(
        _mod_kernel,
        grid=(depth, n // tn),
        in_specs=[
            pl.BlockSpec((MOD_ROWS, d), lambda i, j: (0, 0)),
            pl.BlockSpec((1, d, tn), lambda i, j: (i, 0, j)),
            pl.BlockSpec((1, 1, tn), lambda i, j: (i, 0, j)),
        ],
        out_specs=pl.BlockSpec((1, MOD_ROWS, tn), lambda i, j: (i, 0, j)),
        out_shape=jax.ShapeDtypeStruct((depth, MOD_ROWS, n), F32),
        compiler_params=_cparams(("arbitrary", "arbitrary")),
        name="adaln_mod",
    )(cc, ada_w, ada_b.reshape(depth, 1, n))


def _mod_spec(rows_per_group, group0, tm):
    return lambda i: (group0 + (i * tm) // rows_per_group, 0, 0)


def _modulate(h, m, shift_row, scale_row):
    return h * (1.0 + m[scale_row:scale_row + 1]) + m[shift_row:shift_row + 1]


def _inproj_ab_kernel(h_ref, mod_ref, w_ref, b_ref, g_ref, q_ref, k_ref, v_ref):
    u = _modulate(h_ref[...], mod_ref[0], 0, 1).astype(BF16)
    c = CONV_DIM
    za = _dot(u, w_ref[:, 0:c]) + b_ref[:, 0:c]
    zb = _dot(u, w_ref[:, c:2 * c]) + b_ref[:, c:2 * c]
    g_ref[...] = za * jax.nn.sigmoid(zb)
    q0 = 2 * c
    zq = _dot(u, w_ref[:, q0:q0 + NA_DIM]) + b_ref[:, q0:q0 + NA_DIM]
    q_ref[...] = (zq * (NA_HEAD_DIM ** -0.5 * LOG2E)).astype(BF16)
    k0 = q0 + NA_DIM
    k_ref[...] = (_dot(u, w_ref[:, k0:k0 + NA_DIM]) + b_ref[:, k0:k0 + NA_DIM]).astype(BF16)
    v0 = k0 + NA_DIM
    v_ref[...] = (_dot(u, w_ref[:, v0:v0 + NA_DIM]) + b_ref[:, v0:v0 + NA_DIM]).astype(BF16)


def _inproj_ab(h, mod, w, b, rows_per_group, group0):
    t, d = h.shape
    n = w.shape[1]
    tm = TOKEN_TILE
    row = lambda i: (i, 0)
    return pl.pallas_call(
        _inproj_ab_kernel,
        grid=(t // tm,),
        in_specs=[
            pl.BlockSpec((tm, d), row),
            pl.BlockSpec((1, 6, d), _mod_spec(rows_per_group, group0, tm)),
            pl.BlockSpec((d, n), lambda i: (0, 0)),
            pl.BlockSpec((1, n), lambda i: (0, 0)),
        ],
        out_specs=[
            pl.BlockSpec((tm, CONV_DIM), row),
            pl.BlockSpec((tm, NA_DIM), row),
            pl.BlockSpec((tm, NA_DIM), row),
            pl.BlockSpec((tm, NA_DIM), row),
        ],
        out_shape=[
            jax.ShapeDtypeStruct((t, CONV_DIM), F32),
            jax.ShapeDtypeStruct((t, NA_DIM), BF16),
            jax.ShapeDtypeStruct((t, NA_DIM), BF16),
            jax.ShapeDtypeStruct((t, NA_DIM), BF16),
        ],
        compiler_params=_cparams(("parallel",)),
        name="inproj_conv_na",
    )(h, mod, w, b.reshape(1, n))


def _conv_kernel(g_ref, w_ref, cb_ref, lg_ref, lb_ref, o_ref, pad_ref, *, seq):
    zeros = jnp.zeros((CONV_HALO, CONV_DIM), F32)
    pad_ref[0:CONV_HALO, :] = zeros
    pad_ref[CONV_HALO + seq:2 * CONV_HALO + seq, :] = zeros
    pad_ref[CONV_HALO:CONV_HALO + seq, :] = g_ref[0]
    first = CONV_HALO - CONV_WIDTH // 2
    ext = CONV_CHUNK + SUBLANES

    def chunk(i, carry):
        r0 = pl.multiple_of(i * CONV_CHUNK, CONV_CHUNK)
        acc = jnp.zeros((CONV_CHUNK, CONV_DIM), F32) + cb_ref[...]
        for res in range(SUBLANES):
            part = None
            for base in range(0, first + CONV_WIDTH, SUBLANES):
                tap = base + res - first
                if 0 <= tap < CONV_WIDTH:
                    term = pad_ref[pl.ds(r0 + base, ext), :] * w_ref[tap:tap + 1, :]
                    part = term if part is None else part + term
            acc = acc + part[res:res + CONV_CHUNK]
        y = _layer_norm(acc, lg_ref[...], lb_ref[...])
        o_ref[0, pl.ds(r0, CONV_CHUNK), :] = _silu(y).astype(BF16)
        return carry

    lax.fori_loop(0, seq // CONV_CHUNK, chunk, 0)


def _conv_module(g, conv_w, conv_b, ln_g, ln_b):
    bsz, seq, c = g.shape
    vec = lambda i: (0, 0)
    return pl.pallas_call(
        functools.partial(_conv_kernel, seq=seq),
        grid=(bsz,),
        in_specs=[
            pl.BlockSpec((1, seq, c), lambda i: (i, 0, 0)),
            pl.BlockSpec((CONV_WIDTH, c), vec),
            pl.BlockSpec((1, c), vec),
            pl.BlockSpec((1, c), vec),
            pl.BlockSpec((1, c), vec),
        ],
        out_specs=pl.BlockSpec((1, seq, c), lambda i: (i, 0, 0)),
        out_shape=jax.ShapeDtypeStruct((bsz, seq, c), BF16),
        scratch_shapes=[pltpu.VMEM((seq + 2 * CONV_HALO, c), F32)],
        compiler_params=_cparams(("parallel",)),
        name="conv_module",
    )(g, conv_w, conv_b.reshape(1, c), ln_g.reshape(1, c), ln_b.reshape(1, c))


N_ROW_OFFS = 2 * NA_KH - 1
N_COL_OFFS = 2 * NA_KW - 1


def _na_geometry(rows):
    rq, ru = NA_ROWS_PER_STEP, NA_UNION_ROWS
    geoms, types = [], []
    for step in range(rows // rq):
        r = step * rq + np.arange(rq)[:, None]
        key_row = np.clip(step * rq - NA_KH // 2, 0, rows - ru) + np.arange(ru)[None, :]
        row_start = np.clip(r - NA_KH // 2, 0, rows - NA_KH)
        row_in = (key_row >= row_start) & (key_row < row_start + NA_KH)
        geom = np.where(row_in, key_row - r + NA_KH - 1, N_ROW_OFFS).astype(np.int32)
        for t, other in enumerate(geoms):
            if np.array_equal(other, geom):
                types.append(t)
                break
        else:
            types.append(len(geoms))
            geoms.append(geom)
    return np.stack(geoms), np.asarray(types, np.int32)


def _na_bias_kernel(row_off_ref, rpb_ref, o_ref, tile_ref):
    head = pl.program_id(0)
    qc = lax.broadcasted_iota(jnp.int32, (GRID_W, GRID_W), 0)
    kc = lax.broadcasted_iota(jnp.int32, (GRID_W, GRID_W), 1)
    col_start = jnp.clip(qc - NA_KW // 2, 0, GRID_W - NA_KW)
    col_in = (kc >= col_start) & (kc < col_start + NA_KW)
    col_off = jnp.clip(kc - qc, -(NA_KW - 1), NA_KW - 1) + NA_KW - 1
    for a in range(N_ROW_OFFS):
        tile = jnp.zeros((GRID_W, GRID_W), F32)
        for b in range(N_COL_OFFS):
            tile = jnp.where(col_off == b, rpb_ref[(head * N_ROW_OFFS + a) * N_COL_OFFS + b], tile)
        tile_ref[a] = jnp.where(col_in, tile * LOG2E, NEG_INF)
    tile_ref[N_ROW_OFFS] = jnp.full((GRID_W, GRID_W), NEG_INF, F32)
    for typ in range(o_ref.shape[0]):
        for qr in range(NA_ROWS_PER_STEP):
            for kr in range(NA_UNION_ROWS):
                a = row_off_ref[(typ * NA_ROWS_PER_STEP + qr) * NA_UNION_ROWS + kr]
                o_ref[typ, 0, qr * GRID_W:(qr + 1) * GRID_W, kr * GRID_W:(kr + 1) * GRID_W] = tile_ref[a]


def _na_bias_tables(rpb, row_off):
    n_types = row_off.shape[0]
    tq = NA_ROWS_PER_STEP * GRID_W
    n_win = NA_UNION_ROWS * GRID_W
    return pl.pallas_call(
        _na_bias_kernel,
        grid=(NA_HEADS,),
        in_specs=[pl.BlockSpec(memory_space=pltpu.SMEM), pl.BlockSpec(memory_space=pltpu.SMEM)],
        out_specs=pl.BlockSpec((n_types, 1, tq, n_win), lambda h: (0, h, 0, 0)),
        out_shape=jax.ShapeDtypeStruct((n_types, NA_HEADS, tq, n_win), F32),
        scratch_shapes=[pltpu.VMEM((N_ROW_OFFS + 1, GRID_W, GRID_W), F32)],
        compiler_params=_cparams(("parallel",)),
        name="na_bias_table",
    )(jnp.asarray(row_off.reshape(-1)), rpb.astype(F32).reshape(-1))


def _softmax_rows(scores):
    m = functools.reduce(jnp.maximum, [jnp.max(s, axis=1, keepdims=True) for s in scores])
    ps = [jnp.exp2(s - m) for s in scores]
    denom = functools.reduce(jnp.add, [jnp.sum(p, axis=1, keepdims=True) for p in ps])
    return [p.astype(BF16) for p in ps], denom


def _pv_rows(ps, denom, values):
    return functools.reduce(jnp.add, [_dot(p, v) for p, v in zip(ps, values)]) / denom


def _na_kernel(type_ref, q_ref, k_ref, v_ref, kc_ref, vc_ref, bias_ref, o_ref, *, rows):
    del type_ref
    step = pl.program_id(1)
    start_row = jnp.clip(step * NA_ROWS_PER_STEP - NA_KH // 2, 0, rows - NA_UNION_ROWS)
    start = pl.multiple_of(start_row * GRID_W, GRID_W)
    n_win = NA_UNION_ROWS * GRID_W
    heads = [slice(h * NA_HEAD_DIM, (h + 1) * NA_HEAD_DIM) for h in range(NA_HEADS)]
    scores = []
    for h, sl in enumerate(heads):
        qh = q_ref[0, :, sl]
        scores.append([_dot_nt(qh, k_ref[0, pl.ds(start, n_win), sl]) + bias_ref[0, h],
                       _dot_nt(qh, kc_ref[0, :, sl])])
    probs = [_softmax_rows(s) for s in scores]
    for sl, (ps, denom) in zip(heads, probs):
        out = _pv_rows(ps, denom, [v_ref[0, pl.ds(start, n_win), sl], vc_ref[0, :, sl]])
        o_ref[0, :, sl] = out.astype(BF16)


def _neighbourhood_attention(q, k, v, k_ctx, v_ctx, bias, types):
    bsz, seq, dim = q.shape
    n_ctx = k_ctx.shape[1]
    rows = seq // GRID_W
    tq = NA_ROWS_PER_STEP * GRID_W
    n_win = NA_UNION_ROWS * GRID_W
    full = lambda b, s, t: (b, 0, 0)
    grid_spec = pltpu.PrefetchScalarGridSpec(
        num_scalar_prefetch=1,
        grid=(bsz, seq // tq),
        in_specs=[
            pl.BlockSpec((1, tq, dim), lambda b, s, t: (b, s, 0)),
            pl.BlockSpec((1, seq, dim), full),
            pl.BlockSpec((1, seq, dim), full),
            pl.BlockSpec((1, n_ctx, dim), full),
            pl.BlockSpec((1, n_ctx, dim), full),
            pl.BlockSpec((1, NA_HEADS, tq, n_win), lambda b, s, t: (t[s], 0, 0, 0)),
        ],
        out_specs=pl.BlockSpec((1, tq, dim), lambda b, s, t: (b, s, 0)),
    )
    return pl.pallas_call(
        functools.partial(_na_kernel, rows=rows),
        grid_spec=grid_spec,
        out_shape=jax.ShapeDtypeStruct((bsz, seq, dim), BF16),
        compiler_params=_cparams(("parallel", "arbitrary")),
        name="neighbourhood_attn",
    )(jnp.asarray(types), q, k, v, k_ctx, v_ctx, bias)


def _ctx_attn_kernel(q_ref, k_ref, v_ref, o_ref):
    for h in range(NA_HEADS):
        sl = slice(h * NA_HEAD_DIM, (h + 1) * NA_HEAD_DIM)
        ps, denom = _softmax_rows([_dot_nt(q_ref[0, :, sl], k_ref[0, :, sl])])
        o_ref[0, :, sl] = _pv_rows(ps, denom, [v_ref[0, :, sl]]).astype(BF16)


def _context_attention(q, k, v):
    bsz, n, dim = q.shape
    spec = pl.BlockSpec((1, n, dim), lambda b: (b, 0, 0))
    return pl.pallas_call(
        _ctx_attn_kernel,
        grid=(bsz,),
        in_specs=[spec, spec, spec],
        out_specs=spec,
        out_shape=jax.ShapeDtypeStruct((bsz, n, dim), BF16),
        compiler_params=_cparams(("parallel",)),
        name="context_attn",
    )(q, k, v)


def _rope(x, cos, sin_signed):
    n = x.shape[1]
    half = C_HEAD_DIM // 4
    lane = lax.broadcasted_iota(jnp.int32, x.shape, 1)
    partner = jnp.where(lane % (2 * half) < half, pltpu.roll(x, n - half, 1), pltpu.roll(x, half, 1))
    reps = n // LANES
    cos_full = jnp.concatenate([cos] * reps, axis=1)
    sin_full = jnp.concatenate([sin_signed] * reps, axis=1)
    return x * cos_full + partner * sin_full


def _inproj_gqa_kernel(h_ref, mod_ref, w_ref, cos_ref, sin_ref, q_ref, k_ref, v_ref):
    u = _modulate(h_ref[...], mod_ref[0], 0, 1).astype(BF16)
    qn = C_HEADS * C_HEAD_DIM
    kn = C_KV_HEADS * C_HEAD_DIM
    cos, sin = cos_ref[...], sin_ref[...]
    q = _rope(_dot(u, w_ref[:, 0:qn]), cos, sin)
    q_ref[...] = (q * (C_HEAD_DIM ** -0.5 * LOG2E)).astype(BF16)
    k_ref[...] = _rope(_dot(u, w_ref[:, qn:qn + kn]), cos, sin).astype(BF16)
    v_ref[...] = _dot(u, w_ref[:, qn + kn:qn + 2 * kn]).astype(BF16)


def _inproj_gqa(h, mod, w, cos, sin, seq):
    t, d = h.shape
    n = w.shape[1]
    tm = TOKEN_TILE
    qn = C_HEADS * C_HEAD_DIM
    kn = C_KV_HEADS * C_HEAD_DIM
    row = lambda i: (i, 0)
    pos = lambda i: (i % (seq // tm), 0)
    return pl.pallas_call(
        _inproj_gqa_kernel,
        grid=(t // tm,),
        in_specs=[
            pl.BlockSpec((tm, d), row),
            pl.BlockSpec((1, 6, d), _mod_spec(seq, 0, tm)),
            pl.BlockSpec((d, n), lambda i: (0, 0)),
            pl.BlockSpec((tm, LANES), pos),
            pl.BlockSpec((tm, LANES), pos),
        ],
        out_specs=[pl.BlockSpec((tm, qn), row), pl.BlockSpec((tm, kn), row), pl.BlockSpec((tm, kn), row)],
        out_shape=[
            jax.ShapeDtypeStruct((t, qn), BF16),
            jax.ShapeDtypeStruct((t, kn), BF16),
            jax.ShapeDtypeStruct((t, kn), BF16),
        ],
        compiler_params=_cparams(("parallel",)),
        name="inproj_gqa",
    )(h, mod, w, cos, sin)


def _inproj_kv_kernel(h_ref, mod_ref, w_ref, k_ref, v_ref):
    u = _modulate(h_ref[...], mod_ref[0], 0, 1).astype(BF16)
    kn = C_KV_HEADS * C_HEAD_DIM
    k_ref[...] = _dot(u, w_ref[:, 0:kn]).astype(BF16)
    v_ref[...] = _dot(u, w_ref[:, kn:2 * kn]).astype(BF16)


def _inproj_kv(h, mod, w, rows_per_group, group0):
    t, d = h.shape
    n = w.shape[1]
    tm = TOKEN_TILE
    kn = C_KV_HEADS * C_HEAD_DIM
    row = lambda i: (i, 0)
    return pl.pallas_call(
        _inproj_kv_kernel,
        grid=(t // tm,),
        in_specs=[
            pl.BlockSpec((tm, d), row),
            pl.BlockSpec((1, 6, d), _mod_spec(rows_per_group, group0, tm)),
            pl.BlockSpec((d, n), lambda i: (0, 0)),
        ],
        out_specs=[pl.BlockSpec((tm, kn), row), pl.BlockSpec((tm, kn), row)],
        out_shape=[jax.ShapeDtypeStruct((t, kn), BF16), jax.ShapeDtypeStruct((t, kn), BF16)],
        compiler_params=_cparams(("parallel",)),
        name="inproj_ctx_kv",
    )(h, mod, w)


def _rope_tables(seq):
    t = jnp.arange(seq)
    row = (t // GRID_W).astype(F32)
    col = (t % GRID_W).astype(F32)
    axis_dim = C_HEAD_DIM // 2
    inv_freq = ROPE_BASE ** (-jnp.arange(0, axis_dim, 2, dtype=F32) / axis_dim)
    ang_r = row[:, None] * inv_freq
    ang_c = col[:, None] * inv_freq
    cos = jnp.concatenate([jnp.cos(ang_r)] * 2 + [jnp.cos(ang_c)] * 2, axis=-1)
    sin = jnp.concatenate([-jnp.sin(ang_r), jnp.sin(ang_r), -jnp.sin(ang_c), jnp.sin(ang_c)], axis=-1)
    reps = LANES // C_HEAD_DIM
    return jnp.tile(cos, (1, reps)), jnp.tile(sin, (1, reps))


def _gqa_kernel(sink_ref, q_ref, k_ref, v_ref, kc_ref, vc_ref, o_ref, *, seq):
    blk = pl.program_id(1)
    span = C_BLOCK + 2 * C_WINDOW
    start = pl.multiple_of(jnp.clip(blk * C_BLOCK - C_WINDOW, 0, seq - span), C_BLOCK)
    k_pos = start + lax.broadcasted_iota(jnp.int32, (span, C_BLOCK), 0)
    q_pos = blk * C_BLOCK + lax.broadcasted_iota(jnp.int32, (span, C_BLOCK), 1)
    valid = jnp.abs(q_pos - k_pos) <= C_WINDOW
    valid = jnp.concatenate([valid] * C_GROUP, axis=1)
    d = C_HEAD_DIM

    def scores(hk):
        q0 = hk * C_GROUP * d
        qs = jnp.concatenate([q_ref[0, :, q0 + g * d:q0 + (g + 1) * d] for g in range(C_GROUP)], axis=0)
        ksl = slice(hk * d, (hk + 1) * d)
        return (jnp.where(valid, _dot_nt(k_ref[0, pl.ds(start, span), ksl], qs), NEG_INF),
                _dot_nt(kc_ref[0, :, ksl], qs))

    def softmax(hk, s):
        s_lat, s_ctx = s
        sink = jnp.concatenate(
            [jnp.full((1, C_BLOCK), sink_ref[hk * C_GROUP + g] * LOG2E, F32) for g in range(C_GROUP)], axis=1)
        m = jnp.maximum(jnp.maximum(jnp.max(s_lat, axis=0, keepdims=True), jnp.max(s_ctx, axis=0, keepdims=True)),
                        sink)
        p_lat = jnp.exp2(s_lat - m)
        p_ctx = jnp.exp2(s_ctx - m)
        denom = jnp.sum(p_lat, axis=0, keepdims=True) + jnp.sum(p_ctx, axis=0, keepdims=True) + jnp.exp2(sink - m)
        inv = 1.0 / denom
        return (p_lat * inv).astype(BF16), (p_ctx * inv).astype(BF16)

    def values(hk, p):
        ksl = slice(hk * d, (hk + 1) * d)
        q0 = hk * C_GROUP * d
        out = _dot_tn(p[0], v_ref[0, pl.ds(start, span), ksl]) + _dot_tn(p[1], vc_ref[0, :, ksl])
        for g in range(C_GROUP):
            o_ref[0, :, q0 + g * d:q0 + (g + 1) * d] = out[g * C_BLOCK:(g + 1) * C_BLOCK].astype(BF16)

    s = {0: scores(0), 1: scores(1)}
    p = {0: softmax(0, s[0])}
    for hk in range(C_KV_HEADS):
        if hk + 2 < C_KV_HEADS:
            s[hk + 2] = scores(hk + 2)
        if hk + 1 < C_KV_HEADS:
            p[hk + 1] = softmax(hk + 1, s[hk + 1])
        values(hk, p[hk])


def _window_attention(q, k, v, k_ctx, v_ctx, sink):
    bsz, seq, qn = q.shape
    kn = k.shape[2]
    n_ctx = k_ctx.shape[1]
    full = lambda b, s: (b, 0, 0)
    return pl.pallas_call(
        functools.partial(_gqa_kernel, seq=seq),
        grid=(bsz, seq // C_BLOCK),
        in_specs=[
            pl.BlockSpec(memory_space=pltpu.SMEM),
            pl.BlockSpec((1, C_BLOCK, qn), lambda b, s: (b, s, 0)),
            pl.BlockSpec((1, seq, kn), full),
            pl.BlockSpec((1, seq, kn), full),
            pl.BlockSpec((1, n_ctx, kn), full),
            pl.BlockSpec((1, n_ctx, kn), full),
        ],
        out_specs=pl.BlockSpec((1, C_BLOCK, qn), lambda b, s: (b, s, 0)),
        out_shape=jax.ShapeDtypeStruct((bsz, seq, qn), BF16),
        compiler_params=_cparams(("parallel", "arbitrary")),
        name="window_gqa_attn",
    )(sink, q, k, v, k_ctx, v_ctx)


def _route(logits):
    row = lax.broadcasted_iota(jnp.int32, logits.shape, 0)
    big = jnp.int32(ROUTER_ROWS)
    lg = jnp.where(row < N_GROUPS, logits, NEG_INF)
    g_max = jnp.max(lg, axis=0, keepdims=True)
    g_prob = 1.0 / jnp.sum(jnp.exp(lg - g_max), axis=0, keepdims=True)
    g_idx = jnp.min(jnp.where(lg == g_max, row, big), axis=0, keepdims=True)
    first = GATE_LANE0 + g_idx * EXPERTS_PER_GROUP
    in_group = (row >= first) & (row < first + EXPERTS_PER_GROUP)
    le = jnp.where(in_group, logits, NEG_INF)
    e1 = jnp.max(le, axis=0, keepdims=True)
    i1 = jnp.min(jnp.where(le == e1, row, big), axis=0, keepdims=True)
    le2 = jnp.where(row == i1, NEG_INF, le)
    e2 = jnp.max(le2, axis=0, keepdims=True)
    i2 = jnp.min(jnp.where(le2 == e2, row, big), axis=0, keepdims=True)
    r = jnp.exp(e2 - e1)
    w1 = g_prob / (1.0 + r)
    w2 = w1 * r
    own = [jnp.where(i1 == first + j, w1, jnp.where(i2 == first + j, w2, 0.0)) for j in range(EXPERTS_PER_GROUP)]
    return g_idx, own


def _post_mix_kernel(*refs, n_in, alpha):
    a_refs = refs[:n_in]
    w_refs = refs[n_in:2 * n_in]
    b_ref, h_ref, mod_ref, lg_ref, lb_ref, r_ref, h1_ref, t_ref, gate_ref, gid_ref = refs[2 * n_in:]
    o = functools.reduce(jnp.add, [_dot(a[...], w[...]) for a, w in zip(a_refs, w_refs)]) + b_ref[...]
    m = mod_ref[0]
    h1 = _layer_norm(alpha * h_ref[...] + m[2:3] * o, lg_ref[...], lb_ref[...])
    h1_ref[...] = h1
    t = _modulate(h1, m, 3, 4).astype(BF16)
    t_ref[...] = t
    group, own = _route(_dot_nt(r_ref[...], t))
    parts = _split3(jnp.concatenate(own, axis=0))
    rows = jnp.concatenate([p.astype(F32) for p in parts], axis=0)
    pad = jnp.zeros((ROUTER_LANES - rows.shape[0], rows.shape[1]), F32)
    gate_ref[...] = jnp.transpose(jnp.concatenate([rows, pad], axis=0)).astype(BF16)
    row = lax.broadcasted_iota(jnp.int32, gid_ref.shape[1:], 0)
    count = jnp.sum((row - 1 == group).astype(F32), axis=1, keepdims=True).astype(jnp.int32)
    gid_ref[0] = jnp.where(row == 0, group, count)


def _post_mix(acts, weights, bias, h, mod, ln_g, ln_b, router, rows_per_group, group0, alpha):
    t, d = h.shape
    tm = TOKEN_TILE
    row = lambda i: (i, 0)
    const = lambda i: (0, 0)
    in_specs = [pl.BlockSpec((tm, a.shape[1]), row) for a in acts]
    in_specs += [pl.BlockSpec(w.shape, const) for w in weights]
    in_specs += [
        pl.BlockSpec((1, d), const),
        pl.BlockSpec((tm, d), row),
        pl.BlockSpec((1, 6, d), _mod_spec(rows_per_group, group0, tm)),
        pl.BlockSpec((1, d), const),
        pl.BlockSpec((1, d), const),
        pl.BlockSpec((ROUTER_ROWS, d), const),
    ]
    return pl.pallas_call(
        functools.partial(_post_mix_kernel, n_in=len(acts), alpha=alpha),
        grid=(t // tm,),
        in_specs=in_specs,
        out_specs=[pl.BlockSpec((tm, d), row), pl.BlockSpec((tm, d), row), pl.BlockSpec((tm, ROUTER_LANES), row),
                   pl.BlockSpec((1, SUBLANES, tm), lambda i: (i, 0, 0))],
        out_shape=[
            jax.ShapeDtypeStruct((t, d), F32),
            jax.ShapeDtypeStruct((t, d), BF16),
            jax.ShapeDtypeStruct((t, ROUTER_LANES), BF16),
            jax.ShapeDtypeStruct((t // tm, SUBLANES, tm), jnp.int32),
        ],
        compiler_params=_cparams(("parallel",)),
        name="outproj_ln_router",
    )(*acts, *weights, bias.reshape(1, d), h, mod, ln_g.reshape(1, d), ln_b.reshape(1, d), router)


def _split3(a):
    hi = a.astype(BF16)
    rest = a - hi.astype(F32)
    mid = rest.astype(BF16)
    return hi, mid, (rest - mid.astype(F32)).astype(BF16)


def _expert_group(x, gate_cols, wg_ref, wu_ref, wd_ref, hid_ref):
    f = D_EXPERT
    for j, gate in enumerate(gate_cols):
        hid = _silu(_dot(x, wg_ref[j])) * _dot(x, wu_ref[j]) * gate
        hid_ref[:, f * j:f * (j + 1)] = hid.astype(BF16)
    return _dot(hid_ref[...], wd_ref[0])


def _expert_weights(w_gate, w_up, w_down):
    e, f, d = w_down.shape
    return (w_gate.astype(BF16), w_up.astype(BF16),
            w_down.astype(BF16).reshape(N_GROUPS, EXPERTS_PER_GROUP * f, d))


def _moe_plan(counts):
    chunks = (counts + MOE_CHUNK - 1) // MOE_CHUNK
    ends = jnp.cumsum(chunks, axis=1)
    slot = jnp.arange(MOE_SLOTS, dtype=jnp.int32)
    chunk_group = jnp.minimum(jnp.sum(ends[:, None, :] <= slot[None, :, None], axis=2), N_GROUPS - 1)
    start_row = (ends - chunks) * MOE_CHUNK
    return (chunk_group.astype(jnp.int32).reshape(-1), ends[:, -1].astype(jnp.int32),
            start_row.astype(jnp.int32).reshape(-1))


def _moe_kernel(chunk_group_ref, n_chunks_ref, start_ref, t_ref, gate_ref, gid_ref, h_ref, mod_ref, wg_ref,
                wu_ref, wd_ref, lg_ref, lb_ref, o_ref, perm_ref, xp_ref, gp_ref, yp_ref, hid_ref, *, alpha):
    tile = pl.program_id(0)
    step = pl.program_id(1)
    tm = t_ref.shape[0]

    @pl.when(step == 0)
    def _():
        gid = jnp.concatenate([gid_ref[i, 0:1, :] for i in range(gid_ref.shape[0])], axis=1)
        is_grp = lax.broadcasted_iota(jnp.int32, (SUBLANES, tm), 0) == gid
        upper = (lax.broadcasted_iota(jnp.int32, (tm, tm), 0) <= lax.broadcasted_iota(jnp.int32, (tm, tm), 1))
        count = _dot(is_grp.astype(BF16), upper.astype(BF16))
        rank = jnp.sum(jnp.where(is_grp, count, 0.0), axis=0, keepdims=True).astype(jnp.int32) - 1
        start = jnp.zeros_like(gid)
        for g in range(N_GROUPS):
            start = jnp.where(gid == g, start_ref[tile * N_GROUPS + g], start)
        dest = start + rank
        perm_ref[...] = (lax.broadcasted_iota(jnp.int32, (perm_ref.shape[0], tm), 0) == dest).astype(BF16)

        def move(rows):
            perm = perm_ref[rows, :]
            xp_ref[rows, :] = _dot(perm, t_ref[...]).astype(BF16)
            gp_ref[rows, :] = _dot(perm, gate_ref[...])

        move(slice(0, MOE_BASE_CHUNKS * MOE_CHUNK))
        for s in range(MOE_BASE_CHUNKS, MOE_SLOTS):
            pl.when(s < n_chunks_ref[tile])(functools.partial(move, slice(s * MOE_CHUNK, (s + 1) * MOE_CHUNK)))

    @pl.when(step < n_chunks_ref[tile])
    def _():
        rows = pl.ds(pl.multiple_of(step * MOE_CHUNK, MOE_CHUNK), MOE_CHUNK)
        gp = gp_ref[rows, :]
        e = EXPERTS_PER_GROUP
        cols = [gp[:, j:j + 1] + gp[:, e + j:e + j + 1] + gp[:, 2 * e + j:2 * e + j + 1] for j in range(e)]
        yp_ref[rows, :] = _expert_group(xp_ref[rows, :], cols, wg_ref, wu_ref, wd_ref, hid_ref).astype(BF16)

    @pl.when((step >= n_chunks_ref[tile]) & (step < MOE_BASE_CHUNKS))
    def _():
        rows = pl.ds(pl.multiple_of(step * MOE_CHUNK, MOE_CHUNK), MOE_CHUNK)
        yp_ref[rows, :] = jnp.zeros((MOE_CHUNK, yp_ref.shape[1]), BF16)

    @pl.when(step == MOE_SLOTS - 1)
    def _():
        base = slice(0, MOE_BASE_CHUNKS * MOE_CHUNK)
        o_ref[...] = _dot_tn(perm_ref[base, :], yp_ref[base, :])
        for s in range(MOE_BASE_CHUNKS, MOE_SLOTS):
            @pl.when(s < n_chunks_ref[tile])
            def _():
                rows = slice(s * MOE_CHUNK, (s + 1) * MOE_CHUNK)
                o_ref[...] += _dot_tn(perm_ref[rows, :], yp_ref[rows, :])
        o_ref[...] = _layer_norm(alpha * h_ref[...] + mod_ref[0][5:6] * o_ref[...], lg_ref[...], lb_ref[...])


def _moe_ln(t_act, gates, gid, h, mod, wg, wu, wd, ln_g, ln_b, rows_per_group, group0, alpha):
    t, d = h.shape
    tm = MOE_TILE
    f = wg.shape[2]
    rows = MOE_SLOTS * MOE_CHUNK
    row = lambda i, s, cg, nc, st: (i, 0)
    const = lambda i, s, cg, nc, st: (0, 0)
    group_w = lambda i, s, cg, nc, st: (cg[i * MOE_SLOTS + s], 0, 0)
    mod_idx = _mod_spec(rows_per_group, group0, tm)
    counts = gid[:, 1:1 + N_GROUPS, 0].reshape(t // tm, tm // TOKEN_TILE, N_GROUPS).sum(axis=1)
    plan = _moe_plan(counts)
    grid_spec = pltpu.PrefetchScalarGridSpec(
        num_scalar_prefetch=3,
        grid=(t // tm, MOE_SLOTS),
        in_specs=[
            pl.BlockSpec((tm, d), row),
            pl.BlockSpec((tm, ROUTER_LANES), row),
            pl.BlockSpec((tm // TOKEN_TILE, SUBLANES, TOKEN_TILE), lambda i, s, cg, nc, st: (i, 0, 0)),
            pl.BlockSpec((tm, d), row),
            pl.BlockSpec((1, 6, d), lambda i, s, cg, nc, st: mod_idx(i)),
            pl.BlockSpec((EXPERTS_PER_GROUP, d, f), group_w),
            pl.BlockSpec((EXPERTS_PER_GROUP, d, f), group_w),
            pl.BlockSpec((1,) + wd.shape[1:], group_w),
            pl.BlockSpec((1, d), const),
            pl.BlockSpec((1, d), const),
        ],
        out_specs=pl.BlockSpec((tm, d), row),
        scratch_shapes=[
            pltpu.VMEM((rows, tm), BF16),
            pltpu.VMEM((rows, d), BF16),
            pltpu.VMEM((rows, ROUTER_LANES), F32),
            pltpu.VMEM((rows, d), BF16),
            pltpu.VMEM((MOE_CHUNK, wd.shape[1]), BF16),
        ],
    )
    return pl.pallas_call(
        functools.partial(_moe_kernel, alpha=alpha),
        grid_spec=grid_spec,
        out_shape=jax.ShapeDtypeStruct((t, d), F32),
        compiler_params=_cparams(("parallel", "arbitrary")),
        name="moe_ln",
    )(*plan, t_act, gates, gid, h, mod, wg, wu, wd, ln_g.reshape(1, d), ln_b.reshape(1, d))


def _router_matrix(router_group, router_expert):
    d = router_group.shape[0]
    pad = jnp.zeros((d, ROUTER_ROWS - N_GROUPS - N_EXPERTS), F32)
    return jnp.concatenate([router_group, router_expert, pad], axis=1).T.astype(BF16)


def kernel(x, c, ctx, c_ctx, ada_w, ada_b, ln_g, ln_b, ab_w_in, ab_b_in, conv_w, conv_b, conv_ln_g, conv_ln_b,
           na_rpb, ab_w_out, ab_b_out, gqa_w_in, gqa_sink, gqa_w_out, router_group, router_expert, exp_w_gate,
           exp_w_up, exp_w_down):
    bsz, seq, d = x.shape
    n_ctx = ctx.shape[1]
    depth = ada_w.shape[0]
    assert depth == DEPTH and bsz + 1 <= MOD_ROWS
    assert seq % MOE_TILE == 0 and (bsz * n_ctx) % MOE_TILE == 0 and MOE_TILE % TOKEN_TILE == 0
    assert seq % C_BLOCK == 0 and seq % CONV_CHUNK == 0 and n_ctx % CONV_CHUNK == 0
    rows = seq // GRID_W
    assert rows % NA_ROWS_PER_STEP == 0 and rows >= NA_UNION_ROWS
    alpha = (2.0 * depth) ** 0.25
    t_lat, t_ctx = bsz * seq, bsz * n_ctx

    cc = jnp.concatenate([c, c_ctx[None], jnp.zeros((MOD_ROWS - bsz - 1, d), F32)], axis=0)
    mod = _modulation(cc, ada_w, ada_b).reshape(depth, MOD_ROWS, 6, d)
    lat_grp = dict(rows_per_group=seq, group0=0)
    ctx_grp = dict(rows_per_group=t_ctx, group0=bsz)

    h_lat = x.reshape(t_lat, d)
    h_ctx = ctx.reshape(t_ctx, d)
    for i in range(depth):
        j = i // 2
        need_ctx = i < depth - 1
        router = _router_matrix(router_group[i], router_expert[i])
        wg, wu, wd = _expert_weights(exp_w_gate[i], exp_w_up[i], exp_w_down[i])
        if i % 2 == 0:
            w_in = ab_w_in[j].astype(BF16)
            g_lat, q_lat, k_lat, v_lat = _inproj_ab(h_lat, mod[i], w_in, ab_b_in[j], **lat_grp)
            g_ctx, q_ctx, k_ctx, v_ctx = _inproj_ab(h_ctx, mod[i], w_in, ab_b_in[j], **ctx_grp)
            to_seq = lambda a, n: a.reshape(bsz, n, a.shape[-1])
            conv_args = (conv_w[j], conv_b[j], conv_ln_g[j], conv_ln_b[j])
            conv_lat = _conv_module(to_seq(g_lat, seq), *conv_args).reshape(t_lat, CONV_DIM)
            k_ctx, v_ctx = to_seq(k_ctx, n_ctx), to_seq(v_ctx, n_ctx)
            row_off, types = _na_geometry(rows)
            bias = _na_bias_tables(na_rpb[j], row_off)
            na_lat = _neighbourhood_attention(to_seq(q_lat, seq), to_seq(k_lat, seq), to_seq(v_lat, seq), k_ctx,
                                              v_ctx, bias, types).reshape(t_lat, NA_DIM)
            w_out = ab_w_out[j].astype(BF16)
            w_outs = [w_out[:CONV_DIM], w_out[CONV_DIM:]]
            b_out = ab_b_out[j]
            acts_lat = [conv_lat, na_lat]
            if need_ctx:
                conv_ctx = _conv_module(to_seq(g_ctx, n_ctx), *conv_args).reshape(t_ctx, CONV_DIM)
                na_ctx = _context_attention(to_seq(q_ctx, n_ctx), k_ctx, v_ctx).reshape(t_ctx, NA_DIM)
                acts_ctx = [conv_ctx, na_ctx]
        else:
            assert not need_ctx
            qn = C_HEADS * C_HEAD_DIM
            w_in = gqa_w_in[j].astype(BF16)
            cos, sin = _rope_tables(seq)
            q_lat, k_lat, v_lat = _inproj_gqa(h_lat, mod[i], w_in, cos, sin, seq)
            k_ctx, v_ctx = _inproj_kv(h_ctx, mod[i], w_in[:, qn:], **ctx_grp)
            to_seq = lambda a, n: a.reshape(bsz, n, a.shape[-1])
            att = _window_attention(to_seq(q_lat, seq), to_seq(k_lat, seq), to_seq(v_lat, seq),
                                    to_seq(k_ctx, n_ctx), to_seq(v_ctx, n_ctx), gqa_sink[j])
            acts_lat = [att.reshape(t_lat, qn)]
            w_outs = [gqa_w_out[j].astype(BF16)]
            b_out = jnp.zeros((d,), F32)
        ln1 = (ln_g[i, 0], ln_b[i, 0])
        ln2 = (ln_g[i, 1], ln_b[i, 1])
        h1, t_act, gates, gid = _post_mix(acts_lat, w_outs, b_out, h_lat, mod[i], *ln1, router, alpha=alpha, **lat_grp)
        h_lat = _moe_ln(t_act, gates, gid, h1, mod[i], wg, wu, wd, *ln2, alpha=alpha, **lat_grp)
        if need_ctx:
            h1, t_act, gates, gid = _post_mix(acts_ctx, w_outs, b_out, h_ctx, mod[i], *ln1, router, alpha=alpha,
                                         **ctx_grp)
            h_ctx = _moe_ln(t_act, gates, gid, h1, mod[i], wg, wu, wd, *ln2, alpha=alpha, **ctx_grp)
    return h_lat.reshape(bsz, seq, d)
```

```python
import functools

import numpy as np
import jax
import jax.numpy as jnp
from jax import lax
from jax.experimental import pallas as pl
from jax.experimental.pallas import tpu as pltpu

F32 = jnp.float32
BF16 = jnp.bfloat16

DEPTH = 2
GRID_W = 64
CONV_DIM = 512
CONV_WIDTH = 31
NA_HEADS = 8
NA_HEAD_DIM = 64
NA_DIM = NA_HEADS * NA_HEAD_DIM
NA_KH = 8
NA_KW = 16
C_HEADS = 16
C_KV_HEADS = 4
C_GROUP = C_HEADS // C_KV_HEADS
C_HEAD_DIM = 64
C_WINDOW = 128
C_BLOCK = 128
ROPE_BASE = 10000.0
N_GROUPS = 4
EXPERTS_PER_GROUP = 4
N_EXPERTS = N_GROUPS * EXPERTS_PER_GROUP
D_EXPERT = 256
LN_EPS = 1e-5
NEG_INF = -1e30
LOG2E = 1.4426950408889634

LANES = 128
SUBLANES = 8
VMEM_LIMIT_BYTES = 56 * 1024 * 1024

TOKEN_TILE = 512
MOE_TILE = 1024
MOE_CHUNK = 256
MOE_SLOTS = MOE_TILE // MOE_CHUNK + N_GROUPS - 1
MOE_BASE_CHUNKS = MOE_TILE // MOE_CHUNK + 1
MOD_ROWS = 24
NA_ROWS_PER_STEP = 4
NA_UNION_ROWS = NA_KH + NA_ROWS_PER_STEP
CONV_CHUNK = 256
CONV_HALO = 16
ROUTER_LANES = LANES
GATE_LANE0 = N_GROUPS
ROUTER_ROWS = 24


def _cparams(semantics):
    return pltpu.CompilerParams(dimension_semantics=semantics, vmem_limit_bytes=VMEM_LIMIT_BYTES)


def _dot(a, b):
    return jnp.dot(a, b, preferred_element_type=F32)


def _dot_nt(a, b):
    return lax.dot_general(a, b, (((1,), (1,)), ((), ())), preferred_element_type=F32)


def _dot_tn(a, b):
    return lax.dot_general(a, b, (((0,), (0,)), ((), ())), preferred_element_type=F32)


def _split_bf16(a):
    hi = a.astype(BF16)
    lo = (a - hi.astype(F32)).astype(BF16)
    return hi, lo


def _dot3(a, b):
    a_hi, a_lo = _split_bf16(a)
    b_hi, b_lo = _split_bf16(b)
    return _dot(a_hi, b_hi) + (_dot(a_lo, b_hi) + _dot(a_hi, b_lo))


def _layer_norm(x, g, b):
    mu = jnp.mean(x, axis=-1, keepdims=True)
    xc = x - mu
    var = jnp.mean(xc * xc, axis=-1, keepdims=True)
    return xc * lax.rsqrt(var + LN_EPS) * g + b


def _silu(x):
    return x * jax.nn.sigmoid(x)


def _mod_kernel(cc_ref, w_ref, b_ref, o_ref):
    o_ref[0] = _dot3(_silu(cc_ref[...]), w_ref[0]) + b_ref[0]


def _modulation(cc, ada_w, ada_b):
    depth, d, n = ada_w.shape
    tn = n // 4
    return pl.pallas_call(
        _mod_kernel,
        grid=(depth, n // tn),
        in_specs=[
            pl.BlockSpec((MOD_ROWS, d), lambda i, j: (0, 0)),
            pl.BlockSpec((1, d, tn), lambda i, j: (i, 0, j)),
            pl.BlockSpec((1, 1, tn), lambda i, j: (i, 0, j)),
        ],
        out_specs=pl.BlockSpec((1, MOD_ROWS, tn), lambda i, j: (i, 0, j)),
        out_shape=jax.ShapeDtypeStruct((depth, MOD_ROWS, n), F32),
        compiler_params=_cparams(("arbitrary", "arbitrary")),
        name="adaln_mod",
    )(cc, ada_w, ada_b.reshape(depth, 1, n))


def _mod_spec(rows_per_group, group0, tm):
    return lambda i: (group0 + (i * tm) // rows_per_group, 0, 0)


def _modulate(h, m, shift_row, scale_row):
    return h * (1.0 + m[scale_row:scale_row + 1]) + m[shift_row:shift_row + 1]


def _inproj_ab_kernel(h_ref, mod_ref, w_ref, b_ref, g_ref, q_ref, k_ref, v_ref):
    u = _modulate(h_ref[...], mod_ref[0], 0, 1).astype(BF16)
    c = CONV_DIM
    za = _dot(u, w_ref[:, 0:c]) + b_ref[:, 0:c]
    zb = _dot(u, w_ref[:, c:2 * c]) + b_ref[:, c:2 * c]
    g_ref[...] = za * jax.nn.sigmoid(zb)
    q0 = 2 * c
    zq = _dot(u, w_ref[:, q0:q0 + NA_DIM]) + b_ref[:, q0:q0 + NA_DIM]
    q_ref[...] = (zq * (NA_HEAD_DIM ** -0.5 * LOG2E)).astype(BF16)
    k0 = q0 + NA_DIM
    k_ref[...] = (_dot(u, w_ref[:, k0:k0 + NA_DIM]) + b_ref[:, k0:k0 + NA_DIM]).astype(BF16)
    v0 = k0 + NA_DIM
    v_ref[...] = (_dot(u, w_ref[:, v0:v0 + NA_DIM]) + b_ref[:, v0:v0 + NA_DIM]).astype(BF16)


def _inproj_ab(h, mod, w, b, rows_per_group, group0):
    t, d = h.shape
    n = w.shape[1]
    tm = TOKEN_TILE
    row = lambda i: (i, 0)
    return pl.pallas_call(
        _inproj_ab_kernel,
        grid=(t // tm,),
        in_specs=[
            pl.BlockSpec((tm, d), row),
            pl.BlockSpec((1, 6, d), _mod_spec(rows_per_group, group0, tm)),
            pl.BlockSpec((d, n), lambda i: (0, 0)),
            pl.BlockSpec((1, n), lambda i: (0, 0)),
        ],
        out_specs=[
            pl.BlockSpec((tm, CONV_DIM), row),
            pl.BlockSpec((tm, NA_DIM), row),
            pl.BlockSpec((tm, NA_DIM), row),
            pl.BlockSpec((tm, NA_DIM), row),
        ],
        out_shape=[
            jax.ShapeDtypeStruct((t, CONV_DIM), F32),
            jax.ShapeDtypeStruct((t, NA_DIM), BF16),
            jax.ShapeDtypeStruct((t, NA_DIM), BF16),
            jax.ShapeDtypeStruct((t, NA_DIM), BF16),
        ],
        compiler_params=_cparams(("parallel",)),
        name="inproj_conv_na",
    )(h, mod, w, b.reshape(1, n))


def _conv_kernel(g_ref, w_ref, cb_ref, lg_ref, lb_ref, o_ref, pad_ref, *, seq):
    zeros = jnp.zeros((CONV_HALO, CONV_DIM), F32)
    pad_ref[0:CONV_HALO, :] = zeros
    pad_ref[CONV_HALO + seq:2 * CONV_HALO + seq, :] = zeros
    pad_ref[CONV_HALO:CONV_HALO + seq, :] = g_ref[0]
    first = CONV_HALO - CONV_WIDTH // 2
    ext = CONV_CHUNK + SUBLANES

    def chunk(i, carry):
        r0 = pl.multiple_of(i * CONV_CHUNK, CONV_CHUNK)
        acc = jnp.zeros((CONV_CHUNK, CONV_DIM), F32) + cb_ref[...]
        for res in range(SUBLANES):
            part = None
            for base in range(0, first + CONV_WIDTH, SUBLANES):
                tap = base + res - first
                if 0 <= tap < CONV_WIDTH:
                    term = pad_ref[pl.ds(r0 + base, ext), :] * w_ref[tap:tap + 1, :]
                    part = term if part is None else part + term
            acc = acc + part[res:res + CONV_CHUNK]
        y = _layer_norm(acc, lg_ref[...], lb_ref[...])
        o_ref[0, pl.ds(r0, CONV_CHUNK), :] = _silu(y).astype(BF16)
        return carry

    lax.fori_loop(0, seq // CONV_CHUNK, chunk, 0)


def _conv_module(g, conv_w, conv_b, ln_g, ln_b):
    bsz, seq, c = g.shape
    vec = lambda i: (0, 0)
    return pl.pallas_call(
        functools.partial(_conv_kernel, seq=seq),
        grid=(bsz,),
        in_specs=[
            pl.BlockSpec((1, seq, c), lambda i: (i, 0, 0)),
            pl.BlockSpec((CONV_WIDTH, c), vec),
            pl.BlockSpec((1, c), vec),
            pl.BlockSpec((1, c), vec),
            pl.BlockSpec((1, c), vec),
        ],
        out_specs=pl.BlockSpec((1, seq, c), lambda i: (i, 0, 0)),
        out_shape=jax.ShapeDtypeStruct((bsz, seq, c), BF16),
        scratch_shapes=[pltpu.VMEM((seq + 2 * CONV_HALO, c), F32)],
        compiler_params=_cparams(("parallel",)),
        name="conv_module",
    )(g, conv_w, conv_b.reshape(1, c), ln_g.reshape(1, c), ln_b.reshape(1, c))


N_ROW_OFFS = 2 * NA_KH - 1
N_COL_OFFS = 2 * NA_KW - 1


def _na_geometry(rows):
    rq, ru = NA_ROWS_PER_STEP, NA_UNION_ROWS
    geoms, types = [], []
    for step in range(rows // rq):
        r = step * rq + np.arange(rq)[:, None]
        key_row = np.clip(step * rq - NA_KH // 2, 0, rows - ru) + np.arange(ru)[None, :]
        row_start = np.clip(r - NA_KH // 2, 0, rows - NA_KH)
        row_in = (key_row >= row_start) & (key_row < row_start + NA_KH)
        geom = np.where(row_in, key_row - r + NA_KH - 1, N_ROW_OFFS).astype(np.int32)
        for t, other in enumerate(geoms):
            if np.array_equal(other, geom):
                types.append(t)
                break
        else:
            types.append(len(geoms))
            geoms.append(geom)
    return np.stack(geoms), np.asarray(types, np.int32)


def _na_bias_kernel(row_off_ref, rpb_ref, o_ref, tile_ref):
    head = pl.program_id(0)
    qc = lax.broadcasted_iota(jnp.int32, (GRID_W, GRID_W), 0)
    kc = lax.broadcasted_iota(jnp.int32, (GRID_W, GRID_W), 1)
    col_start = jnp.clip(qc - NA_KW // 2, 0, GRID_W - NA_KW)
    col_in = (kc >= col_start) & (kc < col_start + NA_KW)
    col_off = jnp.clip(kc - qc, -(NA_KW - 1), NA_KW - 1) + NA_KW - 1
    for a in range(N_ROW_OFFS):
        tile = jnp.zeros((GRID_W, GRID_W), F32)
        for b in range(N_COL_OFFS):
            tile = jnp.where(col_off == b, rpb_ref[(head * N_ROW_OFFS + a) * N_COL_OFFS + b], tile)
        tile_ref[a] = jnp.where(col_in, tile * LOG2E, NEG_INF)
    tile_ref[N_ROW_OFFS] = jnp.full((GRID_W, GRID_W), NEG_INF, F32)
    for typ in range(o_ref.shape[0]):
        for qr in range(NA_ROWS_PER_STEP):
            for kr in range(NA_UNION_ROWS):
                a = row_off_ref[(typ * NA_ROWS_PER_STEP + qr) * NA_UNION_ROWS + kr]
                o_ref[typ, 0, qr * GRID_W:(qr + 1) * GRID_W, kr * GRID_W:(kr + 1) * GRID_W] = tile_ref[a]


def _na_bias_tables(rpb, row_off):
    n_types = row_off.shape[0]
    tq = NA_ROWS_PER_STEP * GRID_W
    n_win = NA_UNION_ROWS * GRID_W
    return pl.pallas_call(
        _na_bias_kernel,
        grid=(NA_HEADS,),
        in_specs=[pl.BlockSpec(memory_space=pltpu.SMEM), pl.BlockSpec(memory_space=pltpu.SMEM)],
        out_specs=pl.BlockSpec((n_types, 1, tq, n_win), lambda h: (0, h, 0, 0)),
        out_shape=jax.ShapeDtypeStruct((n_types, NA_HEADS, tq, n_win), F32),
        scratch_shapes=[pltpu.VMEM((N_ROW_OFFS + 1, GRID_W, GRID_W), F32)],
        compiler_params=_cparams(("parallel",)),
        name="na_bias_table",
    )(jnp.asarray(row_off.reshape(-1)), rpb.astype(F32).reshape(-1))


def _softmax_rows(scores):
    m = functools.reduce(jnp.maximum, [jnp.max(s, axis=1, keepdims=True) for s in scores])
    ps = [jnp.exp2(s - m) for s in scores]
    denom = functools.reduce(jnp.add, [jnp.sum(p, axis=1, keepdims=True) for p in ps])
    return [p.astype(BF16) for p in ps], denom


def _pv_rows(ps, denom, values):
    return functools.reduce(jnp.add, [_dot(p, v) for p, v in zip(ps, values)]) / denom


def _na_kernel(type_ref, q_ref, k_ref, v_ref, kc_ref, vc_ref, bias_ref, o_ref, *, rows):
    del type_ref
    step = pl.program_id(1)
    start_row = jnp.clip(step * NA_ROWS_PER_STEP - NA_KH // 2, 0, rows - NA_UNION_ROWS)
    start = pl.multiple_of(start_row * GRID_W, GRID_W)
    n_win = NA_UNION_ROWS * GRID_W
    heads = [slice(h * NA_HEAD_DIM, (h + 1) * NA_HEAD_DIM) for h in range(NA_HEADS)]
    scores = []
    for h, sl in enumerate(heads):
        keys = jnp.concatenate([k_ref[0, pl.ds(start, n_win), sl], kc_ref[0, :, sl]], axis=0)
        s = _dot_nt(q_ref[0, :, sl], keys)
        scores.append([jnp.concatenate([s[:, :n_win] + bias_ref[0, h], s[:, n_win:]], axis=1)])
    probs = [_softmax_rows(s) for s in scores]
    for sl, (ps, denom) in zip(heads, probs):
        values = jnp.concatenate([v_ref[0, pl.ds(start, n_win), sl], vc_ref[0, :, sl]], axis=0)
        o_ref[0, :, sl] = _pv_rows(ps, denom, [values]).astype(BF16)


def _neighbourhood_attention(q, k, v, k_ctx, v_ctx, bias, types):
    bsz, seq, dim = q.shape
    n_ctx = k_ctx.shape[1]
    rows = seq // GRID_W
    tq = NA_ROWS_PER_STEP * GRID_W
    n_win = NA_UNION_ROWS * GRID_W
    full = lambda b, s, t: (b, 0, 0)
    grid_spec = pltpu.PrefetchScalarGridSpec(
        num_scalar_prefetch=1,
        grid=(bsz, seq // tq),
        in_specs=[
            pl.BlockSpec((1, tq, dim), lambda b, s, t: (b, s, 0)),
            pl.BlockSpec((1, seq, dim), full),
            pl.BlockSpec((1, seq, dim), full),
            pl.BlockSpec((1, n_ctx, dim), full),
            pl.BlockSpec((1, n_ctx, dim), full),
            pl.BlockSpec((1, NA_HEADS, tq, n_win), lambda b, s, t: (t[s], 0, 0, 0)),
        ],
        out_specs=pl.BlockSpec((1, tq, dim), lambda b, s, t: (b, s, 0)),
    )
    return pl.pallas_call(
        functools.partial(_na_kernel, rows=rows),
        grid_spec=grid_spec,
        out_shape=jax.ShapeDtypeStruct((bsz, seq, dim), BF16),
        compiler_params=_cparams(("parallel", "arbitrary")),
        name="neighbourhood_attn",
    )(jnp.asarray(types), q, k, v, k_ctx, v_ctx, bias)


def _ctx_attn_kernel(q_ref, k_ref, v_ref, o_ref):
    for h in range(NA_HEADS):
        sl = slice(h * NA_HEAD_DIM, (h + 1) * NA_HEAD_DIM)
        ps, denom = _softmax_rows([_dot_nt(q_ref[0, :, sl], k_ref[0, :, sl])])
        o_ref[0, :, sl] = _pv_rows(ps, denom, [v_ref[0, :, sl]]).astype(BF16)


def _context_attention(q, k, v):
    bsz, n, dim = q.shape
    spec = pl.BlockSpec((1, n, dim), lambda b: (b, 0, 0))
    return pl.pallas_call(
        _ctx_attn_kernel,
        grid=(bsz,),
        in_specs=[spec, spec, spec],
        out_specs=spec,
        out_shape=jax.ShapeDtypeStruct((bsz, n, dim), BF16),
        compiler_params=_cparams(("parallel",)),
        name="context_attn",
    )(q, k, v)


def _rope(x, cos, sin_signed):
    n = x.shape[1]
    half = C_HEAD_DIM // 4
    lane = lax.broadcasted_iota(jnp.int32, x.shape, 1)
    partner = jnp.where(lane % (2 * half) < half, pltpu.roll(x, n - half, 1), pltpu.roll(x, half, 1))
    reps = n // LANES
    cos_full = jnp.concatenate([cos] * reps, axis=1)
    sin_full = jnp.concatenate([sin_signed] * reps, axis=1)
    return x * cos_full + partner * sin_full


def _inproj_gqa_kernel(h_ref, mod_ref, w_ref, cos_ref, sin_ref, q_ref, k_ref, v_ref):
    u = _modulate(h_ref[...], mod_ref[0], 0, 1).astype(BF16)
    qn = C_HEADS * C_HEAD_DIM
    kn = C_KV_HEADS * C_HEAD_DIM
    cos, sin = cos_ref[...], sin_ref[...]
    q = _rope(_dot(u, w_ref[:, 0:qn]), cos, sin)
    q_ref[...] = (q * (C_HEAD_DIM ** -0.5 * LOG2E)).astype(BF16)
    k_ref[...] = _rope(_dot(u, w_ref[:, qn:qn + kn]), cos, sin).astype(BF16)
    v_ref[...] = _dot(u, w_ref[:, qn + kn:qn + 2 * kn]).astype(BF16)


def _inproj_gqa(h, mod, w, cos, sin, seq):
    t, d = h.shape
    n = w.shape[1]
    tm = TOKEN_TILE
    qn = C_HEADS * C_HEAD_DIM
    kn = C_KV_HEADS * C_HEAD_DIM
    row = lambda i: (i, 0)
    pos = lambda i: (i % (seq // tm), 0)
    return pl.pallas_call(
        _inproj_gqa_kernel,
        grid=(t // tm,),
        in_specs=[
            pl.BlockSpec((tm, d), row),
            pl.BlockSpec((1, 6, d), _mod_spec(seq, 0, tm)),
            pl.BlockSpec((d, n), lambda i: (0, 0)),
            pl.BlockSpec((tm, LANES), pos),
            pl.BlockSpec((tm, LANES), pos),
        ],
        out_specs=[pl.BlockSpec((tm, qn), row), pl.BlockSpec((tm, kn), row), pl.BlockSpec((tm, kn), row)],
        out_shape=[
            jax.ShapeDtypeStruct((t, qn), BF16),
            jax.ShapeDtypeStruct((t, kn), BF16),
            jax.ShapeDtypeStruct((t, kn), BF16),
        ],
        compiler_params=_cparams(("parallel",)),
        name="inproj_gqa",
    )(h, mod, w, cos, sin)


def _inproj_kv_kernel(h_ref, mod_ref, w_ref, k_ref, v_ref):
    u = _modulate(h_ref[...], mod_ref[0], 0, 1).astype(BF16)
    kn = C_KV_HEADS * C_HEAD_DIM
    k_ref[...] = _dot(u, w_ref[:, 0:kn]).astype(BF16)
    v_ref[...] = _dot(u, w_ref[:, kn:2 * kn]).astype(BF16)


def _inproj_kv(h, mod, w, rows_per_group, group0):
    t, d = h.shape
    n = w.shape[1]
    tm = TOKEN_TILE
    kn = C_KV_HEADS * C_HEAD_DIM
    row = lambda i: (i, 0)
    return pl.pallas_call(
        _inproj_kv_kernel,
        grid=(t // tm,),
        in_specs=[
            pl.BlockSpec((tm, d), row),
            pl.BlockSpec((1, 6, d), _mod_spec(rows_per_group, group0, tm)),
            pl.BlockSpec((d, n), lambda i: (0, 0)),
        ],
        out_specs=[pl.BlockSpec((tm, kn), row), pl.BlockSpec((tm, kn), row)],
        out_shape=[jax.ShapeDtypeStruct((t, kn), BF16), jax.ShapeDtypeStruct((t, kn), BF16)],
        compiler_params=_cparams(("parallel",)),
        name="inproj_ctx_kv",
    )(h, mod, w)


def _rope_tables(seq):
    t = jnp.arange(seq)
    row = (t // GRID_W).astype(F32)
    col = (t % GRID_W).astype(F32)
    axis_dim = C_HEAD_DIM // 2
    inv_freq = ROPE_BASE ** (-jnp.arange(0, axis_dim, 2, dtype=F32) / axis_dim)
    ang_r = row[:, None] * inv_freq
    ang_c = col[:, None] * inv_freq
    cos = jnp.concatenate([jnp.cos(ang_r)] * 2 + [jnp.cos(ang_c)] * 2, axis=-1)
    sin = jnp.concatenate([-jnp.sin(ang_r), jnp.sin(ang_r), -jnp.sin(ang_c), jnp.sin(ang_c)], axis=-1)
    reps = LANES // C_HEAD_DIM
    return jnp.tile(cos, (1, reps)), jnp.tile(sin, (1, reps))


def _gqa_kernel(sink_ref, q_ref, k_ref, v_ref, kc_ref, vc_ref, o_ref, *, seq):
    blk = pl.program_id(1)
    span = C_BLOCK + 2 * C_WINDOW
    start = pl.multiple_of(jnp.clip(blk * C_BLOCK - C_WINDOW, 0, seq - span), C_BLOCK)
    k_pos = start + lax.broadcasted_iota(jnp.int32, (span, C_BLOCK), 0)
    q_pos = blk * C_BLOCK + lax.broadcasted_iota(jnp.int32, (span, C_BLOCK), 1)
    valid = jnp.abs(q_pos - k_pos) <= C_WINDOW
    valid = jnp.concatenate([valid] * C_GROUP, axis=1)
    d = C_HEAD_DIM

    def scores(hk):
        q0 = hk * C_GROUP * d
        qs = jnp.concatenate([q_ref[0, :, q0 + g * d:q0 + (g + 1) * d] for g in range(C_GROUP)], axis=0)
        ksl = slice(hk * d, (hk + 1) * d)
        keys = jnp.concatenate([k_ref[0, pl.ds(start, span), ksl], kc_ref[0, :, ksl]], axis=0)
        s = _dot_nt(keys, qs)
        return jnp.where(valid, s[:span], NEG_INF), s[span:]

    def softmax(hk, s):
        s_lat, s_ctx = s
        sink = jnp.concatenate(
            [jnp.full((1, C_BLOCK), sink_ref[hk * C_GROUP + g] * LOG2E, F32) for g in range(C_GROUP)], axis=1)
        m = jnp.maximum(jnp.maximum(jnp.max(s_lat, axis=0, keepdims=True), jnp.max(s_ctx, axis=0, keepdims=True)),
                        sink)
        p_lat = jnp.exp2(s_lat - m)
        p_ctx = jnp.exp2(s_ctx - m)
        denom = jnp.sum(p_lat, axis=0, keepdims=True) + jnp.sum(p_ctx, axis=0, keepdims=True) + jnp.exp2(sink - m)
        inv = 1.0 / denom
        return (p_lat * inv).astype(BF16), (p_ctx * inv).astype(BF16)

    def values(hk, p):
        ksl = slice(hk * d, (hk + 1) * d)
        q0 = hk * C_GROUP * d
        vals = jnp.concatenate([v_ref[0, pl.ds(start, span), ksl], vc_ref[0, :, ksl]], axis=0)
        out = _dot_tn(jnp.concatenate([p[0], p[1]], axis=0), vals)
        for g in range(C_GROUP):
            o_ref[0, :, q0 + g * d:q0 + (g + 1) * d] = out[g * C_BLOCK:(g + 1) * C_BLOCK].astype(BF16)

    s = {0: scores(0), 1: scores(1)}
    p = {0: softmax(0, s[0])}
    for hk in range(C_KV_HEADS):
        if hk + 2 < C_KV_HEADS:
            s[hk + 2] = scores(hk + 2)
        if hk + 1 < C_KV_HEADS:
            p[hk + 1] = softmax(hk + 1, s[hk + 1])
        values(hk, p[hk])


def _window_attention(q, k, v, k_ctx, v_ctx, sink):
    bsz, seq, qn = q.shape
    kn = k.shape[2]
    n_ctx = k_ctx.shape[1]
    full = lambda b, s: (b, 0, 0)
    return pl.pallas_call(
        functools.partial(_gqa_kernel, seq=seq),
        grid=(bsz, seq // C_BLOCK),
        in_specs=[
            pl.BlockSpec(memory_space=pltpu.SMEM),
            pl.BlockSpec((1, C_BLOCK, qn), lambda b, s: (b, s, 0)),
            pl.BlockSpec((1, seq, kn), full),
            pl.BlockSpec((1, seq, kn), full),
            pl.BlockSpec((1, n_ctx, kn), full),
            pl.BlockSpec((1, n_ctx, kn), full),
        ],
        out_specs=pl.BlockSpec((1, C_BLOCK, qn), lambda b, s: (b, s, 0)),
        out_shape=jax.ShapeDtypeStruct((bsz, seq, qn), BF16),
        compiler_params=_cparams(("parallel", "arbitrary")),
        name="window_gqa_attn",
    )(sink, q, k, v, k_ctx, v_ctx)


def _route(logits):
    row = lax.broadcasted_iota(jnp.int32, logits.shape, 0)
    big = jnp.int32(ROUTER_ROWS)
    lg = jnp.where(row < N_GROUPS, logits, NEG_INF)
    g_max = jnp.max(lg, axis=0, keepdims=True)
    g_prob = 1.0 / jnp.sum(jnp.exp(lg - g_max), axis=0, keepdims=True)
    g_idx = jnp.min(jnp.where(lg == g_max, row, big), axis=0, keepdims=True)
    first = GATE_LANE0 + g_idx * EXPERTS_PER_GROUP
    in_group = (row >= first) & (row < first + EXPERTS_PER_GROUP)
    le = jnp.where(in_group, logits, NEG_INF)
    e1 = jnp.max(le, axis=0, keepdims=True)
    i1 = jnp.min(jnp.where(le == e1, row, big), axis=0, keepdims=True)
    le2 = jnp.where(row == i1, NEG_INF, le)
    e2 = jnp.max(le2, axis=0, keepdims=True)
    i2 = jnp.min(jnp.where(le2 == e2, row, big), axis=0, keepdims=True)
    r = jnp.exp(e2 - e1)
    w1 = g_prob / (1.0 + r)
    w2 = w1 * r
    own = [jnp.where(i1 == first + j, w1, jnp.where(i2 == first + j, w2, 0.0)) for j in range(EXPERTS_PER_GROUP)]
    return g_idx, own


def _post_mix_kernel(*refs, n_in, alpha):
    a_refs = refs[:n_in]
    w_refs = refs[n_in:2 * n_in]
    b_ref, h_ref, mod_ref, lg_ref, lb_ref, r_ref, h1_ref, t_ref, gate_ref, gid_ref = refs[2 * n_in:]
    o = functools.reduce(jnp.add, [_dot(a[...], w[...]) for a, w in zip(a_refs, w_refs)]) + b_ref[...]
    m = mod_ref[0]
    h1 = _layer_norm(alpha * h_ref[...] + m[2:3] * o, lg_ref[...], lb_ref[...])
    h1_ref[...] = h1
    t = _modulate(h1, m, 3, 4).astype(BF16)
    t_ref[...] = t
    group, own = _route(_dot_nt(r_ref[...], t))
    parts = _split3(jnp.concatenate(own, axis=0))
    rows = jnp.concatenate([p.astype(F32) for p in parts], axis=0)
    pad = jnp.zeros((ROUTER_LANES - rows.shape[0], rows.shape[1]), F32)
    gate_ref[...] = jnp.transpose(jnp.concatenate([rows, pad], axis=0)).astype(BF16)
    row = lax.broadcasted_iota(jnp.int32, gid_ref.shape[1:], 0)
    count = jnp.sum((row - 1 == group).astype(F32), axis=1, keepdims=True).astype(jnp.int32)
    gid_ref[0] = jnp.where(row == 0, group, count)


def _post_mix(acts, weights, bias, h, mod, ln_g, ln_b, router, rows_per_group, group0, alpha):
    t, d = h.shape
    tm = TOKEN_TILE
    row = lambda i: (i, 0)
    const = lambda i: (0, 0)
    in_specs = [pl.BlockSpec((tm, a.shape[1]), row) for a in acts]
    in_specs += [pl.BlockSpec(w.shape, const) for w in weights]
    in_specs += [
        pl.BlockSpec((1, d), const),
        pl.BlockSpec((tm, d), row),
        pl.BlockSpec((1, 6, d), _mod_spec(rows_per_group, group0, tm)),
        pl.BlockSpec((1, d), const),
        pl.BlockSpec((1, d), const),
        pl.BlockSpec((ROUTER_ROWS, d), const),
    ]
    return pl.pallas_call(
        functools.partial(_post_mix_kernel, n_in=len(acts), alpha=alpha),
        grid=(t // tm,),
        in_specs=in_specs,
        out_specs=[pl.BlockSpec((tm, d), row), pl.BlockSpec((tm, d), row), pl.BlockSpec((tm, ROUTER_LANES), row),
                   pl.BlockSpec((1, SUBLANES, tm), lambda i: (i, 0, 0))],
        out_shape=[
            jax.ShapeDtypeStruct((t, d), F32),
            jax.ShapeDtypeStruct((t, d), BF16),
            jax.ShapeDtypeStruct((t, ROUTER_LANES), BF16),
            jax.ShapeDtypeStruct((t // tm, SUBLANES, tm), jnp.int32),
        ],
        compiler_params=_cparams(("parallel",)),
        name="outproj_ln_router",
    )(*acts, *weights, bias.reshape(1, d), h, mod, ln_g.reshape(1, d), ln_b.reshape(1, d), router)


def _split3(a):
    hi = a.astype(BF16)
    rest = a - hi.astype(F32)
    mid = rest.astype(BF16)
    return hi, mid, (rest - mid.astype(F32)).astype(BF16)


def _expert_group(x, gate_cols, wg_ref, wu_ref, wd_ref, hid_ref):
    f = D_EXPERT
    for j, gate in enumerate(gate_cols):
        hid = _silu(_dot(x, wg_ref[j])) * _dot(x, wu_ref[j]) * gate
        hid_ref[:, f * j:f * (j + 1)] = hid.astype(BF16)
    return _dot(hid_ref[...], wd_ref[0])


def _expert_weights(w_gate, w_up, w_down):
    e, f, d = w_down.shape
    return (w_gate.astype(BF16), w_up.astype(BF16),
            w_down.astype(BF16).reshape(N_GROUPS, EXPERTS_PER_GROUP * f, d))


def _moe_plan(counts):
    chunks = (counts + MOE_CHUNK - 1) // MOE_CHUNK
    ends = jnp.cumsum(chunks, axis=1)
    slot = jnp.arange(MOE_SLOTS, dtype=jnp.int32)
    chunk_group = jnp.minimum(jnp.sum(ends[:, None, :] <= slot[None, :, None], axis=2), N_GROUPS - 1)
    start_row = (ends - chunks) * MOE_CHUNK
    return (chunk_group.astype(jnp.int32).reshape(-1), ends[:, -1].astype(jnp.int32),
            start_row.astype(jnp.int32).reshape(-1))


def _moe_kernel(chunk_group_ref, n_chunks_ref, start_ref, t_ref, gate_ref, gid_ref, h_ref, mod_ref, wg_ref,
                wu_ref, wd_ref, lg_ref, lb_ref, o_ref, perm_ref, xp_ref, gp_ref, yp_ref, hid_ref, *, alpha):
    tile = pl.program_id(0)
    step = pl.program_id(1)
    tm = t_ref.shape[0]

    @pl.when(step == 0)
    def _():
        gid = jnp.concatenate([gid_ref[i, 0:1, :] for i in range(gid_ref.shape[0])], axis=1)
        is_grp = lax.broadcasted_iota(jnp.int32, (SUBLANES, tm), 0) == gid
        upper = (lax.broadcasted_iota(jnp.int32, (tm, tm), 0) <= lax.broadcasted_iota(jnp.int32, (tm, tm), 1))
        count = _dot(is_grp.astype(BF16), upper.astype(BF16))
        rank = jnp.sum(jnp.where(is_grp, count, 0.0), axis=0, keepdims=True).astype(jnp.int32) - 1
        start = jnp.zeros_like(gid)
        for g in range(N_GROUPS):
            start = jnp.where(gid == g, start_ref[tile * N_GROUPS + g], start)
        dest = start + rank
        perm_ref[...] = (lax.broadcasted_iota(jnp.int32, (perm_ref.shape[0], tm), 0) == dest).astype(BF16)

        def move(rows):
            perm = perm_ref[rows, :]
            xp_ref[rows, :] = _dot(perm, t_ref[...]).astype(BF16)
            gp_ref[rows, :] = _dot(perm, gate_ref[...])

        move(slice(0, MOE_BASE_CHUNKS * MOE_CHUNK))
        for s in range(MOE_BASE_CHUNKS, MOE_SLOTS):
            pl.when(s < n_chunks_ref[tile])(functools.partial(move, slice(s * MOE_CHUNK, (s + 1) * MOE_CHUNK)))

    @pl.when(step < n_chunks_ref[tile])
    def _():
        rows = pl.ds(pl.multiple_of(step * MOE_CHUNK, MOE_CHUNK), MOE_CHUNK)
        gp = gp_ref[rows, :]
        e = EXPERTS_PER_GROUP
        cols = [gp[:, j:j + 1] + gp[:, e + j:e + j + 1] + gp[:, 2 * e + j:2 * e + j + 1] for j in range(e)]
        yp_ref[rows, :] = _expert_group(xp_ref[rows, :], cols, wg_ref, wu_ref, wd_ref, hid_ref).astype(BF16)

    @pl.when((step >= n_chunks_ref[tile]) & (step < MOE_BASE_CHUNKS))
    def _():
        rows = pl.ds(pl.multiple_of(step * MOE_CHUNK, MOE_CHUNK), MOE_CHUNK)
        yp_ref[rows, :] = jnp.zeros((MOE_CHUNK, yp_ref.shape[1]), BF16)

    @pl.when(step == MOE_SLOTS - 1)
    def _():
        base = slice(0, MOE_BASE_CHUNKS * MOE_CHUNK)
        o_ref[...] = _dot_tn(perm_ref[base, :], yp_ref[base, :])
        for s in range(MOE_BASE_CHUNKS, MOE_SLOTS):
            @pl.when(s < n_chunks_ref[tile])
            def _():
                rows = slice(s * MOE_CHUNK, (s + 1) * MOE_CHUNK)
                o_ref[...] += _dot_tn(perm_ref[rows, :], yp_ref[rows, :])
        o_ref[...] = _layer_norm(alpha * h_ref[...] + mod_ref[0][5:6] * o_ref[...], lg_ref[...], lb_ref[...])


def _moe_ln(t_act, gates, gid, h, mod, wg, wu, wd, ln_g, ln_b, rows_per_group, group0, alpha):
    t, d = h.shape
    tm = MOE_TILE
    f = wg.shape[2]
    rows = MOE_SLOTS * MOE_CHUNK
    row = lambda i, s, cg, nc, st: (i, 0)
    const = lambda i, s, cg, nc, st: (0, 0)
    group_w = lambda i, s, cg, nc, st: (cg[i * MOE_SLOTS + s], 0, 0)
    mod_idx = _mod_spec(rows_per_group, group0, tm)
    counts = gid[:, 1:1 + N_GROUPS, 0].reshape(t // tm, tm // TOKEN_TILE, N_GROUPS).sum(axis=1)
    plan = _moe_plan(counts)
    grid_spec = pltpu.PrefetchScalarGridSpec(
        num_scalar_prefetch=3,
        grid=(t // tm, MOE_SLOTS),
        in_specs=[
            pl.BlockSpec((tm, d), row),
            pl.BlockSpec((tm, ROUTER_LANES), row),
            pl.BlockSpec((tm // TOKEN_TILE, SUBLANES, TOKEN_TILE), lambda i, s, cg, nc, st: (i, 0, 0)),
            pl.BlockSpec((tm, d), row),
            pl.BlockSpec((1, 6, d), lambda i, s, cg, nc, st: mod_idx(i)),
            pl.BlockSpec((EXPERTS_PER_GROUP, d, f), group_w),
            pl.BlockSpec((EXPERTS_PER_GROUP, d, f), group_w),
            pl.BlockSpec((1,) + wd.shape[1:], group_w),
            pl.BlockSpec((1, d), const),
            pl.BlockSpec((1, d), const),
        ],
        out_specs=pl.BlockSpec((tm, d), row),
        scratch_shapes=[
            pltpu.VMEM((rows, tm), BF16),
            pltpu.VMEM((rows, d), BF16),
            pltpu.VMEM((rows, ROUTER_LANES), F32),
            pltpu.VMEM((rows, d), BF16),
            pltpu.VMEM((MOE_CHUNK, wd.shape[1]), BF16),
        ],
    )
    return pl.pallas_call(
        functools.partial(_moe_kernel, alpha=alpha),
        grid_spec=grid_spec,
        out_shape=jax.ShapeDtypeStruct((t, d), F32),
        compiler_params=_cparams(("parallel", "arbitrary")),
        name="moe_ln",
    )(*plan, t_act, gates, gid, h, mod, wg, wu, wd, ln_g.reshape(1, d), ln_b.reshape(1, d))


def _router_matrix(router_group, router_expert):
    d = router_group.shape[0]
    pad = jnp.zeros((d, ROUTER_ROWS - N_GROUPS - N_EXPERTS), F32)
    return jnp.concatenate([router_group, router_expert, pad], axis=1).T.astype(BF16)


def kernel(x, c, ctx, c_ctx, ada_w, ada_b, ln_g, ln_b, ab_w_in, ab_b_in, conv_w, conv_b, conv_ln_g, conv_ln_b,
           na_rpb, ab_w_out, ab_b_out, gqa_w_in, gqa_sink, gqa_w_out, router_group, router_expert, exp_w_gate,
           exp_w_up, exp_w_down):
    bsz, seq, d = x.shape
    n_ctx = ctx.shape[1]
    depth = ada_w.shape[0]
    assert depth == DEPTH and bsz + 1 <= MOD_ROWS
    assert seq % MOE_TILE == 0 and (bsz * n_ctx) % MOE_TILE == 0 and MOE_TILE % TOKEN_TILE == 0
    assert seq % C_BLOCK == 0 and seq % CONV_CHUNK == 0 and n_ctx % CONV_CHUNK == 0
    rows = seq // GRID_W
    assert rows % NA_ROWS_PER_STEP == 0 and rows >= NA_UNION_ROWS
    alpha = (2.0 * depth) ** 0.25
    t_lat, t_ctx = bsz * seq, bsz * n_ctx

    cc = jnp.concatenate([c, c_ctx[None], jnp.zeros((MOD_ROWS - bsz - 1, d), F32)], axis=0)
    mod = _modulation(cc, ada_w, ada_b).reshape(depth, MOD_ROWS, 6, d)
    lat_grp = dict(rows_per_group=seq, group0=0)
    ctx_grp = dict(rows_per_group=t_ctx, group0=bsz)

    h_lat = x.reshape(t_lat, d)
    h_ctx = ctx.reshape(t_ctx, d)
    for i in range(depth):
        j = i // 2
        need_ctx = i < depth - 1
        router = _router_matrix(router_group[i], router_expert[i])
        wg, wu, wd = _expert_weights(exp_w_gate[i], exp_w_up[i], exp_w_down[i])
        if i % 2 == 0:
            w_in = ab_w_in[j].astype(BF16)
            g_lat, q_lat, k_lat, v_lat = _inproj_ab(h_lat, mod[i], w_in, ab_b_in[j], **lat_grp)
            g_ctx, q_ctx, k_ctx, v_ctx = _inproj_ab(h_ctx, mod[i], w_in, ab_b_in[j], **ctx_grp)
            to_seq = lambda a, n: a.reshape(bsz, n, a.shape[-1])
            conv_args = (conv_w[j], conv_b[j], conv_ln_g[j], conv_ln_b[j])
            conv_lat = _conv_module(to_seq(g_lat, seq), *conv_args).reshape(t_lat, CONV_DIM)
            k_ctx, v_ctx = to_seq(k_ctx, n_ctx), to_seq(v_ctx, n_ctx)
            row_off, types = _na_geometry(rows)
            bias = _na_bias_tables(na_rpb[j], row_off)
            na_lat = _neighbourhood_attention(to_seq(q_lat, seq), to_seq(k_lat, seq), to_seq(v_lat, seq), k_ctx,
                                              v_ctx, bias, types).reshape(t_lat, NA_DIM)
            w_out = ab_w_out[j].astype(BF16)
            w_outs = [w_out[:CONV_DIM], w_out[CONV_DIM:]]
            b_out = ab_b_out[j]
            acts_lat = [conv_lat, na_lat]
            if need_ctx:
                conv_ctx = _conv_module(to_seq(g_ctx, n_ctx), *conv_args).reshape(t_ctx, CONV_DIM)
                na_ctx = _context_attention(to_seq(q_ctx, n_ctx), k_ctx, v_ctx).reshape(t_ctx, NA_DIM)
                acts_ctx = [conv_ctx, na_ctx]
        else:
            assert not need_ctx
            qn = C_HEADS * C_HEAD_DIM
            w_in = gqa_w_in[j].astype(BF16)
            cos, sin = _rope_tables(seq)
            q_lat, k_lat, v_lat = _inproj_gqa(h_lat, mod[i], w_in, cos, sin, seq)
            k_ctx, v_ctx = _inproj_kv(h_ctx, mod[i], w_in[:, qn:], **ctx_grp)
            to_seq = lambda a, n: a.reshape(bsz, n, a.shape[-1])
            att = _window_attention(to_seq(q_lat, seq), to_seq(k_lat, seq), to_seq(v_lat, seq),
                                    to_seq(k_ctx, n_ctx), to_seq(v_ctx, n_ctx), gqa_sink[j])
            acts_lat = [att.reshape(t_lat, qn)]
            w_outs = [gqa_w_out[j].astype(BF16)]
            b_out = jnp.zeros((d,), F32)
        ln1 = (ln_g[i, 0], ln_b[i, 0])
        ln2 = (ln_g[i, 1], ln_b[i, 1])
        h1, t_act, gates, gid = _post_mix(acts_lat, w_outs, b_out, h_lat, mod[i], *ln1, router, alpha=alpha, **lat_grp)
        h_lat = _moe_ln(t_act, gates, gid, h1, mod[i], wg, wu, wd, *ln2, alpha=alpha, **lat_grp)
        if need_ctx:
            h1, t_act, gates, gid = _post_mix(acts_ctx, w_outs, b_out, h_ctx, mod[i], *ln1, router, alpha=alpha,
                                         **ctx_grp)
            h_ctx = _moe_ln(t_act, gates, gid, h1, mod[i], wg, wu, wd, *ln2, alpha=alpha, **ctx_grp)
    return h_lat.reshape(bsz, seq, d)
```

```python
import functools

import numpy as np
import jax
import jax.numpy as jnp
from jax import lax
from jax.experimental import pallas as pl
from jax.experimental.pallas import tpu as pltpu

F32 = jnp.float32
BF16 = jnp.bfloat16

DEPTH = 2
GRID_W = 64
CONV_DIM = 512
CONV_WIDTH = 31
NA_HEADS = 8
NA_HEAD_DIM = 64
NA_DIM = NA_HEADS * NA_HEAD_DIM
NA_KH = 8
NA_KW = 16
C_HEADS = 16
C_KV_HEADS = 4
C_GROUP = C_HEADS // C_KV_HEADS
C_HEAD_DIM = 64
C_WINDOW = 128
C_BLOCK = 128
ROPE_BASE = 10000.0
N_GROUPS = 4
EXPERTS_PER_GROUP = 4
N_EXPERTS = N_GROUPS * EXPERTS_PER_GROUP
D_EXPERT = 256
LN_EPS = 1e-5
NEG_INF = -1e30
LOG2E = 1.4426950408889634

LANES = 128
SUBLANES = 8
VMEM_LIMIT_BYTES = 56 * 1024 * 1024

TOKEN_TILE = 1024
MOE_TILE = 1024
MOE_CHUNK = 256
MOE_SLOTS = MOE_TILE // MOE_CHUNK + N_GROUPS - 1
MOE_BASE_CHUNKS = MOE_TILE // MOE_CHUNK + 1
MOD_ROWS = 24
NA_ROWS_PER_STEP = 4
NA_UNION_ROWS = NA_KH + NA_ROWS_PER_STEP
CONV_CHUNK = 256
CONV_HALO = 16
ROUTER_LANES = LANES
GATE_LANE0 = N_GROUPS
ROUTER_ROWS = 24


def _cparams(semantics):
    return pltpu.CompilerParams(dimension_semantics=semantics, vmem_limit_bytes=VMEM_LIMIT_BYTES)


def _dot(a, b):
    return jnp.dot(a, b, preferred_element_type=F32)


def _dot_nt(a, b):
    return lax.dot_general(a, b, (((1,), (1,)), ((), ())), preferred_element_type=F32)


def _dot_tn(a, b):
    return lax.dot_general(a, b, (((0,), (0,)), ((), ())), preferred_element_type=F32)


def _split_bf16(a):
    hi = a.astype(BF16)
    lo = (a - hi.astype(F32)).astype(BF16)
    return hi, lo


def _dot3(a, b):
    a_hi, a_lo = _split_bf16(a)
    b_hi, b_lo = _split_bf16(b)
    return _dot(a_hi, b_hi) + (_dot(a_lo, b_hi) + _dot(a_hi, b_lo))


def _layer_norm(x, g, b):
    mu = jnp.mean(x, axis=-1, keepdims=True)
    xc = x - mu
    var = jnp.mean(xc * xc, axis=-1, keepdims=True)
    return xc * lax.rsqrt(var + LN_EPS) * g + b


def _silu(x):
    return x * jax.nn.sigmoid(x)


def _mod_kernel(cc_ref, w_ref, b_ref, o_ref):
    o_ref[0] = _dot3(_silu(cc_ref[...]), w_ref[0]) + b_ref[0]


def _modulation(cc, ada_w, ada_b):
    depth, d, n = ada_w.shape
    tn = n // 4
    return pl.pallas_call(
        _mod_kernel,
        grid=(depth, n // tn),
        in_specs=[
            pl.BlockSpec((MOD_ROWS, d), lambda i, j: (0, 0)),
            pl.BlockSpec((1, d, tn), lambda i, j: (i, 0, j)),
            pl.BlockSpec((1, 1, tn), lambda i, j: (i, 0, j)),
        ],
        out_specs=pl.BlockSpec((1, MOD_ROWS, tn), lambda i, j: (i, 0, j)),
        out_shape=jax.ShapeDtypeStruct((depth, MOD_ROWS, n), F32),
        compiler_params=_cparams(("arbitrary", "arbitrary")),
        name="adaln_mod",
    )(cc, ada_w, ada_b.reshape(depth, 1, n))


def _mod_spec(rows_per_group, group0, tm):
    return lambda i: (group0 + (i * tm) // rows_per_group, 0, 0)


def _modulate(h, m, shift_row, scale_row):
    return h * (1.0 + m[scale_row:scale_row + 1]) + m[shift_row:shift_row + 1]


def _inproj_ab_kernel(h_ref, mod_ref, w_ref, b_ref, g_ref, q_ref, k_ref, v_ref):
    u = _modulate(h_ref[...], mod_ref[0], 0, 1).astype(BF16)
    c = CONV_DIM
    za = _dot(u, w_ref[:, 0:c]) + b_ref[:, 0:c]
    zb = _dot(u, w_ref[:, c:2 * c]) + b_ref[:, c:2 * c]
    g_ref[...] = za * jax.nn.sigmoid(zb)
    q0 = 2 * c
    zq = _dot(u, w_ref[:, q0:q0 + NA_DIM]) + b_ref[:, q0:q0 + NA_DIM]
    q_ref[...] = (zq * (NA_HEAD_DIM ** -0.5 * LOG2E)).astype(BF16)
    k0 = q0 + NA_DIM
    k_ref[...] = (_dot(u, w_ref[:, k0:k0 + NA_DIM]) + b_ref[:, k0:k0 + NA_DIM]).astype(BF16)
    v0 = k0 + NA_DIM
    v_ref[...] = (_dot(u, w_ref[:, v0:v0 + NA_DIM]) + b_ref[:, v0:v0 + NA_DIM]).astype(BF16)


def _inproj_ab(h, mod, w, b, rows_per_group, group0):
    t, d = h.shape
    n = w.shape[1]
    tm = TOKEN_TILE
    row = lambda i: (i, 0)
    return pl.pallas_call(
        _inproj_ab_kernel,
        grid=(t // tm,),
        in_specs=[
            pl.BlockSpec((tm, d), row),
            pl.BlockSpec((1, 6, d), _mod_spec(rows_per_group, group0, tm)),
            pl.BlockSpec((d, n), lambda i: (0, 0), pipeline_mode=pl.Buffered(1)),
            pl.BlockSpec((1, n), lambda i: (0, 0)),
        ],
        out_specs=[
            pl.BlockSpec((tm, CONV_DIM), row),
            pl.BlockSpec((tm, NA_DIM), row),
            pl.BlockSpec((tm, NA_DIM), row),
            pl.BlockSpec((tm, NA_DIM), row),
        ],
        out_shape=[
            jax.ShapeDtypeStruct((t, CONV_DIM), F32),
            jax.ShapeDtypeStruct((t, NA_DIM), BF16),
            jax.ShapeDtypeStruct((t, NA_DIM), BF16),
            jax.ShapeDtypeStruct((t, NA_DIM), BF16),
        ],
        compiler_params=_cparams(("parallel",)),
        name="inproj_conv_na",
    )(h, mod, w, b.reshape(1, n))


def _conv_kernel(g_ref, w_ref, cb_ref, lg_ref, lb_ref, o_ref, pad_ref, *, seq):
    zeros = jnp.zeros((CONV_HALO, CONV_DIM), F32)
    pad_ref[0:CONV_HALO, :] = zeros
    pad_ref[CONV_HALO + seq:2 * CONV_HALO + seq, :] = zeros
    pad_ref[CONV_HALO:CONV_HALO + seq, :] = g_ref[0]
    first = CONV_HALO - CONV_WIDTH // 2
    ext = CONV_CHUNK + SUBLANES

    def chunk(i, carry):
        r0 = pl.multiple_of(i * CONV_CHUNK, CONV_CHUNK)
        acc = jnp.zeros((CONV_CHUNK, CONV_DIM), F32) + cb_ref[...]
        for res in range(SUBLANES):
            part = None
            for base in range(0, first + CONV_WIDTH, SUBLANES):
                tap = base + res - first
                if 0 <= tap < CONV_WIDTH:
                    term = pad_ref[pl.ds(r0 + base, ext), :] * w_ref[tap:tap + 1, :]
                    part = term if part is None else part + term
            acc = acc + part[res:res + CONV_CHUNK]
        y = _layer_norm(acc, lg_ref[...], lb_ref[...])
        o_ref[0, pl.ds(r0, CONV_CHUNK), :] = _silu(y).astype(BF16)
        return carry

    lax.fori_loop(0, seq // CONV_CHUNK, chunk, 0)


def _conv_module(g, conv_w, conv_b, ln_g, ln_b):
    bsz, seq, c = g.shape
    vec = lambda i: (0, 0)
    return pl.pallas_call(
        functools.partial(_conv_kernel, seq=seq),
        grid=(bsz,),
        in_specs=[
            pl.BlockSpec((1, seq, c), lambda i: (i, 0, 0)),
            pl.BlockSpec((CONV_WIDTH, c), vec),
            pl.BlockSpec((1, c), vec),
            pl.BlockSpec((1, c), vec),
            pl.BlockSpec((1, c), vec),
        ],
        out_specs=pl.BlockSpec((1, seq, c), lambda i: (i, 0, 0)),
        out_shape=jax.ShapeDtypeStruct((bsz, seq, c), BF16),
        scratch_shapes=[pltpu.VMEM((seq + 2 * CONV_HALO, c), F32)],
        compiler_params=_cparams(("parallel",)),
        name="conv_module",
    )(g, conv_w, conv_b.reshape(1, c), ln_g.reshape(1, c), ln_b.reshape(1, c))


N_ROW_OFFS = 2 * NA_KH - 1
N_COL_OFFS = 2 * NA_KW - 1


def _na_geometry(rows):
    rq, ru = NA_ROWS_PER_STEP, NA_UNION_ROWS
    geoms, types = [], []
    for step in range(rows // rq):
        r = step * rq + np.arange(rq)[:, None]
        key_row = np.clip(step * rq - NA_KH // 2, 0, rows - ru) + np.arange(ru)[None, :]
        row_start = np.clip(r - NA_KH // 2, 0, rows - NA_KH)
        row_in = (key_row >= row_start) & (key_row < row_start + NA_KH)
        geom = np.where(row_in, key_row - r + NA_KH - 1, N_ROW_OFFS).astype(np.int32)
        for t, other in enumerate(geoms):
            if np.array_equal(other, geom):
                types.append(t)
                break
        else:
            types.append(len(geoms))
            geoms.append(geom)
    return np.stack(geoms), np.asarray(types, np.int32)


def _na_bias_kernel(row_off_ref, rpb_ref, o_ref, tile_ref):
    head = pl.program_id(0)
    qc = lax.broadcasted_iota(jnp.int32, (GRID_W, GRID_W), 0)
    kc = lax.broadcasted_iota(jnp.int32, (GRID_W, GRID_W), 1)
    col_start = jnp.clip(qc - NA_KW // 2, 0, GRID_W - NA_KW)
    col_in = (kc >= col_start) & (kc < col_start + NA_KW)
    col_off = jnp.clip(kc - qc, -(NA_KW - 1), NA_KW - 1) + NA_KW - 1
    for a in range(N_ROW_OFFS):
        tile = jnp.zeros((GRID_W, GRID_W), F32)
        for b in range(N_COL_OFFS):
            tile = jnp.where(col_off == b, rpb_ref[(head * N_ROW_OFFS + a) * N_COL_OFFS + b], tile)
        tile_ref[a] = jnp.where(col_in, tile * LOG2E, NEG_INF)
    tile_ref[N_ROW_OFFS] = jnp.full((GRID_W, GRID_W), NEG_INF, F32)
    for typ in range(o_ref.shape[0]):
        for qr in range(NA_ROWS_PER_STEP):
            for kr in range(NA_UNION_ROWS):
                a = row_off_ref[(typ * NA_ROWS_PER_STEP + qr) * NA_UNION_ROWS + kr]
                o_ref[typ, 0, qr * GRID_W:(qr + 1) * GRID_W, kr * GRID_W:(kr + 1) * GRID_W] = tile_ref[a]


def _na_bias_tables(rpb, row_off):
    n_types = row_off.shape[0]
    tq = NA_ROWS_PER_STEP * GRID_W
    n_win = NA_UNION_ROWS * GRID_W
    return pl.pallas_call(
        _na_bias_kernel,
        grid=(NA_HEADS,),
        in_specs=[pl.BlockSpec(memory_space=pltpu.SMEM), pl.BlockSpec(memory_space=pltpu.SMEM)],
        out_specs=pl.BlockSpec((n_types, 1, tq, n_win), lambda h: (0, h, 0, 0)),
        out_shape=jax.ShapeDtypeStruct((n_types, NA_HEADS, tq, n_win), F32),
        scratch_shapes=[pltpu.VMEM((N_ROW_OFFS + 1, GRID_W, GRID_W), F32)],
        compiler_params=_cparams(("parallel",)),
        name="na_bias_table",
    )(jnp.asarray(row_off.reshape(-1)), rpb.astype(F32).reshape(-1))


def _softmax_rows(scores):
    m = functools.reduce(jnp.maximum, [jnp.max(s, axis=1, keepdims=True) for s in scores])
    ps = [jnp.exp2(s - m) for s in scores]
    denom = functools.reduce(jnp.add, [jnp.sum(p, axis=1, keepdims=True) for p in ps])
    return [p.astype(BF16) for p in ps], denom


def _pv_rows(ps, denom, values):
    return functools.reduce(jnp.add, [_dot(p, v) for p, v in zip(ps, values)]) / denom


def _na_kernel(type_ref, q_ref, k_ref, v_ref, kc_ref, vc_ref, bias_ref, o_ref, *, rows):
    del type_ref
    step = pl.program_id(1)
    start_row = jnp.clip(step * NA_ROWS_PER_STEP - NA_KH // 2, 0, rows - NA_UNION_ROWS)
    start = pl.multiple_of(start_row * GRID_W, GRID_W)
    n_win = NA_UNION_ROWS * GRID_W
    heads = [slice(h * NA_HEAD_DIM, (h + 1) * NA_HEAD_DIM) for h in range(NA_HEADS)]
    scores = []
    for h, sl in enumerate(heads):
        keys = jnp.concatenate([k_ref[0, pl.ds(start, n_win), sl], kc_ref[0, :, sl]], axis=0)
        s = _dot_nt(q_ref[0, :, sl], keys)
        scores.append([jnp.concatenate([s[:, :n_win] + bias_ref[0, h], s[:, n_win:]], axis=1)])
    probs = [_softmax_rows(s) for s in scores]
    for sl, (ps, denom) in zip(heads, probs):
        values = jnp.concatenate([v_ref[0, pl.ds(start, n_win), sl], vc_ref[0, :, sl]], axis=0)
        o_ref[0, :, sl] = _pv_rows(ps, denom, [values]).astype(BF16)


def _neighbourhood_attention(q, k, v, k_ctx, v_ctx, bias, types):
    bsz, seq, dim = q.shape
    n_ctx = k_ctx.shape[1]
    rows = seq // GRID_W
    tq = NA_ROWS_PER_STEP * GRID_W
    n_win = NA_UNION_ROWS * GRID_W
    full = lambda b, s, t: (b, 0, 0)
    grid_spec = pltpu.PrefetchScalarGridSpec(
        num_scalar_prefetch=1,
        grid=(bsz, seq // tq),
        in_specs=[
            pl.BlockSpec((1, tq, dim), lambda b, s, t: (b, s, 0)),
            pl.BlockSpec((1, seq, dim), full),
            pl.BlockSpec((1, seq, dim), full),
            pl.BlockSpec((1, n_ctx, dim), full),
            pl.BlockSpec((1, n_ctx, dim), full),
            pl.BlockSpec((1, NA_HEADS, tq, n_win), lambda b, s, t: (t[s], 0, 0, 0)),
        ],
        out_specs=pl.BlockSpec((1, tq, dim), lambda b, s, t: (b, s, 0)),
    )
    return pl.pallas_call(
        functools.partial(_na_kernel, rows=rows),
        grid_spec=grid_spec,
        out_shape=jax.ShapeDtypeStruct((bsz, seq, dim), BF16),
        compiler_params=_cparams(("parallel", "arbitrary")),
        name="neighbourhood_attn",
    )(jnp.asarray(types), q, k, v, k_ctx, v_ctx, bias)


def _ctx_attn_kernel(q_ref, k_ref, v_ref, o_ref):
    for h in range(NA_HEADS):
        sl = slice(h * NA_HEAD_DIM, (h + 1) * NA_HEAD_DIM)
        ps, denom = _softmax_rows([_dot_nt(q_ref[0, :, sl], k_ref[0, :, sl])])
        o_ref[0, :, sl] = _pv_rows(ps, denom, [v_ref[0, :, sl]]).astype(BF16)


def _context_attention(q, k, v):
    bsz, n, dim = q.shape
    spec = pl.BlockSpec((1, n, dim), lambda b: (b, 0, 0))
    return pl.pallas_call(
        _ctx_attn_kernel,
        grid=(bsz,),
        in_specs=[spec, spec, spec],
        out_specs=spec,
        out_shape=jax.ShapeDtypeStruct((bsz, n, dim), BF16),
        compiler_params=_cparams(("parallel",)),
        name="context_attn",
    )(q, k, v)


def _rope(x, cos, sin_signed):
    n = x.shape[1]
    half = C_HEAD_DIM // 4
    lane = lax.broadcasted_iota(jnp.int32, x.shape, 1)
    partner = jnp.where(lane % (2 * half) < half, pltpu.roll(x, n - half, 1), pltpu.roll(x, half, 1))
    reps = n // LANES
    cos_full = jnp.concatenate([cos] * reps, axis=1)
    sin_full = jnp.concatenate([sin_signed] * reps, axis=1)
    return x * cos_full + partner * sin_full


def _inproj_gqa_kernel(h_ref, mod_ref, w_ref, cos_ref, sin_ref, q_ref, k_ref, v_ref):
    u = _modulate(h_ref[...], mod_ref[0], 0, 1).astype(BF16)
    qn = C_HEADS * C_HEAD_DIM
    kn = C_KV_HEADS * C_HEAD_DIM
    cos, sin = cos_ref[...], sin_ref[...]
    q = _rope(_dot(u, w_ref[:, 0:qn]), cos, sin)
    q_ref[...] = (q * (C_HEAD_DIM ** -0.5 * LOG2E)).astype(BF16)
    k_ref[...] = _rope(_dot(u, w_ref[:, qn:qn + kn]), cos, sin).astype(BF16)
    v_ref[...] = _dot(u, w_ref[:, qn + kn:qn + 2 * kn]).astype(BF16)


def _inproj_gqa(h, mod, w, cos, sin, seq):
    t, d = h.shape
    n = w.shape[1]
    tm = TOKEN_TILE
    qn = C_HEADS * C_HEAD_DIM
    kn = C_KV_HEADS * C_HEAD_DIM
    row = lambda i: (i, 0)
    pos = lambda i: (i % (seq // tm), 0)
    return pl.pallas_call(
        _inproj_gqa_kernel,
        grid=(t // tm,),
        in_specs=[
            pl.BlockSpec((tm, d), row),
            pl.BlockSpec((1, 6, d), _mod_spec(seq, 0, tm)),
            pl.BlockSpec((d, n), lambda i: (0, 0), pipeline_mode=pl.Buffered(1)),
            pl.BlockSpec((tm, LANES), pos),
            pl.BlockSpec((tm, LANES), pos),
        ],
        out_specs=[pl.BlockSpec((tm, qn), row), pl.BlockSpec((tm, kn), row), pl.BlockSpec((tm, kn), row)],
        out_shape=[
            jax.ShapeDtypeStruct((t, qn), BF16),
            jax.ShapeDtypeStruct((t, kn), BF16),
            jax.ShapeDtypeStruct((t, kn), BF16),
        ],
        compiler_params=_cparams(("parallel",)),
        name="inproj_gqa",
    )(h, mod, w, cos, sin)


def _inproj_kv_kernel(h_ref, mod_ref, w_ref, k_ref, v_ref):
    u = _modulate(h_ref[...], mod_ref[0], 0, 1).astype(BF16)
    kn = C_KV_HEADS * C_HEAD_DIM
    k_ref[...] = _dot(u, w_ref[:, 0:kn]).astype(BF16)
    v_ref[...] = _dot(u, w_ref[:, kn:2 * kn]).astype(BF16)


def _inproj_kv(h, mod, w, rows_per_group, group0):
    t, d = h.shape
    n = w.shape[1]
    tm = TOKEN_TILE
    kn = C_KV_HEADS * C_HEAD_DIM
    row = lambda i: (i, 0)
    return pl.pallas_call(
        _inproj_kv_kernel,
        grid=(t // tm,),
        in_specs=[
            pl.BlockSpec((tm, d), row),
            pl.BlockSpec((1, 6, d), _mod_spec(rows_per_group, group0, tm)),
            pl.BlockSpec((d, n), lambda i: (0, 0), pipeline_mode=pl.Buffered(1)),
        ],
        out_specs=[pl.BlockSpec((tm, kn), row), pl.BlockSpec((tm, kn), row)],
        out_shape=[jax.ShapeDtypeStruct((t, kn), BF16), jax.ShapeDtypeStruct((t, kn), BF16)],
        compiler_params=_cparams(("parallel",)),
        name="inproj_ctx_kv",
    )(h, mod, w)


def _rope_tables(seq):
    t = jnp.arange(seq)
    row = (t // GRID_W).astype(F32)
    col = (t % GRID_W).astype(F32)
    axis_dim = C_HEAD_DIM // 2
    inv_freq = ROPE_BASE ** (-jnp.arange(0, axis_dim, 2, dtype=F32) / axis_dim)
    ang_r = row[:, None] * inv_freq
    ang_c = col[:, None] * inv_freq
    cos = jnp.concatenate([jnp.cos(ang_r)] * 2 + [jnp.cos(ang_c)] * 2, axis=-1)
    sin = jnp.concatenate([-jnp.sin(ang_r), jnp.sin(ang_r), -jnp.sin(ang_c), jnp.sin(ang_c)], axis=-1)
    reps = LANES // C_HEAD_DIM
    return jnp.tile(cos, (1, reps)), jnp.tile(sin, (1, reps))


def _gqa_kernel(sink_ref, q_ref, k_ref, v_ref, kc_ref, vc_ref, o_ref, *, seq):
    blk = pl.program_id(1)
    span = C_BLOCK + 2 * C_WINDOW
    start = pl.multiple_of(jnp.clip(blk * C_BLOCK - C_WINDOW, 0, seq - span), C_BLOCK)
    k_pos = start + lax.broadcasted_iota(jnp.int32, (span, C_BLOCK), 0)
    q_pos = blk * C_BLOCK + lax.broadcasted_iota(jnp.int32, (span, C_BLOCK), 1)
    valid = jnp.abs(q_pos - k_pos) <= C_WINDOW
    valid = jnp.concatenate([valid] * C_GROUP, axis=1)
    d = C_HEAD_DIM

    def scores(hk):
        q0 = hk * C_GROUP * d
        qs = jnp.concatenate([q_ref[0, :, q0 + g * d:q0 + (g + 1) * d] for g in range(C_GROUP)], axis=0)
        ksl = slice(hk * d, (hk + 1) * d)
        keys = jnp.concatenate([k_ref[0, pl.ds(start, span), ksl], kc_ref[0, :, ksl]], axis=0)
        s = _dot_nt(keys, qs)
        return jnp.where(valid, s[:span], NEG_INF), s[span:]

    def softmax(hk, s):
        s_lat, s_ctx = s
        sink = jnp.concatenate(
            [jnp.full((1, C_BLOCK), sink_ref[hk * C_GROUP + g] * LOG2E, F32) for g in range(C_GROUP)], axis=1)
        m = jnp.maximum(jnp.maximum(jnp.max(s_lat, axis=0, keepdims=True), jnp.max(s_ctx, axis=0, keepdims=True)),
                        sink)
        p_lat = jnp.exp2(s_lat - m)
        p_ctx = jnp.exp2(s_ctx - m)
        denom = jnp.sum(p_lat, axis=0, keepdims=True) + jnp.sum(p_ctx, axis=0, keepdims=True) + jnp.exp2(sink - m)
        inv = 1.0 / denom
        return (p_lat * inv).astype(BF16), (p_ctx * inv).astype(BF16)

    def values(hk, p):
        ksl = slice(hk * d, (hk + 1) * d)
        q0 = hk * C_GROUP * d
        vals = jnp.concatenate([v_ref[0, pl.ds(start, span), ksl], vc_ref[0, :, ksl]], axis=0)
        out = _dot_tn(jnp.concatenate([p[0], p[1]], axis=0), vals)
        for g in range(C_GROUP):
            o_ref[0, :, q0 + g * d:q0 + (g + 1) * d] = out[g * C_BLOCK:(g + 1) * C_BLOCK].astype(BF16)

    s = {0: scores(0), 1: scores(1)}
    p = {0: softmax(0, s[0])}
    for hk in range(C_KV_HEADS):
        if hk + 2 < C_KV_HEADS:
            s[hk + 2] = scores(hk + 2)
        if hk + 1 < C_KV_HEADS:
            p[hk + 1] = softmax(hk + 1, s[hk + 1])
        values(hk, p[hk])


def _window_attention(q, k, v, k_ctx, v_ctx, sink):
    bsz, seq, qn = q.shape
    kn = k.shape[2]
    n_ctx = k_ctx.shape[1]
    full = lambda b, s: (b, 0, 0)
    return pl.pallas_call(
        functools.partial(_gqa_kernel, seq=seq),
        grid=(bsz, seq // C_BLOCK),
        in_specs=[
            pl.BlockSpec(memory_space=pltpu.SMEM),
            pl.BlockSpec((1, C_BLOCK, qn), lambda b, s: (b, s, 0)),
            pl.BlockSpec((1, seq, kn), full),
            pl.BlockSpec((1, seq, kn), full),
            pl.BlockSpec((1, n_ctx, kn), full),
            pl.BlockSpec((1, n_ctx, kn), full),
        ],
        out_specs=pl.BlockSpec((1, C_BLOCK, qn), lambda b, s: (b, s, 0)),
        out_shape=jax.ShapeDtypeStruct((bsz, seq, qn), BF16),
        compiler_params=_cparams(("parallel", "arbitrary")),
        name="window_gqa_attn",
    )(sink, q, k, v, k_ctx, v_ctx)


def _route(logits):
    row = lax.broadcasted_iota(jnp.int32, logits.shape, 0)
    big = jnp.int32(ROUTER_ROWS)
    lg = jnp.where(row < N_GROUPS, logits, NEG_INF)
    g_max = jnp.max(lg, axis=0, keepdims=True)
    g_prob = 1.0 / jnp.sum(jnp.exp(lg - g_max), axis=0, keepdims=True)
    g_idx = jnp.min(jnp.where(lg == g_max, row, big), axis=0, keepdims=True)
    first = GATE_LANE0 + g_idx * EXPERTS_PER_GROUP
    in_group = (row >= first) & (row < first + EXPERTS_PER_GROUP)
    le = jnp.where(in_group, logits, NEG_INF)
    e1 = jnp.max(le, axis=0, keepdims=True)
    i1 = jnp.min(jnp.where(le == e1, row, big), axis=0, keepdims=True)
    le2 = jnp.where(row == i1, NEG_INF, le)
    e2 = jnp.max(le2, axis=0, keepdims=True)
    i2 = jnp.min(jnp.where(le2 == e2, row, big), axis=0, keepdims=True)
    r = jnp.exp(e2 - e1)
    w1 = g_prob / (1.0 + r)
    w2 = w1 * r
    own = [jnp.where(i1 == first + j, w1, jnp.where(i2 == first + j, w2, 0.0)) for j in range(EXPERTS_PER_GROUP)]
    return g_idx, own


def _post_mix_kernel(*refs, n_in, alpha):
    a_refs = refs[:n_in]
    w_refs = refs[n_in:2 * n_in]
    b_ref, h_ref, mod_ref, lg_ref, lb_ref, r_ref, h1_ref, t_ref, gate_ref, gid_ref = refs[2 * n_in:]
    o = functools.reduce(jnp.add, [_dot(a[...], w[...]) for a, w in zip(a_refs, w_refs)]) + b_ref[...]
    m = mod_ref[0]
    h1 = _layer_norm(alpha * h_ref[...] + m[2:3] * o, lg_ref[...], lb_ref[...])
    h1_ref[...] = h1
    t = _modulate(h1, m, 3, 4).astype(BF16)
    t_ref[...] = t
    group, own = _route(_dot_nt(r_ref[...], t))
    parts = _split3(jnp.concatenate(own, axis=0))
    rows = jnp.concatenate([p.astype(F32) for p in parts], axis=0)
    pad = jnp.zeros((ROUTER_LANES - rows.shape[0], rows.shape[1]), F32)
    gate_ref[...] = jnp.transpose(jnp.concatenate([rows, pad], axis=0)).astype(BF16)
    row = lax.broadcasted_iota(jnp.int32, gid_ref.shape[1:], 0)
    count = jnp.sum((row - 1 == group).astype(F32), axis=1, keepdims=True).astype(jnp.int32)
    gid_ref[0] = jnp.where(row == 0, group, count)


def _post_mix(acts, weights, bias, h, mod, ln_g, ln_b, router, rows_per_group, group0, alpha):
    t, d = h.shape
    tm = TOKEN_TILE
    row = lambda i: (i, 0)
    const = lambda i: (0, 0)
    in_specs = [pl.BlockSpec((tm, a.shape[1]), row) for a in acts]
    in_specs += [pl.BlockSpec(w.shape, const, pipeline_mode=pl.Buffered(1)) for w in weights]
    in_specs += [
        pl.BlockSpec((1, d), const),
        pl.BlockSpec((tm, d), row),
        pl.BlockSpec((1, 6, d), _mod_spec(rows_per_group, group0, tm)),
        pl.BlockSpec((1, d), const),
        pl.BlockSpec((1, d), const),
        pl.BlockSpec((ROUTER_ROWS, d), const),
    ]
    return pl.pallas_call(
        functools.partial(_post_mix_kernel, n_in=len(acts), alpha=alpha),
        grid=(t // tm,),
        in_specs=in_specs,
        out_specs=[pl.BlockSpec((tm, d), row), pl.BlockSpec((tm, d), row), pl.BlockSpec((tm, ROUTER_LANES), row),
                   pl.BlockSpec((1, SUBLANES, tm), lambda i: (i, 0, 0))],
        out_shape=[
            jax.ShapeDtypeStruct((t, d), F32),
            jax.ShapeDtypeStruct((t, d), BF16),
            jax.ShapeDtypeStruct((t, ROUTER_LANES), BF16),
            jax.ShapeDtypeStruct((t // tm, SUBLANES, tm), jnp.int32),
        ],
        compiler_params=_cparams(("parallel",)),
        name="outproj_ln_router",
    )(*acts, *weights, bias.reshape(1, d), h, mod, ln_g.reshape(1, d), ln_b.reshape(1, d), router)


def _split3(a):
    hi = a.astype(BF16)
    rest = a - hi.astype(F32)
    mid = rest.astype(BF16)
    return hi, mid, (rest - mid.astype(F32)).astype(BF16)


def _expert_group(x, gate_cols, wg_ref, wu_ref, wd_ref, hid_ref):
    f = D_EXPERT
    for j, gate in enumerate(gate_cols):
        hid = _silu(_dot(x, wg_ref[j])) * _dot(x, wu_ref[j]) * gate
        hid_ref[:, f * j:f * (j + 1)] = hid.astype(BF16)
    return _dot(hid_ref[...], wd_ref[0])


def _expert_weights(w_gate, w_up, w_down):
    e, f, d = w_down.shape
    return (w_gate.astype(BF16), w_up.astype(BF16),
            w_down.astype(BF16).reshape(N_GROUPS, EXPERTS_PER_GROUP * f, d))


def _moe_plan(counts):
    chunks = (counts + MOE_CHUNK - 1) // MOE_CHUNK
    ends = jnp.cumsum(chunks, axis=1)
    slot = jnp.arange(MOE_SLOTS, dtype=jnp.int32)
    chunk_group = jnp.minimum(jnp.sum(ends[:, None, :] <= slot[None, :, None], axis=2), N_GROUPS - 1)
    start_row = (ends - chunks) * MOE_CHUNK
    return (chunk_group.astype(jnp.int32).reshape(-1), ends[:, -1].astype(jnp.int32),
            start_row.astype(jnp.int32).reshape(-1))


def _moe_kernel(chunk_group_ref, n_chunks_ref, start_ref, t_ref, gate_ref, gid_ref, h_ref, mod_ref, wg_ref,
                wu_ref, wd_ref, lg_ref, lb_ref, o_ref, perm_ref, xp_ref, gp_ref, yp_ref, hid_ref, *, alpha):
    tile = pl.program_id(0)
    step = pl.program_id(1)
    tm = t_ref.shape[0]

    @pl.when(step == 0)
    def _():
        gid = jnp.concatenate([gid_ref[i, 0:1, :] for i in range(gid_ref.shape[0])], axis=1)
        is_grp = lax.broadcasted_iota(jnp.int32, (SUBLANES, tm), 0) == gid
        upper = (lax.broadcasted_iota(jnp.int32, (tm, tm), 0) <= lax.broadcasted_iota(jnp.int32, (tm, tm), 1))
        count = _dot(is_grp.astype(BF16), upper.astype(BF16))
        rank = jnp.sum(jnp.where(is_grp, count, 0.0), axis=0, keepdims=True).astype(jnp.int32) - 1
        start = jnp.zeros_like(gid)
        for g in range(N_GROUPS):
            start = jnp.where(gid == g, start_ref[tile * N_GROUPS + g], start)
        dest = start + rank
        perm_ref[...] = (lax.broadcasted_iota(jnp.int32, (perm_ref.shape[0], tm), 0) == dest).astype(BF16)

        def move(rows):
            perm = perm_ref[rows, :]
            xp_ref[rows, :] = _dot(perm, t_ref[...]).astype(BF16)
            gp_ref[rows, :] = _dot(perm, gate_ref[...])

        move(slice(0, MOE_BASE_CHUNKS * MOE_CHUNK))
        for s in range(MOE_BASE_CHUNKS, MOE_SLOTS):
            pl.when(s < n_chunks_ref[tile])(functools.partial(move, slice(s * MOE_CHUNK, (s + 1) * MOE_CHUNK)))

    @pl.when(step < n_chunks_ref[tile])
    def _():
        rows = pl.ds(pl.multiple_of(step * MOE_CHUNK, MOE_CHUNK), MOE_CHUNK)
        gp = gp_ref[rows, :]
        e = EXPERTS_PER_GROUP
        cols = [gp[:, j:j + 1] + gp[:, e + j:e + j + 1] + gp[:, 2 * e + j:2 * e + j + 1] for j in range(e)]
        yp_ref[rows, :] = _expert_group(xp_ref[rows, :], cols, wg_ref, wu_ref, wd_ref, hid_ref).astype(BF16)

    @pl.when((step >= n_chunks_ref[tile]) & (step < MOE_BASE_CHUNKS))
    def _():
        rows = pl.ds(pl.multiple_of(step * MOE_CHUNK, MOE_CHUNK), MOE_CHUNK)
        yp_ref[rows, :] = jnp.zeros((MOE_CHUNK, yp_ref.shape[1]), BF16)

    @pl.when(step == MOE_SLOTS - 1)
    def _():
        base = slice(0, MOE_BASE_CHUNKS * MOE_CHUNK)
        o_ref[...] = _dot_tn(perm_ref[base, :], yp_ref[base, :])
        for s in range(MOE_BASE_CHUNKS, MOE_SLOTS):
            @pl.when(s < n_chunks_ref[tile])
            def _():
                rows = slice(s * MOE_CHUNK, (s + 1) * MOE_CHUNK)
                o_ref[...] += _dot_tn(perm_ref[rows, :], yp_ref[rows, :])
        o_ref[...] = _layer_norm(alpha * h_ref[...] + mod_ref[0][5:6] * o_ref[...], lg_ref[...], lb_ref[...])


def _moe_ln(t_act, gates, gid, h, mod, wg, wu, wd, ln_g, ln_b, rows_per_group, group0, alpha):
    t, d = h.shape
    tm = MOE_TILE
    f = wg.shape[2]
    rows = MOE_SLOTS * MOE_CHUNK
    row = lambda i, s, cg, nc, st: (i, 0)
    const = lambda i, s, cg, nc, st: (0, 0)
    group_w = lambda i, s, cg, nc, st: (cg[i * MOE_SLOTS + s], 0, 0)
    mod_idx = _mod_spec(rows_per_group, group0, tm)
    counts = gid[:, 1:1 + N_GROUPS, 0].reshape(t // tm, tm // TOKEN_TILE, N_GROUPS).sum(axis=1)
    plan = _moe_plan(counts)
    grid_spec = pltpu.PrefetchScalarGridSpec(
        num_scalar_prefetch=3,
        grid=(t // tm, MOE_SLOTS),
        in_specs=[
            pl.BlockSpec((tm, d), row),
            pl.BlockSpec((tm, ROUTER_LANES), row),
            pl.BlockSpec((tm // TOKEN_TILE, SUBLANES, TOKEN_TILE), lambda i, s, cg, nc, st: (i, 0, 0)),
            pl.BlockSpec((tm, d), row),
            pl.BlockSpec((1, 6, d), lambda i, s, cg, nc, st: mod_idx(i)),
            pl.BlockSpec((EXPERTS_PER_GROUP, d, f), group_w),
            pl.BlockSpec((EXPERTS_PER_GROUP, d, f), group_w),
            pl.BlockSpec((1,) + wd.shape[1:], group_w),
            pl.BlockSpec((1, d), const),
            pl.BlockSpec((1, d), const),
        ],
        out_specs=pl.BlockSpec((tm, d), row),
        scratch_shapes=[
            pltpu.VMEM((rows, tm), BF16),
            pltpu.VMEM((rows, d), BF16),
            pltpu.VMEM((rows, ROUTER_LANES), F32),
            pltpu.VMEM((rows, d), BF16),
            pltpu.VMEM((MOE_CHUNK, wd.shape[1]), BF16),
        ],
    )
    return pl.pallas_call(
        functools.partial(_moe_kernel, alpha=alpha),
        grid_spec=grid_spec,
        out_shape=jax.ShapeDtypeStruct((t, d), F32),
        compiler_params=_cparams(("parallel", "arbitrary")),
        name="moe_ln",
    )(*plan, t_act, gates, gid, h, mod, wg, wu, wd, ln_g.reshape(1, d), ln_b.reshape(1, d))


def _router_matrix(router_group, router_expert):
    d = router_group.shape[0]
    pad = jnp.zeros((d, ROUTER_ROWS - N_GROUPS - N_EXPERTS), F32)
    return jnp.concatenate([router_group, router_expert, pad], axis=1).T.astype(BF16)


def kernel(x, c, ctx, c_ctx, ada_w, ada_b, ln_g, ln_b, ab_w_in, ab_b_in, conv_w, conv_b, conv_ln_g, conv_ln_b,
           na_rpb, ab_w_out, ab_b_out, gqa_w_in, gqa_sink, gqa_w_out, router_group, router_expert, exp_w_gate,
           exp_w_up, exp_w_down):
    bsz, seq, d = x.shape
    n_ctx = ctx.shape[1]
    depth = ada_w.shape[0]
    assert depth == DEPTH and bsz + 1 <= MOD_ROWS
    assert seq % MOE_TILE == 0 and (bsz * n_ctx) % MOE_TILE == 0 and MOE_TILE % TOKEN_TILE == 0
    assert seq % C_BLOCK == 0 and seq % CONV_CHUNK == 0 and n_ctx % CONV_CHUNK == 0
    rows = seq // GRID_W
    assert rows % NA_ROWS_PER_STEP == 0 and rows >= NA_UNION_ROWS
    alpha = (2.0 * depth) ** 0.25
    t_lat, t_ctx = bsz * seq, bsz * n_ctx

    cc = jnp.concatenate([c, c_ctx[None], jnp.zeros((MOD_ROWS - bsz - 1, d), F32)], axis=0)
    mod = _modulation(cc, ada_w, ada_b).reshape(depth, MOD_ROWS, 6, d)
    lat_grp = dict(rows_per_group=seq, group0=0)
    ctx_grp = dict(rows_per_group=t_ctx, group0=bsz)

    h_lat = x.reshape(t_lat, d)
    h_ctx = ctx.reshape(t_ctx, d)
    for i in range(depth):
        j = i // 2
        need_ctx = i < depth - 1
        router = _router_matrix(router_group[i], router_expert[i])
        wg, wu, wd = _expert_weights(exp_w_gate[i], exp_w_up[i], exp_w_down[i])
        if i % 2 == 0:
            w_in = ab_w_in[j].astype(BF16)
            g_lat, q_lat, k_lat, v_lat = _inproj_ab(h_lat, mod[i], w_in, ab_b_in[j], **lat_grp)
            g_ctx, q_ctx, k_ctx, v_ctx = _inproj_ab(h_ctx, mod[i], w_in, ab_b_in[j], **ctx_grp)
            to_seq = lambda a, n: a.reshape(bsz, n, a.shape[-1])
            conv_args = (conv_w[j], conv_b[j], conv_ln_g[j], conv_ln_b[j])
            conv_lat = _conv_module(to_seq(g_lat, seq), *conv_args).reshape(t_lat, CONV_DIM)
            k_ctx, v_ctx = to_seq(k_ctx, n_ctx), to_seq(v_ctx, n_ctx)
            row_off, types = _na_geometry(rows)
            bias = _na_bias_tables(na_rpb[j], row_off)
            na_lat = _neighbourhood_attention(to_seq(q_lat, seq), to_seq(k_lat, seq), to_seq(v_lat, seq), k_ctx,
                                              v_ctx, bias, types).reshape(t_lat, NA_DIM)
            w_out = ab_w_out[j].astype(BF16)
            w_outs = [w_out[:CONV_DIM], w_out[CONV_DIM:]]
            b_out = ab_b_out[j]
            acts_lat = [conv_lat, na_lat]
            if need_ctx:
                conv_ctx = _conv_module(to_seq(g_ctx, n_ctx), *conv_args).reshape(t_ctx, CONV_DIM)
                na_ctx = _context_attention(to_seq(q_ctx, n_ctx), k_ctx, v_ctx).reshape(t_ctx, NA_DIM)
                acts_ctx = [conv_ctx, na_ctx]
        else:
            assert not need_ctx
            qn = C_HEADS * C_HEAD_DIM
            w_in = gqa_w_in[j].astype(BF16)
            cos, sin = _rope_tables(seq)
            q_lat, k_lat, v_lat = _inproj_gqa(h_lat, mod[i], w_in, cos, sin, seq)
            k_ctx, v_ctx = _inproj_kv(h_ctx, mod[i], w_in[:, qn:], **ctx_grp)
            to_seq = lambda a, n: a.reshape(bsz, n, a.shape[-1])
            att = _window_attention(to_seq(q_lat, seq), to_seq(k_lat, seq), to_seq(v_lat, seq),
                                    to_seq(k_ctx, n_ctx), to_seq(v_ctx, n_ctx), gqa_sink[j])
            acts_lat = [att.reshape(t_lat, qn)]
            w_outs = [gqa_w_out[j].astype(BF16)]
            b_out = jnp.zeros((d,), F32)
        ln1 = (ln_g[i, 0], ln_b[i, 0])
        ln2 = (ln_g[i, 1], ln_b[i, 1])
        h1, t_act, gates, gid = _post_mix(acts_lat, w_outs, b_out, h_lat, mod[i], *ln1, router, alpha=alpha, **lat_grp)
        h_lat = _moe_ln(t_act, gates, gid, h1, mod[i], wg, wu, wd, *ln2, alpha=alpha, **lat_grp)
        if need_ctx:
            h1, t_act, gates, gid = _post_mix(acts_ctx, w_outs, b_out, h_ctx, mod[i], *ln1, router, alpha=alpha,
                                         **ctx_grp)
            h_ctx = _moe_ln(t_act, gates, gid, h1, mod[i], wg, wu, wd, *ln2, alpha=alpha, **ctx_grp)
    return h_lat.reshape(bsz, seq, d)
```
